```python
import math
import jax, jax.numpy as jnp
from jax import lax
import numpy as np

D_MODEL = 1024
BATCH = 1
SEQ = 16384
DEPTH = 2
DEC_BATCH = 16
DEC_SEQ = 64
PAST_LEN = 4096

CHUNK = 64
N_A = DEPTH // 2
N_B = DEPTH - N_A
ML_HEADS = 4
ML_HEAD_DIM = D_MODEL // ML_HEADS
ML_WIDTH = ML_HEADS * ML_HEAD_DIM
DA_HEAD_DIM = 64
DA_V_DIM = 2 * DA_HEAD_DIM
DA_HEADS = D_MODEL // DA_V_DIM
DA_QK_WIDTH = DA_HEADS * 2 * DA_HEAD_DIM
DA_V_WIDTH = DA_HEADS * DA_V_DIM
MEM_LEN = 256
MEM_HEADS = 4
MEM_HEAD_DIM = 128
MEM_WIDTH = MEM_HEADS * MEM_HEAD_DIM
A_IN = 5 * ML_WIDTH + 2 * ML_HEADS + 2 * MEM_WIDTH
B_IN = DA_QK_WIDTH + DA_V_WIDTH + 2 * MEM_WIDTH
A_SPLITS = [ML_WIDTH, 2 * ML_WIDTH, 3 * ML_WIDTH, 4 * ML_WIDTH, 5 * ML_WIDTH,
            5 * ML_WIDTH + 2 * ML_HEADS, 5 * ML_WIDTH + 2 * ML_HEADS + MEM_WIDTH]
B_SPLITS = [DA_QK_WIDTH, DA_QK_WIDTH + DA_V_WIDTH, DA_QK_WIDTH + DA_V_WIDTH + MEM_WIDTH]
Q_BLOCK = 128
EPS = 1e-6

kernel_name = "yoco_mlstm_diffattn_stream_step"


def _rmsnorm(x, g):
    x32 = x.astype(jnp.float32)
    y = x32 * lax.rsqrt(jnp.mean(x32 * x32, axis=-1, keepdims=True) + EPS)
    return (y * g.astype(jnp.float32)).astype(x.dtype)


def _mem_attend(q, mem_k, mem_v):
    s = jnp.einsum('bthd,bmhd->bhtm', q, mem_k).astype(jnp.float32) * (MEM_HEAD_DIM ** -0.5)
    p = jax.nn.softmax(s, axis=-1).astype(mem_v.dtype)
    return jnp.einsum('bhtm,bmhd->bthd', p, mem_v)


def _mlstm_chunk(carry, xs):
    C, n, m = carry
    q, k, v, ig, lf = xs
    L = q.shape[2]
    g = jnp.cumsum(lf, axis=-1)
    causal = jnp.tril(jnp.ones((L, L), dtype=bool))
    dmat = jnp.where(causal, g[..., :, None] - g[..., None, :] + ig[..., None, :], -jnp.inf)
    inter = g + m[..., None]
    m_t = jnp.maximum(inter, jnp.max(dmat, axis=-1))
    w_intra = jnp.exp(dmat - m_t[..., None])
    w_inter = jnp.exp(inter - m_t)
    s = w_intra * jnp.einsum('bhtd,bhsd->bhts', q, k)
    num = w_inter[..., None] * jnp.einsum('bhvk,bhtk->bhtv', C, q) + jnp.einsum('bhts,bhsv->bhtv', s, v)
    den = w_inter * jnp.einsum('bhk,bhtk->bht', n, q) + jnp.sum(s, axis=-1)
    h = num / jnp.maximum(jnp.abs(den), jnp.exp(-m_t))[..., None]
    g_last = g[..., -1]
    m_new = m_t[..., -1]
    w_s = jnp.exp(g_last[..., None] - g + ig - m_new[..., None])
    dec = jnp.exp(g_last + m - m_new)
    C_new = dec[..., None, None] * C + jnp.einsum('bhs,bhsv,bhsk->bhvk', w_s, v, k)
    n_new = dec[..., None] * n + jnp.einsum('bhs,bhsk->bhk', w_s, k)
    return (C_new, n_new, m_new), h


def _mlstm_layer(x, C0, n0, m0, mem_k, mem_v, norm_g, w_in, b_gate, head_g, w_out):
    B, T, _ = x.shape
    H, dh = ML_HEADS, ML_HEAD_DIM
    f32 = jnp.float32
    p = _rmsnorm(x, norm_g) @ w_in
    q, k, v, o, z, gates, mq, mz = jnp.split(p, A_SPLITS, axis=-1)
    gates = gates.astype(f32) + b_gate.astype(f32)
    ig = gates[..., :H]
    lf = jax.nn.log_sigmoid(gates[..., H:])
    L = min(T, CHUNK)
    nc = T // L

    def to_chunks(a):
        return a.reshape(B, nc, L, H, -1).transpose(1, 0, 3, 2, 4)

    qc = to_chunks(q.astype(f32))
    kc = to_chunks(k.astype(f32)) * (dh ** -0.5)
    vc = to_chunks(v.astype(f32))
    igc = to_chunks(ig)[..., 0]
    lfc = to_chunks(lf)[..., 0]
    (C, n, m), hc = lax.scan(_mlstm_chunk, (C0.astype(f32), n0.astype(f32), m0.astype(f32)),
                             (qc, kc, vc, igc, lfc))
    hm = hc.transpose(1, 0, 3, 2, 4).reshape(B, T, H, dh)
    hm = jax.nn.sigmoid(o.astype(f32)).reshape(B, T, H, dh) * hm
    hm = hm * lax.rsqrt(jnp.mean(hm * hm, axis=-1, keepdims=True) + EPS)
    hm = (hm * head_g.astype(f32).reshape(H, dh)).reshape(B, T, ML_WIDTH).astype(x.dtype)
    hm = hm * jax.nn.silu(z)
    mo = _mem_attend(mq.reshape(B, T, MEM_HEADS, MEM_HEAD_DIM), mem_k, mem_v).reshape(B, T, MEM_WIDTH)
    mo = mo * jax.nn.silu(mz)
    y = x + jnp.concatenate([hm, mo], axis=-1) @ w_out
    return y, C, n, m


def _diff_attend(q, k5, v_all, q_pos, k_pos, lam, slopes):
    s = jnp.einsum('bqhcd,bkhcd->bchqk', q, k5).astype(jnp.float32)
    dist = jnp.abs(q_pos[:, None] - k_pos[None, :]).astype(jnp.float32)
    bias = -slopes[:, None, None] * dist
    visible = (k_pos[None, :] // CHUNK) <= (q_pos[:, None] // CHUNK)
    s = jnp.where(visible, s + bias, -jnp.inf)
    p = jax.nn.softmax(s, axis=-1)
    attn = p[:, 0] - lam * p[:, 1]
    return jnp.einsum('bhqk,bkhv->bqhv', attn.astype(v_all.dtype), v_all)


def _diff_layer(x, k_all, v_all, q_pos, k_pos, mem_k, mem_v, norm_g, w_in, lam_v, subln_g, w_out, lam_init):
    B, T, _ = x.shape
    S = k_all.shape[1]
    f32 = jnp.float32
    p = _rmsnorm(x, norm_g) @ w_in
    q, z, mq, mz = jnp.split(p, B_SPLITS, axis=-1)
    q = q.reshape(B, T, DA_HEADS, 2, DA_HEAD_DIM) * (DA_HEAD_DIM ** -0.5)
    lv = lam_v.astype(f32)
    lam = jnp.exp(jnp.sum(lv[0] * lv[1])) - jnp.exp(jnp.sum(lv[2] * lv[3])) + lam_init
    slopes = jnp.exp2(-8.0 * jnp.arange(1, DA_HEADS + 1, dtype=f32) / DA_HEADS)
    k5 = k_all.reshape(B, S, DA_HEADS, 2, DA_HEAD_DIM)
    if T % Q_BLOCK == 0:
        nb = T // Q_BLOCK
        qb = q.reshape(B, nb, Q_BLOCK, DA_HEADS, 2, DA_HEAD_DIM).transpose(1, 0, 2, 3, 4, 5)
        pb = q_pos.reshape(nb, Q_BLOCK)
        ob = lax.map(lambda a: _diff_attend(a[0], k5, v_all, a[1], k_pos, lam, slopes), (qb, pb))
        o = ob.transpose(1, 0, 2, 3, 4).reshape(B, T, DA_HEADS, DA_V_DIM)
    else:
        o = _diff_attend(q, k5, v_all, q_pos, k_pos, lam, slopes)
    o = o.astype(f32)
    o = o * lax.rsqrt(jnp.mean(o * o, axis=-1, keepdims=True) + EPS) * subln_g.astype(f32) * (1.0 - lam_init)
    o = o.reshape(B, T, DA_V_WIDTH).astype(x.dtype) * jax.nn.silu(z)
    mo = _mem_attend(mq.reshape(B, T, MEM_HEADS, MEM_HEAD_DIM), mem_k, mem_v).reshape(B, T, MEM_WIDTH)
    mo = mo * jax.nn.silu(mz)
    return x + jnp.concatenate([o, mo], axis=-1) @ w_out


def _trunk(x, C0, n0, m0, past_k, past_v, mem_k, mem_v, norm_g, final_norm_g, w_in_a, b_gate_a,
           head_g_a, w_out_a, kv_norm_g, w_kv, w_in_b, lam_b, subln_g_b, w_out_b):
    B, T, _ = x.shape
    P = past_k.shape[1]
    q_pos = P + jnp.arange(T, dtype=jnp.int32)
    k_pos = jnp.arange(P + T, dtype=jnp.int32)
    Cs, ns, ms = [], [], []
    k_new = v_new = k_all = v_all = None
    for l in range(DEPTH):
        if l < N_A:
            x, C, n, m = _mlstm_layer(x, C0[l], n0[l], m0[l], mem_k[l], mem_v[l], norm_g[l],
                                      w_in_a[l], b_gate_a[l], head_g_a[l], w_out_a[l])
            Cs.append(C.astype(x.dtype))
            ns.append(n.astype(x.dtype))
            ms.append(m.astype(x.dtype))
        else:
            if l == N_A:
                kv = _rmsnorm(x, kv_norm_g) @ w_kv
                k_new = kv[..., :DA_QK_WIDTH].reshape(B, T, DA_HEADS, 2 * DA_HEAD_DIM)
                v_new = kv[..., DA_QK_WIDTH:].reshape(B, T, DA_HEADS, DA_V_DIM)
                k_all = jnp.concatenate([past_k.astype(x.dtype), k_new], axis=1)
                v_all = jnp.concatenate([past_v.astype(x.dtype), v_new], axis=1)
            j = l - N_A
            lam_init = 0.8 - 0.6 * math.exp(-0.3 * l)
            x = _diff_layer(x, k_all, v_all, q_pos, k_pos, mem_k[l], mem_v[l], norm_g[l],
                            w_in_b[j], lam_b[j], subln_g_b[j], w_out_b[j], lam_init)
    y = _rmsnorm(x, final_norm_g)
    return y, jnp.stack(Cs), jnp.stack(ns), jnp.stack(ms), k_new, v_new


def setup_inputs(seed: int = 0) -> dict:
    key = jax.random.key(seed)
    ks = jax.random.split(key, 32)
    f32 = jnp.float32

    def nrm(k, shape, s):
        return jax.random.normal(k, shape, f32) * s

    return {
        "x_prompt": nrm(ks[0], (BATCH, SEQ, D_MODEL), 1.0),
        "x_sample": nrm(ks[1], (DEC_BATCH, DEC_SEQ, D_MODEL), 1.0),
        "cache_k": nrm(ks[2], (DEC_BATCH, PAST_LEN, DA_HEADS, 2 * DA_HEAD_DIM), 1.0),
        "cache_v": nrm(ks[3], (DEC_BATCH, PAST_LEN, DA_HEADS, DA_V_DIM), 1.0),
        "cache_mem_k": nrm(ks[4], (DEPTH, DEC_BATCH, MEM_LEN, MEM_HEADS, MEM_HEAD_DIM), 1.0),
        "cache_mem_v": nrm(ks[5], (DEPTH, DEC_BATCH, MEM_LEN, MEM_HEADS, MEM_HEAD_DIM), 1.0),
        "state_C": nrm(ks[6], (N_A, DEC_BATCH, ML_HEADS, ML_HEAD_DIM, ML_HEAD_DIM), 0.1),
        "state_n": nrm(ks[7], (N_A, DEC_BATCH, ML_HEADS, ML_HEAD_DIM), 0.1),
        "state_m": nrm(ks[8], (N_A, DEC_BATCH, ML_HEADS), 0.5),
        "mem_prompt": nrm(ks[9], (BATCH, MEM_LEN, D_MODEL), 1.0),
        "norm_g": 1.0 + nrm(ks[10], (DEPTH, D_MODEL), 0.02),
        "final_norm_g": 1.0 + nrm(ks[11], (D_MODEL,), 0.02),
        "mem_norm_g": 1.0 + nrm(ks[12], (DEPTH, D_MODEL), 0.02),
        "w_mem_kv": nrm(ks[13], (DEPTH, D_MODEL, 2 * MEM_WIDTH), D_MODEL ** -0.5),
        "w_in_a": nrm(ks[14], (N_A, D_MODEL, A_IN), D_MODEL ** -0.5),
        "b_gate_a": jnp.concatenate([nrm(ks[15], (N_A, ML_HEADS), 0.1),
                                     3.0 + 3.0 * jax.random.uniform(ks[16], (N_A, ML_HEADS), f32)], axis=-1),
        "head_g_a": 1.0 + nrm(ks[17], (N_A, ML_WIDTH), 0.02),
        "w_out_a": nrm(ks[18], (N_A, ML_WIDTH + MEM_WIDTH, D_MODEL), (ML_WIDTH + MEM_WIDTH) ** -0.5),
        "kv_norm_g": 1.0 + nrm(ks[19], (D_MODEL,), 0.02),
        "w_kv": nrm(ks[20], (D_MODEL, DA_QK_WIDTH + DA_V_WIDTH), D_MODEL ** -0.5),
        "w_in_b": nrm(ks[21], (N_B, D_MODEL, B_IN), D_MODEL ** -0.5),
        "lam_b": nrm(ks[22], (N_B, 4, DA_HEAD_DIM), 0.1),
        "subln_g_b": 1.0 + nrm(ks[23], (N_B, DA_V_DIM), 0.02),
        "w_out_b": nrm(ks[24], (N_B, DA_V_WIDTH + MEM_WIDTH, D_MODEL), (DA_V_WIDTH + MEM_WIDTH) ** -0.5),
    }


def reference(x_prompt, x_sample, cache_k, cache_v, cache_mem_k, cache_mem_v, state_C, state_n, state_m,
              mem_prompt, norm_g, final_norm_g, mem_norm_g, w_mem_kv, w_in_a, b_gate_a, head_g_a, w_out_a,
              kv_norm_g, w_kv, w_in_b, lam_b, subln_g_b, w_out_b):
    B = x_prompt.shape[0]
    dt = x_prompt.dtype
    mk, mv = [], []
    for l in range(DEPTH):
        kv = _rmsnorm(mem_prompt, mem_norm_g[l]) @ w_mem_kv[l]
        mk.append(kv[..., :MEM_WIDTH].reshape(B, MEM_LEN, MEM_HEADS, MEM_HEAD_DIM))
        mv.append(kv[..., MEM_WIDTH:].reshape(B, MEM_LEN, MEM_HEADS, MEM_HEAD_DIM))
    prompt_mem_k = jnp.stack(mk)
    prompt_mem_v = jnp.stack(mv)
    C0 = jnp.zeros((N_A, B, ML_HEADS, ML_HEAD_DIM, ML_HEAD_DIM), dt)
    n0 = jnp.zeros((N_A, B, ML_HEADS, ML_HEAD_DIM), dt)
    m0 = jnp.zeros((N_A, B, ML_HEADS), dt)
    pk0 = jnp.zeros((B, 0, DA_HEADS, 2 * DA_HEAD_DIM), dt)
    pv0 = jnp.zeros((B, 0, DA_HEADS, DA_V_DIM), dt)
    y_prompt, prompt_C, prompt_n, prompt_m, prompt_k, prompt_v = _trunk(
        x_prompt, C0, n0, m0, pk0, pv0, prompt_mem_k, prompt_mem_v, norm_g, final_norm_g, w_in_a, b_gate_a,
        head_g_a, w_out_a, kv_norm_g, w_kv, w_in_b, lam_b, subln_g_b, w_out_b)
    y_sample, sample_C, sample_n, sample_m, sample_k, sample_v = _trunk(
        x_sample, state_C, state_n, state_m, cache_k, cache_v, cache_mem_k, cache_mem_v, norm_g, final_norm_g,
        w_in_a, b_gate_a, head_g_a, w_out_a, kv_norm_g, w_kv, w_in_b, lam_b, subln_g_b, w_out_b)
    return (y_prompt, y_sample, prompt_C, prompt_n, prompt_m, prompt_k, prompt_v, prompt_mem_k, prompt_mem_v,
            sample_C, sample_n, sample_m, sample_k, sample_v)
```

```python
import functools
import math

import numpy as np
import jax
import jax.numpy as jnp
from jax import lax
from jax.experimental import pallas as pl
from jax.experimental.pallas import tpu as pltpu

F32 = jnp.float32
BF16 = jnp.bfloat16
HIGHEST = lax.Precision.HIGHEST

D_MODEL = 1024
CHUNK = 64
ML_HEADS = 4
ML_HEAD_DIM = 256
ML_WIDTH = 1024
DA_HEADS = 8
DA_HEAD_DIM = 64
DA_V_DIM = 128
MEM_LEN = 256
MEM_HEADS = 4
MEM_HEAD_DIM = 128
MEM_WIDTH = 512
EPS = 1e-6
LANES = 128
NEG_BIG = -1e30
VMEM_LIMIT = 56 * 1024 * 1024

_NT = (((1,), (1,)), ((), ()))
_TN = (((0,), (0,)), ((), ()))


def _sigmoid(x):
    return 1.0 / (1.0 + jnp.exp(-x))


def _norm_matmul_body(*refs, splits, with_gates):
    if with_gates:
        x_ref, g_ref, w_ref, wg_ref = refs[:4]
        outs = refs[4:]
        gate_out = outs[-1]
        outs = outs[:-1]
    else:
        x_ref, g_ref, w_ref = refs[:3]
        outs = refs[3:]
    x = x_ref[...]
    xn = x * lax.rsqrt(jnp.mean(x * x, axis=-1, keepdims=True) + EPS) * g_ref[...]
    xb = xn.astype(BF16)
    off = 0
    for o_ref, width in zip(outs, splits):
        o_ref[...] = jnp.dot(xb, w_ref[:, off:off + width], preferred_element_type=F32)
        off += width
    if with_gates:
        gate_out[...] = jnp.dot(xn, wg_ref[...], precision=HIGHEST, preferred_element_type=F32)


def _norm_matmul(x, g, w_bf16, splits, gates_w=None, tm=256, name="norm_matmul"):
    n, d = x.shape
    width = w_bf16.shape[1]
    assert sum(splits) == width and n % tm == 0
    with_gates = gates_w is not None
    in_specs = [
        pl.BlockSpec((tm, d), lambda i: (i, 0)),
        pl.BlockSpec((1, d), lambda i: (0, 0)),
        pl.BlockSpec((d, width), lambda i: (0, 0), pipeline_mode=pl.Buffered(1)),
    ]
    args = [x, g.reshape(1, d), w_bf16]
    out_shape = [jax.ShapeDtypeStruct((n, s), F32) for s in splits]
    out_specs = [pl.BlockSpec((tm, s), lambda i: (i, 0)) for s in splits]
    if with_gates:
        in_specs.append(pl.BlockSpec((d, LANES), lambda i: (0, 0)))
        args.append(gates_w)
        out_shape.append(jax.ShapeDtypeStruct((n, LANES), F32))
        out_specs.append(pl.BlockSpec((tm, LANES), lambda i: (i, 0)))
    return pl.pallas_call(
        functools.partial(_norm_matmul_body, splits=tuple(splits), with_gates=with_gates),
        grid=(n // tm,),
        in_specs=in_specs,
        out_specs=out_specs,
        out_shape=out_shape,
        compiler_params=pltpu.CompilerParams(
            dimension_semantics=("arbitrary",), vmem_limit_bytes=VMEM_LIMIT),
        name=name,
    )(*args)


def _mlstm_body(q_ref, k_ref, v_ref, o_ref, z_ref, gt_ref, bg_ref, hg_ref, c0_ref, n0_ref, m0_ref,
                h_out, c_out, n_out, m_out, c_s, n_s, m_s, *, L, nc):
    c = pl.program_id(1)

    @pl.when(c == 0)
    def _():
        c_s[...] = c0_ref[0]
        n_s[...] = n0_ref[0]
        m_s[...] = m0_ref[0]

    gc = gt_ref[...] + bg_ref[...]
    lane = lax.broadcasted_iota(jnp.int32, (L, LANES), 1)
    lf = jnp.minimum(gc, 0.0) - jnp.log1p(jnp.exp(-jnp.abs(gc)))
    row = lax.broadcasted_iota(jnp.int32, (L, L), 0)
    col = lax.broadcasted_iota(jnp.int32, (L, L), 1)
    tril = col <= row
    gcum = jnp.dot(tril.astype(F32), lf, precision=HIGHEST, preferred_element_type=F32)
    comb = jnp.where(lane < ML_HEADS, gc, gcum)
    eye8 = (lax.broadcasted_iota(jnp.int32, (8, LANES), 0)
            == lax.broadcasted_iota(jnp.int32, (8, LANES), 1)).astype(F32)
    rows = lax.dot_general(eye8, comb, _NT, precision=HIGHEST, preferred_element_type=F32)

    for h in range(ML_HEADS):
        sl = slice(h * ML_HEAD_DIM, (h + 1) * ML_HEAD_DIM)
        ig_r = rows[h:h + 1, :]
        g_r = rows[ML_HEADS + h:ML_HEADS + h + 1, :]
        ig_c = comb[:, h:h + 1]
        g_c = comb[:, ML_HEADS + h:ML_HEADS + h + 1]
        m_prev = m_s[h][:, :1]

        dmat = jnp.where(tril, g_c - g_r + ig_r, NEG_BIG)
        inter = g_c + m_prev
        m_t = jnp.maximum(inter, jnp.max(dmat, axis=-1, keepdims=True))
        w_intra = jnp.exp(dmat - m_t)
        w_inter = jnp.exp(inter - m_t)

        qh = q_ref[:, sl]
        kh = k_ref[:, sl] * (ML_HEAD_DIM ** -0.5)
        vh = v_ref[:, sl]
        qb = qh.astype(BF16)
        kb = kh.astype(BF16)
        vb = vh.astype(BF16)
        qk = lax.dot_general(qb, kb, _NT, preferred_element_type=F32)
        s = w_intra * qk
        ch = c_s[h]
        nh = n_s[h]
        cq = lax.dot_general(qb, ch.astype(BF16), _NT, preferred_element_type=F32)
        num = w_inter * cq + jnp.dot(s.astype(BF16), vb, preferred_element_type=F32)
        nq = jnp.sum(qh * nh, axis=-1, keepdims=True)
        den = w_inter * nq + jnp.sum(s, axis=-1, keepdims=True)
        hh = num * (1.0 / jnp.maximum(jnp.abs(den), jnp.exp(-m_t)))

        g_last = g_c[L - 1:L, :]
        m_new = m_t[L - 1:L, :]
        w_s = jnp.exp(g_last - g_c + ig_c - m_new)
        dec = jnp.exp(g_last + m_prev - m_new)
        vw = (vh * w_s).astype(BF16)
        c_s[h] = dec * ch + lax.dot_general(vw, kb, _TN, preferred_element_type=F32)
        n_s[h] = dec * nh + jnp.sum(kh * w_s, axis=0, keepdims=True)
        m_s[h] = jnp.broadcast_to(m_new, (1, LANES))

        oh = o_ref[:, sl]
        zh = z_ref[:, sl]
        hm = _sigmoid(oh) * hh
        hm = hm * lax.rsqrt(jnp.mean(hm * hm, axis=-1, keepdims=True) + EPS) * hg_ref[:, sl]
        h_out[:, sl] = hm * (zh * _sigmoid(zh))

    @pl.when(c == nc - 1)
    def _():
        c_out[0] = c_s[...]
        n_out[0] = n_s[...]
        m_out[0] = m_s[...]


def _mlstm(q, k, v, o, z, gates, b_gate, head_g, c0, n0, m0, B, T, L):
    nc = T // L
    dh = ML_HEAD_DIM
    tok = lambda b, c: (b * nc + c, 0)
    st4 = lambda b, c: (b, 0, 0, 0)
    return pl.pallas_call(
        functools.partial(_mlstm_body, L=L, nc=nc),
        grid=(B, nc),
        in_specs=[pl.BlockSpec((L, ML_WIDTH), tok)] * 5 + [
            pl.BlockSpec((L, LANES), tok),
            pl.BlockSpec((1, LANES), lambda b, c: (0, 0)),
            pl.BlockSpec((1, ML_WIDTH), lambda b, c: (0, 0)),
            pl.BlockSpec((1, ML_HEADS, dh, dh), st4),
            pl.BlockSpec((1, ML_HEADS, 1, dh), st4),
            pl.BlockSpec((1, ML_HEADS, 1, LANES), st4),
        ],
        out_specs=[
            pl.BlockSpec((L, ML_WIDTH), tok),
            pl.BlockSpec((1, ML_HEADS, dh, dh), st4),
            pl.BlockSpec((1, ML_HEADS, 1, dh), st4),
            pl.BlockSpec((1, ML_HEADS, 1, LANES), st4),
        ],
        out_shape=[
            jax.ShapeDtypeStruct((B * T, ML_WIDTH), F32),
            jax.ShapeDtypeStruct((B, ML_HEADS, dh, dh), F32),
            jax.ShapeDtypeStruct((B, ML_HEADS, 1, dh), F32),
            jax.ShapeDtypeStruct((B, ML_HEADS, 1, LANES), F32),
        ],
        scratch_shapes=[
            pltpu.VMEM((ML_HEADS, dh, dh), F32),
            pltpu.VMEM((ML_HEADS, 1, dh), F32),
            pltpu.VMEM((ML_HEADS, 1, LANES), F32),
        ],
        compiler_params=pltpu.CompilerParams(
            dimension_semantics=("arbitrary", "arbitrary"), vmem_limit_bytes=VMEM_LIMIT),
        name="mlstm",
    )(q, k, v, o, z, gates, b_gate, head_g, c0, n0, m0)


def _epilogue_body(*refs, final_norm):
    if final_norm:
        x_ref, a_ref, mq_ref, mz_ref, mk_ref, mv_ref, w1_ref, w2_ref, fg_ref, y_ref = refs
    else:
        x_ref, a_ref, mq_ref, mz_ref, mk_ref, mv_ref, w1_ref, w2_ref, y_ref = refs
    acc = x_ref[...] + jnp.dot(a_ref[...].astype(BF16), w1_ref[...], preferred_element_type=F32)
    mos = []
    for h in range(MEM_HEADS):
        sl = slice(h * MEM_HEAD_DIM, (h + 1) * MEM_HEAD_DIM)
        qh = mq_ref[:, sl].astype(BF16)
        kh = mk_ref[0, :, sl].astype(BF16)
        vh = mv_ref[0, :, sl].astype(BF16)
        s = lax.dot_general(qh, kh, _NT, preferred_element_type=F32) * (MEM_HEAD_DIM ** -0.5)
        e = jnp.exp(s - jnp.max(s, axis=-1, keepdims=True))
        p = e * (1.0 / jnp.sum(e, axis=-1, keepdims=True))
        oh = jnp.dot(p.astype(BF16), vh, preferred_element_type=F32)
        zh = mz_ref[:, sl]
        mos.append((oh * (zh * _sigmoid(zh))).astype(BF16))
    mo = jnp.concatenate(mos, axis=-1)
    acc = acc + jnp.dot(mo, w2_ref[...], preferred_element_type=F32)
    if final_norm:
        acc = acc * lax.rsqrt(jnp.mean(acc * acc, axis=-1, keepdims=True) + EPS) * fg_ref[...]
    y_ref[...] = acc


def _epilogue(x, a, mq, mz, mem_k, mem_v, w1, w2, B, T, final_g=None, name="epilogue"):
    tm = min(T, 256)
    nt = T // tm
    tok = lambda b, i: (b * nt + i, 0)
    const = lambda b, i: (0, 0)
    final_norm = final_g is not None
    in_specs = [
        pl.BlockSpec((tm, D_MODEL), tok),
        pl.BlockSpec((tm, a.shape[1]), tok),
        pl.BlockSpec((tm, MEM_WIDTH), tok),
        pl.BlockSpec((tm, MEM_WIDTH), tok),
        pl.BlockSpec((1, MEM_LEN, MEM_WIDTH), lambda b, i: (b, 0, 0)),
        pl.BlockSpec((1, MEM_LEN, MEM_WIDTH), lambda b, i: (b, 0, 0)),
        pl.BlockSpec(w1.shape, const, pipeline_mode=pl.Buffered(1)),
        pl.BlockSpec(w2.shape, const, pipeline_mode=pl.Buffered(1)),
    ]
    args = [x, a, mq, mz, mem_k, mem_v, w1, w2]
    if final_norm:
        in_specs.append(pl.BlockSpec((1, D_MODEL), const))
        args.append(final_g.reshape(1, D_MODEL))
    return pl.pallas_call(
        functools.partial(_epilogue_body, final_norm=final_norm),
        grid=(B, nt),
        in_specs=in_specs,
        out_specs=pl.BlockSpec((tm, D_MODEL), tok),
        out_shape=jax.ShapeDtypeStruct((B * T, D_MODEL), F32),
        compiler_params=pltpu.CompilerParams(
            dimension_semantics=("arbitrary", "arbitrary"), vmem_limit_bytes=VMEM_LIMIT),
        name=name,
    )(*args)


_FLAG_FIRST, _FLAG_LAST, _FLAG_MASKED, _FLAG_PAST = 1, 2, 4, 8


def _attn_steps(P, T, tq, tkp, tkn):
    nq, n_past, n_new = T // tq, (P // tkp if P else 0), T // tkn
    qi_t, pj_t, nj_t, fl_t = [], [], [], []
    for qi in range(nq):
        q_lo = P + qi * tq
        q_hi = q_lo + tq - 1
        blocks = []
        for j in range(n_past):
            k_lo, k_hi = j * tkp, j * tkp + tkp - 1
            if k_lo // CHUNK <= q_hi // CHUNK:
                blocks.append((True, j, k_hi > q_lo))
        for j in range(n_new):
            k_lo, k_hi = P + j * tkn, P + j * tkn + tkn - 1
            if k_lo // CHUNK <= q_hi // CHUNK:
                blocks.append((False, j, k_hi > q_lo))
        first_new = next(j for past, j, _ in blocks if not past)
        last_past = 0
        for idx, (past, j, masked) in enumerate(blocks):
            flag = ((_FLAG_FIRST if idx == 0 else 0) | (_FLAG_LAST if idx == len(blocks) - 1 else 0)
                    | (_FLAG_MASKED if masked else 0) | (_FLAG_PAST if past else 0))
            if past:
                last_past = j
            qi_t.append(qi)
            pj_t.append(j if past else last_past)
            nj_t.append(first_new if past else j)
            fl_t.append(flag)
    as_i32 = lambda a: jnp.asarray(np.asarray(a, dtype=np.int32))
    return as_i32(qi_t), as_i32(pj_t), as_i32(nj_t), as_i32(fl_t), len(qi_t)


def _diff_attn_body(qi_ref, pj_ref, nj_ref, fl_ref, *refs, P, tq, tkp, tkn, lam_init, has_past):
    if has_past:
        q_ref, pk_ref, pv_ref, k_ref, v_ref, z_ref, lam_ref, sg_ref, o_ref, acc_ref, m_ref, l_ref = refs
    else:
        q_ref, k_ref, v_ref, z_ref, lam_ref, sg_ref, o_ref, acc_ref, m_ref, l_ref = refs
    t = pl.program_id(1)
    flags = fl_ref[t]
    q_start = P + qi_ref[t] * tq

    @pl.when((flags & _FLAG_FIRST) != 0)
    def _():
        acc_ref[...] = jnp.zeros_like(acc_ref)
        m_ref[...] = jnp.full_like(m_ref, NEG_BIG)
        l_ref[...] = jnp.zeros_like(l_ref)

    def attend(kr, vr, k_start, tk, masked):
        rel = (lax.broadcasted_iota(jnp.int32, (tq, tk), 1) - lax.broadcasted_iota(jnp.int32, (tq, tk), 0)
               + (k_start - q_start))
        if masked:
            qc = (lax.broadcasted_iota(jnp.int32, (tq, tk), 0) + q_start) // CHUNK
            kc = (lax.broadcasted_iota(jnp.int32, (tq, tk), 1) + k_start) // CHUNK
            visible = kc <= qc
            ndist = -jnp.abs(rel).astype(F32)
        else:
            ndist = rel.astype(F32)
        half = lax.broadcasted_iota(jnp.int32, (tq, DA_V_DIM), 1) < DA_HEAD_DIM
        for h in range(DA_HEADS):
            sl = slice(h * DA_V_DIM, (h + 1) * DA_V_DIM)
            slope = 2.0 ** (-8.0 * (h + 1) / DA_HEADS)
            bias = ndist * slope
            qh = q_ref[:, sl] * (DA_HEAD_DIM ** -0.5)
            kb = kr[:, sl].astype(BF16)
            vb = vr[:, sl].astype(BF16)
            for c in range(2):
                keep = half if c == 0 else jnp.logical_not(half)
                qc_b = jnp.where(keep, qh, 0.0).astype(BF16)
                s = lax.dot_general(qc_b, kb, _NT, preferred_element_type=F32) + bias
                if masked:
                    s = jnp.where(visible, s, NEG_BIG)
                idx = 2 * h + c
                m_old = m_ref[idx]
                m_new = jnp.maximum(m_old, jnp.max(s, axis=-1, keepdims=True))
                alpha = jnp.exp(m_old - m_new)
                p = jnp.exp(s - m_new)
                l_ref[idx] = alpha * l_ref[idx] + jnp.sum(p, axis=-1, keepdims=True)
                acc_ref[c, :, sl] = alpha * acc_ref[c, :, sl] + jnp.dot(
                    p.astype(BF16), vb, preferred_element_type=F32)
                m_ref[idx] = m_new

    is_masked = (flags & _FLAG_MASKED) != 0
    if has_past:
        is_past = (flags & _FLAG_PAST) != 0
        pk_start = pj_ref[t] * tkp

        @pl.when(is_past & is_masked)
        def _():
            attend(pk_ref, pv_ref, pk_start, tkp, True)

        @pl.when(is_past & jnp.logical_not(is_masked))
        def _():
            attend(pk_ref, pv_ref, pk_start, tkp, False)

        is_new = jnp.logical_not(is_past)
    else:
        is_new = True
    nk_start = P + nj_ref[t] * tkn

    @pl.when(is_new & is_masked)
    def _():
        attend(k_ref, v_ref, nk_start, tkn, True)

    @pl.when(is_new & jnp.logical_not(is_masked))
    def _():
        attend(k_ref, v_ref, nk_start, tkn, False)

    @pl.when((flags & _FLAG_LAST) != 0)
    def _():
        lv = lam_ref[...]
        lam = (jnp.exp(jnp.sum(lv[0:1] * lv[1:2], axis=-1, keepdims=True))
               - jnp.exp(jnp.sum(lv[2:3] * lv[3:4], axis=-1, keepdims=True)) + lam_init)
        for h in range(DA_HEADS):
            sl = slice(h * DA_V_DIM, (h + 1) * DA_V_DIM)
            o = (acc_ref[0, :, sl] * (1.0 / l_ref[2 * h])
                 - lam * (acc_ref[1, :, sl] * (1.0 / l_ref[2 * h + 1])))
            o = o * lax.rsqrt(jnp.mean(o * o, axis=-1, keepdims=True) + EPS) * sg_ref[...] * (1.0 - lam_init)
            zh = z_ref[:, sl]
            o_ref[:, sl] = o * (zh * _sigmoid(zh))


def _diff_attn(q, z, k_new, v_new, past_k, past_v, lam_v, subln_g, B, T, P, lam_init):
    tq = min(T, 512)
    tkn = tq
    tkp = 512
    has_past = P > 0
    assert T % tq == 0 and tq % CHUNK == 0 and (not has_past or P % tkp == 0)
    qi_t, pj_t, nj_t, fl_t, n_steps = _attn_steps(P, T, tq, tkp, tkn)
    nq, n_new = T // tq, T // tkn
    width = DA_HEADS * DA_V_DIM
    q_map = lambda b, t, qi, pj, nj, fl: (b * nq + qi[t], 0)
    new_map = lambda b, t, qi, pj, nj, fl: (b * n_new + nj[t], 0)
    const = lambda b, t, qi, pj, nj, fl: (0, 0)
    in_specs = [pl.BlockSpec((tq, width), q_map)]
    args = [q]
    if has_past:
        n_past = P // tkp
        past_map = lambda b, t, qi, pj, nj, fl: (b * n_past + pj[t], 0)
        in_specs += [pl.BlockSpec((tkp, width), past_map)] * 2
        args += [past_k, past_v]
    in_specs += [
        pl.BlockSpec((tkn, width), new_map),
        pl.BlockSpec((tkn, width), new_map),
        pl.BlockSpec((tq, width), q_map),
        pl.BlockSpec((4, DA_HEAD_DIM), const),
        pl.BlockSpec((1, DA_V_DIM), const),
    ]
    args += [k_new, v_new, z, lam_v, subln_g.reshape(1, DA_V_DIM)]
    grid_spec = pltpu.PrefetchScalarGridSpec(
        num_scalar_prefetch=4,
        grid=(B, n_steps),
        in_specs=in_specs,
        out_specs=pl.BlockSpec((tq, width), q_map),
        scratch_shapes=[
            pltpu.VMEM((2, tq, width), F32),
            pltpu.VMEM((2 * DA_HEADS, tq, 1), F32),
            pltpu.VMEM((2 * DA_HEADS, tq, 1), F32),
        ],
    )
    return pl.pallas_call(
        functools.partial(_diff_attn_body, P=P, tq=tq, tkp=tkp, tkn=tkn, lam_init=lam_init,
                          has_past=has_past),
        grid_spec=grid_spec,
        out_shape=jax.ShapeDtypeStruct((B * T, width), F32),
        compiler_params=pltpu.CompilerParams(
            dimension_semantics=("arbitrary", "arbitrary"), vmem_limit_bytes=VMEM_LIMIT),
        name="diff_attn",
    )(qi_t, pj_t, nj_t, fl_t, *args)


def _trunk(x, c0, n0, m0, past_k, past_v, mem_k, mem_v, wts):
    B, T, _ = x.shape
    P = 0 if past_k is None else past_k.shape[1]
    n_tok = B * T
    x2 = x.reshape(n_tok, D_MODEL)
    tm = 256

    q, k, v, o, z, mq, mz, gates = _norm_matmul(
        x2, wts["norm_g"][0], wts["w_a"], [ML_WIDTH] * 5 + [MEM_WIDTH] * 2, gates_w=wts["w_a_gates"],
        tm=tm, name="in_proj_a")
    L = min(T, 256)
    hm, c_new, n_new, m_new = _mlstm(
        q, k, v, o, z, gates, wts["b_gate"], wts["head_g"],
        c0, n0.reshape(B, ML_HEADS, 1, ML_HEAD_DIM),
        jnp.broadcast_to(m0.reshape(B, ML_HEADS, 1, 1), (B, ML_HEADS, 1, LANES)), B, T, L)
    x1 = _epilogue(x2, hm, mq, mz, mem_k[0], mem_v[0], wts["w_out_a1"], wts["w_out_a2"], B, T,
                   name="epilogue_a")

    k_new, v_new = _norm_matmul(x1, wts["kv_norm_g"], wts["w_kv"], [DA_HEADS * 2 * DA_HEAD_DIM,
                                                                    DA_HEADS * DA_V_DIM], tm=tm,
                                name="kv_proj")
    qd, zd, mq2, mz2 = _norm_matmul(x1, wts["norm_g"][1], wts["w_b"],
                                    [DA_HEADS * 2 * DA_HEAD_DIM, DA_HEADS * DA_V_DIM, MEM_WIDTH, MEM_WIDTH],
                                    tm=tm, name="in_proj_b")
    lam_init = 0.8 - 0.6 * math.exp(-0.3 * 1)
    pk = None if past_k is None else past_k.reshape(B * P, DA_HEADS * 2 * DA_HEAD_DIM)
    pv = None if past_v is None else past_v.reshape(B * P, DA_HEADS * DA_V_DIM)
    od = _diff_attn(qd, zd, k_new, v_new, pk, pv, wts["lam_b"], wts["subln_g"], B, T, P, lam_init)
    y = _epilogue(x1, od, mq2, mz2, mem_k[1], mem_v[1], wts["w_out_b1"], wts["w_out_b2"], B, T,
                  final_g=wts["final_norm_g"], name="epilogue_b")

    return (y.reshape(B, T, D_MODEL),
            c_new.reshape(1, B, ML_HEADS, ML_HEAD_DIM, ML_HEAD_DIM),
            n_new.reshape(1, B, ML_HEADS, ML_HEAD_DIM),
            m_new[..., 0, 0].reshape(1, B, ML_HEADS),
            k_new.reshape(B, T, DA_HEADS, 2 * DA_HEAD_DIM),
            v_new.reshape(B, T, DA_HEADS, DA_V_DIM))


def kernel(x_prompt, x_sample, cache_k, cache_v, cache_mem_k, cache_mem_v, state_C, state_n, state_m, mem_prompt, norm_g, final_norm_g, mem_norm_g, w_mem_kv, w_in_a, b_gate_a, head_g_a, w_out_a, kv_norm_g, w_kv, w_in_b, lam_b, subln_g_b, w_out_b):
    B = x_prompt.shape[0]
    DB = x_sample.shape[0]
    n_gate = 2 * ML_HEADS
    g0 = 5 * ML_WIDTH
    w_a = w_in_a[0]
    wts = {
        "norm_g": norm_g,
        "final_norm_g": final_norm_g,
        "kv_norm_g": kv_norm_g,
        "w_a": jnp.concatenate([w_a[:, :g0], w_a[:, g0 + n_gate:]], axis=1).astype(BF16),
        "w_a_gates": jnp.pad(w_a[:, g0:g0 + n_gate], ((0, 0), (0, LANES - n_gate))),
        "b_gate": jnp.pad(b_gate_a[0], (0, LANES - n_gate)).reshape(1, LANES),
        "head_g": head_g_a[0].reshape(1, ML_WIDTH),
        "w_out_a1": w_out_a[0, :ML_WIDTH].astype(BF16),
        "w_out_a2": w_out_a[0, ML_WIDTH:].astype(BF16),
        "w_kv": w_kv.astype(BF16),
        "w_b": w_in_b[0].astype(BF16),
        "lam_b": lam_b[0],
        "subln_g": subln_g_b[0],
        "w_out_b1": w_out_b[0, :DA_HEADS * DA_V_DIM].astype(BF16),
        "w_out_b2": w_out_b[0, DA_HEADS * DA_V_DIM:].astype(BF16),
    }

    mem2 = mem_prompt.reshape(B * MEM_LEN, D_MODEL)
    mks, mvs = [], []
    for l in range(2):
        mk, mv = _norm_matmul(mem2, mem_norm_g[l], w_mem_kv[l].astype(BF16), [MEM_WIDTH, MEM_WIDTH],
                              tm=256, name="mem_kv")
        mks.append(mk.reshape(B, MEM_LEN, MEM_WIDTH))
        mvs.append(mv.reshape(B, MEM_LEN, MEM_WIDTH))
    prompt_mem_k = jnp.stack(mks).reshape(2, B, MEM_LEN, MEM_HEADS, MEM_HEAD_DIM)
    prompt_mem_v = jnp.stack(mvs).reshape(2, B, MEM_LEN, MEM_HEADS, MEM_HEAD_DIM)

    zc = jnp.zeros((B, ML_HEADS, ML_HEAD_DIM, ML_HEAD_DIM), F32)
    zn = jnp.zeros((B, ML_HEADS, ML_HEAD_DIM), F32)
    zm = jnp.zeros((B, ML_HEADS), F32)
    y_prompt, prompt_C, prompt_n, prompt_m, prompt_k, prompt_v = _trunk(
        x_prompt, zc, zn, zm, None, None, mks, mvs, wts)

    smk = [cache_mem_k[l].reshape(DB, MEM_LEN, MEM_WIDTH) for l in range(2)]
    smv = [cache_mem_v[l].reshape(DB, MEM_LEN, MEM_WIDTH) for l in range(2)]
    y_sample, sample_C, sample_n, sample_m, sample_k, sample_v = _trunk(
        x_sample, state_C[0], state_n[0], state_m[0], cache_k, cache_v, smk, smv, wts)

    return (y_prompt, y_sample, prompt_C, prompt_n, prompt_m, prompt_k, prompt_v, prompt_mem_k, prompt_mem_v,
            sample_C, sample_n, sample_m, sample_k, sample_v)
```

```python
import functools
import math

import numpy as np
import jax
import jax.numpy as jnp
from jax import lax
from jax.experimental import pallas as pl
from jax.experimental.pallas import tpu as pltpu

F32 = jnp.float32
BF16 = jnp.bfloat16
HIGHEST = lax.Precision.HIGHEST

D_MODEL = 1024
CHUNK = 64
ML_HEADS = 4
ML_HEAD_DIM = 256
ML_WIDTH = 1024
DA_HEADS = 8
DA_HEAD_DIM = 64
DA_V_DIM = 128
MEM_LEN = 256
MEM_HEADS = 4
MEM_HEAD_DIM = 128
MEM_WIDTH = 512
EPS = 1e-6
LANES = 128
ONES_ROWS = 16
NEG_BIG = -1e30
VMEM_LIMIT = 56 * 1024 * 1024

_NT = (((1,), (1,)), ((), ()))
_TN = (((0,), (0,)), ((), ()))


def _sigmoid(x):
    return 1.0 / (1.0 + jnp.exp(-x))


def _norm_matmul_body(*refs, splits, with_gates, head_major):
    if with_gates:
        x_ref, g_ref, w_ref, wg_ref = refs[:4]
        outs = refs[4:]
        gate_out = outs[-1]
        outs = outs[:-1]
    else:
        x_ref, g_ref, w_ref = refs[:3]
        outs = refs[3:]
    hm_outs = outs[len(splits):]
    outs = outs[:len(splits)]
    x = x_ref[...]
    xn = x * lax.rsqrt(jnp.mean(x * x, axis=-1, keepdims=True) + EPS) * g_ref[...]
    xb = xn.astype(BF16)
    off = 0
    for i, (o_ref, width) in enumerate(zip(outs, splits)):
        r = jnp.dot(xb, w_ref[:, off:off + width], preferred_element_type=F32)
        o_ref[...] = r
        for (split, transposed), hb_ref in zip(head_major, hm_outs):
            if split == i:
                for h in range(width // LANES):
                    rh = r[:, h * LANES:(h + 1) * LANES]
                    if transposed:
                        hb_ref[h, :LANES, :] = rh.T.astype(BF16)
                        extra = lax.broadcasted_iota(jnp.int32, (ONES_ROWS, rh.shape[0]), 0) == 0
                        hb_ref[h, LANES:, :] = jnp.where(extra, 1.0, 0.0).astype(BF16)
                    else:
                        hb_ref[h] = rh.astype(BF16)
        off += width
    if with_gates:
        gate_out[...] = jnp.dot(xn, wg_ref[...], precision=HIGHEST, preferred_element_type=F32)


def _norm_matmul(x, g, w_bf16, splits, gates_w=None, head_major=(), tm=256, name="norm_matmul"):
    n, d = x.shape
    width = w_bf16.shape[1]
    assert sum(splits) == width and n % tm == 0
    with_gates = gates_w is not None
    head_major = tuple(head_major)
    in_specs = [
        pl.BlockSpec((tm, d), lambda i: (i, 0)),
        pl.BlockSpec((1, d), lambda i: (0, 0)),
        pl.BlockSpec((d, width), lambda i: (0, 0), pipeline_mode=pl.Buffered(1)),
    ]
    args = [x, g.reshape(1, d), w_bf16]
    out_shape = [jax.ShapeDtypeStruct((n, s), F32) for s in splits]
    out_specs = [pl.BlockSpec((tm, s), lambda i: (i, 0)) for s in splits]
    for split, transposed in head_major:
        nh = splits[split] // LANES
        if transposed:
            out_shape.append(jax.ShapeDtypeStruct((nh, LANES + ONES_ROWS, n), BF16))
            out_specs.append(pl.BlockSpec((nh, LANES + ONES_ROWS, tm), lambda i: (0, 0, i)))
        else:
            out_shape.append(jax.ShapeDtypeStruct((nh, n, LANES), BF16))
            out_specs.append(pl.BlockSpec((nh, tm, LANES), lambda i: (0, i, 0)))
    if with_gates:
        in_specs.append(pl.BlockSpec((d, LANES), lambda i: (0, 0)))
        args.append(gates_w)
        out_shape.append(jax.ShapeDtypeStruct((n, LANES), F32))
        out_specs.append(pl.BlockSpec((tm, LANES), lambda i: (i, 0)))
    return pl.pallas_call(
        functools.partial(_norm_matmul_body, splits=tuple(splits), with_gates=with_gates,
                          head_major=head_major),
        grid=(n // tm,),
        in_specs=in_specs,
        out_specs=out_specs,
        out_shape=out_shape,
        compiler_params=pltpu.CompilerParams(
            dimension_semantics=("arbitrary",), vmem_limit_bytes=VMEM_LIMIT),
        name=name,
    )(*args)


def _mlstm_body(q_ref, k_ref, v_ref, o_ref, z_ref, gt_ref, bg_ref, hg_ref, c0_ref, n0_ref, m0_ref,
                h_out, c_out, n_out, m_out, c_s, n_s, m_s, *, L, nc):
    c = pl.program_id(1)

    @pl.when(c == 0)
    def _():
        c_s[...] = c0_ref[0]
        n_s[...] = n0_ref[0]
        m_s[...] = m0_ref[0]

    gc = gt_ref[...] + bg_ref[...]
    lane = lax.broadcasted_iota(jnp.int32, (L, LANES), 1)
    lf = jnp.minimum(gc, 0.0) - jnp.log1p(jnp.exp(-jnp.abs(gc)))
    row = lax.broadcasted_iota(jnp.int32, (L, L), 0)
    col = lax.broadcasted_iota(jnp.int32, (L, L), 1)
    tril = col <= row
    gcum = jnp.dot(tril.astype(F32), lf, precision=HIGHEST, preferred_element_type=F32)
    comb = jnp.where(lane < ML_HEADS, gc, gcum)
    eye8 = (lax.broadcasted_iota(jnp.int32, (8, LANES), 0)
            == lax.broadcasted_iota(jnp.int32, (8, LANES), 1)).astype(F32)
    rows = lax.dot_general(eye8, comb, _NT, precision=HIGHEST, preferred_element_type=F32)

    for h in range(ML_HEADS):
        sl = slice(h * ML_HEAD_DIM, (h + 1) * ML_HEAD_DIM)
        ig_r = rows[h:h + 1, :]
        g_r = rows[ML_HEADS + h:ML_HEADS + h + 1, :]
        ig_c = comb[:, h:h + 1]
        g_c = comb[:, ML_HEADS + h:ML_HEADS + h + 1]
        m_prev = m_s[h][:, :1]

        dmat = jnp.where(tril, g_c - g_r + ig_r, NEG_BIG)
        inter = g_c + m_prev
        m_t = jnp.maximum(inter, jnp.max(dmat, axis=-1, keepdims=True))
        w_intra = jnp.exp(dmat - m_t)
        w_inter = jnp.exp(inter - m_t)

        qh = q_ref[:, sl]
        kh = k_ref[:, sl] * (ML_HEAD_DIM ** -0.5)
        vh = v_ref[:, sl]
        qb = qh.astype(BF16)
        kb = kh.astype(BF16)
        vb = vh.astype(BF16)
        qk = lax.dot_general(qb, kb, _NT, preferred_element_type=F32)
        s = w_intra * qk
        ch = c_s[h]
        nh = n_s[h]
        cq = lax.dot_general(qb, ch.astype(BF16), _NT, preferred_element_type=F32)
        num = w_inter * cq + jnp.dot(s.astype(BF16), vb, preferred_element_type=F32)
        nq = jnp.sum(qh * nh, axis=-1, keepdims=True)
        den = w_inter * nq + jnp.sum(s, axis=-1, keepdims=True)
        hh = num * (1.0 / jnp.maximum(jnp.abs(den), jnp.exp(-m_t)))

        g_last = g_c[L - 1:L, :]
        m_new = m_t[L - 1:L, :]
        w_s = jnp.exp(g_last - g_c + ig_c - m_new)
        dec = jnp.exp(g_last + m_prev - m_new)
        vw = (vh * w_s).astype(BF16)
        c_s[h] = dec * ch + lax.dot_general(vw, kb, _TN, preferred_element_type=F32)
        n_s[h] = dec * nh + jnp.sum(kh * w_s, axis=0, keepdims=True)
        m_s[h] = jnp.broadcast_to(m_new, (1, LANES))

        oh = o_ref[:, sl]
        zh = z_ref[:, sl]
        hm = _sigmoid(oh) * hh
        hm = hm * lax.rsqrt(jnp.mean(hm * hm, axis=-1, keepdims=True) + EPS) * hg_ref[:, sl]
        h_out[:, sl] = hm * (zh * _sigmoid(zh))

    @pl.when(c == nc - 1)
    def _():
        c_out[0] = c_s[...]
        n_out[0] = n_s[...]
        m_out[0] = m_s[...]


def _mlstm(q, k, v, o, z, gates, b_gate, head_g, c0, n0, m0, B, T, L):
    nc = T // L
    dh = ML_HEAD_DIM
    tok = lambda b, c: (b * nc + c, 0)
    st4 = lambda b, c: (b, 0, 0, 0)
    return pl.pallas_call(
        functools.partial(_mlstm_body, L=L, nc=nc),
        grid=(B, nc),
        in_specs=[pl.BlockSpec((L, ML_WIDTH), tok)] * 5 + [
            pl.BlockSpec((L, LANES), tok),
            pl.BlockSpec((1, LANES), lambda b, c: (0, 0)),
            pl.BlockSpec((1, ML_WIDTH), lambda b, c: (0, 0)),
            pl.BlockSpec((1, ML_HEADS, dh, dh), st4),
            pl.BlockSpec((1, ML_HEADS, 1, dh), st4),
            pl.BlockSpec((1, ML_HEADS, 1, LANES), st4),
        ],
        out_specs=[
            pl.BlockSpec((L, ML_WIDTH), tok),
            pl.BlockSpec((1, ML_HEADS, dh, dh), st4),
            pl.BlockSpec((1, ML_HEADS, 1, dh), st4),
            pl.BlockSpec((1, ML_HEADS, 1, LANES), st4),
        ],
        out_shape=[
            jax.ShapeDtypeStruct((B * T, ML_WIDTH), F32),
            jax.ShapeDtypeStruct((B, ML_HEADS, dh, dh), F32),
            jax.ShapeDtypeStruct((B, ML_HEADS, 1, dh), F32),
            jax.ShapeDtypeStruct((B, ML_HEADS, 1, LANES), F32),
        ],
        scratch_shapes=[
            pltpu.VMEM((ML_HEADS, dh, dh), F32),
            pltpu.VMEM((ML_HEADS, 1, dh), F32),
            pltpu.VMEM((ML_HEADS, 1, LANES), F32),
        ],
        compiler_params=pltpu.CompilerParams(
            dimension_semantics=("arbitrary", "arbitrary"), vmem_limit_bytes=VMEM_LIMIT),
        name="mlstm",
    )(q, k, v, o, z, gates, b_gate, head_g, c0, n0, m0)


def _epilogue_body(*refs, final_norm):
    if final_norm:
        x_ref, a_ref, mq_ref, mz_ref, mk_ref, mv_ref, w1_ref, w2_ref, fg_ref, y_ref = refs
    else:
        x_ref, a_ref, mq_ref, mz_ref, mk_ref, mv_ref, w1_ref, w2_ref, y_ref = refs
    acc = x_ref[...] + jnp.dot(a_ref[...].astype(BF16), w1_ref[...], preferred_element_type=F32)
    mos = []
    for h in range(MEM_HEADS):
        sl = slice(h * MEM_HEAD_DIM, (h + 1) * MEM_HEAD_DIM)
        qh = mq_ref[:, sl].astype(BF16)
        kh = mk_ref[0, :, sl].astype(BF16)
        vh = mv_ref[0, :, sl].astype(BF16)
        s = lax.dot_general(qh, kh, _NT, preferred_element_type=F32) * (MEM_HEAD_DIM ** -0.5)
        e = jnp.exp(s - jnp.max(s, axis=-1, keepdims=True))
        p = e * (1.0 / jnp.sum(e, axis=-1, keepdims=True))
        oh = jnp.dot(p.astype(BF16), vh, preferred_element_type=F32)
        zh = mz_ref[:, sl]
        mos.append((oh * (zh * _sigmoid(zh))).astype(BF16))
    mo = jnp.concatenate(mos, axis=-1)
    acc = acc + jnp.dot(mo, w2_ref[...], preferred_element_type=F32)
    if final_norm:
        acc = acc * lax.rsqrt(jnp.mean(acc * acc, axis=-1, keepdims=True) + EPS) * fg_ref[...]
    y_ref[...] = acc


def _epilogue(x, a, mq, mz, mem_k, mem_v, w1, w2, B, T, final_g=None, name="epilogue"):
    tm = min(T, 256)
    nt = T // tm
    tok = lambda b, i: (b * nt + i, 0)
    const = lambda b, i: (0, 0)
    final_norm = final_g is not None
    in_specs = [
        pl.BlockSpec((tm, D_MODEL), tok),
        pl.BlockSpec((tm, a.shape[1]), tok),
        pl.BlockSpec((tm, MEM_WIDTH), tok),
        pl.BlockSpec((tm, MEM_WIDTH), tok),
        pl.BlockSpec((1, MEM_LEN, MEM_WIDTH), lambda b, i: (b, 0, 0)),
        pl.BlockSpec((1, MEM_LEN, MEM_WIDTH), lambda b, i: (b, 0, 0)),
        pl.BlockSpec(w1.shape, const, pipeline_mode=pl.Buffered(1)),
        pl.BlockSpec(w2.shape, const, pipeline_mode=pl.Buffered(1)),
    ]
    args = [x, a, mq, mz, mem_k, mem_v, w1, w2]
    if final_norm:
        in_specs.append(pl.BlockSpec((1, D_MODEL), const))
        args.append(final_g.reshape(1, D_MODEL))
    return pl.pallas_call(
        functools.partial(_epilogue_body, final_norm=final_norm),
        grid=(B, nt),
        in_specs=in_specs,
        out_specs=pl.BlockSpec((tm, D_MODEL), tok),
        out_shape=jax.ShapeDtypeStruct((B * T, D_MODEL), F32),
        compiler_params=pltpu.CompilerParams(
            dimension_semantics=("arbitrary", "arbitrary"), vmem_limit_bytes=VMEM_LIMIT),
        name=name,
    )(*args)


_FLAG_FIRST, _FLAG_LAST, _FLAG_MASKED, _FLAG_PAST = 1, 2, 4, 8


def _attn_steps(P, T, tq, tkp, tkn):
    nq, n_past, n_new = T // tq, (P // tkp if P else 0), T // tkn
    qi_t, pj_t, nj_t, fl_t = [], [], [], []
    for qi in range(nq):
        q_lo = P + qi * tq
        q_hi = q_lo + tq - 1
        blocks = []
        for j in range(n_past):
            k_lo, k_hi = j * tkp, j * tkp + tkp - 1
            if k_lo // CHUNK <= q_hi // CHUNK:
                blocks.append((True, j, k_hi > q_lo))
        for j in range(n_new):
            k_lo, k_hi = P + j * tkn, P + j * tkn + tkn - 1
            if k_lo // CHUNK <= q_hi // CHUNK:
                blocks.append((False, j, k_hi > q_lo))
        first_new = next(j for past, j, _ in blocks if not past)
        last_past = 0
        for idx, (past, j, masked) in enumerate(blocks):
            flag = ((_FLAG_FIRST if idx == 0 else 0) | (_FLAG_LAST if idx == len(blocks) - 1 else 0)
                    | (_FLAG_MASKED if masked else 0) | (_FLAG_PAST if past else 0))
            if past:
                last_past = j
            qi_t.append(qi)
            pj_t.append(j if past else last_past)
            nj_t.append(first_new if past else j)
            fl_t.append(flag)
    as_i32 = lambda a: jnp.asarray(np.asarray(a, dtype=np.int32))
    return as_i32(qi_t), as_i32(pj_t), as_i32(nj_t), as_i32(fl_t), len(qi_t)


def _diff_attn_body(qi_ref, pj_ref, nj_ref, fl_ref, *refs, P, tq, tkp, tkn, lam_init, has_past):
    if has_past:
        q_ref, pk_ref, pv_ref, k_ref, v_ref, z_ref, lam_ref, sg_ref, o_ref, acc_ref, m_ref, l_ref = refs
    else:
        q_ref, k_ref, v_ref, z_ref, lam_ref, sg_ref, o_ref, acc_ref, m_ref, l_ref = refs
    t = pl.program_id(1)
    flags = fl_ref[t]
    q_start = P + qi_ref[t] * tq

    @pl.when((flags & _FLAG_FIRST) != 0)
    def _():
        acc_ref[...] = jnp.zeros_like(acc_ref)
        m_ref[...] = jnp.full_like(m_ref, NEG_BIG)
        l_ref[...] = jnp.zeros_like(l_ref)

    def attend(kr, vr, k_start, tk, masked):
        rel = (lax.broadcasted_iota(jnp.int32, (tq, tk), 1) - lax.broadcasted_iota(jnp.int32, (tq, tk), 0)
               + (k_start - q_start))
        if masked:
            qc = (lax.broadcasted_iota(jnp.int32, (tq, tk), 0) + q_start) // CHUNK
            kc = (lax.broadcasted_iota(jnp.int32, (tq, tk), 1) + k_start) // CHUNK
            visible = kc <= qc
            ndist = -jnp.abs(rel).astype(F32)
        else:
            ndist = rel.astype(F32)
        half = lax.broadcasted_iota(jnp.int32, (tq, DA_V_DIM), 1) < DA_HEAD_DIM
        for h in range(DA_HEADS):
            sl = slice(h * DA_V_DIM, (h + 1) * DA_V_DIM)
            slope = 2.0 ** (-8.0 * (h + 1) / DA_HEADS)
            bias = ndist * slope
            qh = q_ref[:, sl] * (DA_HEAD_DIM ** -0.5)
            kb = kr[:, sl].astype(BF16)
            vb = vr[:, sl].astype(BF16)
            for c in range(2):
                keep = half if c == 0 else jnp.logical_not(half)
                qc_b = jnp.where(keep, qh, 0.0).astype(BF16)
                s = lax.dot_general(qc_b, kb, _NT, preferred_element_type=F32) + bias
                if masked:
                    s = jnp.where(visible, s, NEG_BIG)
                idx = 2 * h + c
                m_old = m_ref[idx]
                m_new = jnp.maximum(m_old, jnp.max(s, axis=-1, keepdims=True))
                alpha = jnp.exp(m_old - m_new)
                p = jnp.exp(s - m_new)
                l_ref[idx] = alpha * l_ref[idx] + jnp.sum(p, axis=-1, keepdims=True)
                acc_ref[c, :, sl] = alpha * acc_ref[c, :, sl] + jnp.dot(
                    p.astype(BF16), vb, preferred_element_type=F32)
                m_ref[idx] = m_new

    is_masked = (flags & _FLAG_MASKED) != 0
    if has_past:
        is_past = (flags & _FLAG_PAST) != 0
        pk_start = pj_ref[t] * tkp

        @pl.when(is_past & is_masked)
        def _():
            attend(pk_ref, pv_ref, pk_start, tkp, True)

        @pl.when(is_past & jnp.logical_not(is_masked))
        def _():
            attend(pk_ref, pv_ref, pk_start, tkp, False)

        is_new = jnp.logical_not(is_past)
    else:
        is_new = True
    nk_start = P + nj_ref[t] * tkn

    @pl.when(is_new & is_masked)
    def _():
        attend(k_ref, v_ref, nk_start, tkn, True)

    @pl.when(is_new & jnp.logical_not(is_masked))
    def _():
        attend(k_ref, v_ref, nk_start, tkn, False)

    @pl.when((flags & _FLAG_LAST) != 0)
    def _():
        lv = lam_ref[...]
        lam = (jnp.exp(jnp.sum(lv[0:1] * lv[1:2], axis=-1, keepdims=True))
               - jnp.exp(jnp.sum(lv[2:3] * lv[3:4], axis=-1, keepdims=True)) + lam_init)
        for h in range(DA_HEADS):
            sl = slice(h * DA_V_DIM, (h + 1) * DA_V_DIM)
            o = (acc_ref[0, :, sl] * (1.0 / l_ref[2 * h])
                 - lam * (acc_ref[1, :, sl] * (1.0 / l_ref[2 * h + 1])))
            o = o * lax.rsqrt(jnp.mean(o * o, axis=-1, keepdims=True) + EPS) * sg_ref[...] * (1.0 - lam_init)
            zh = z_ref[:, sl]
            o_ref[:, sl] = o * (zh * _sigmoid(zh))


def _diff_attn(q, z, k_new, v_new, past_k, past_v, lam_v, subln_g, B, T, P, lam_init):
    tq = min(T, 512)
    tkn = tq
    tkp = 512
    has_past = P > 0
    assert T % tq == 0 and tq % CHUNK == 0 and (not has_past or P % tkp == 0)
    qi_t, pj_t, nj_t, fl_t, n_steps = _attn_steps(P, T, tq, tkp, tkn)
    nq, n_new = T // tq, T // tkn
    width = DA_HEADS * DA_V_DIM
    q_map = lambda b, t, qi, pj, nj, fl: (b * nq + qi[t], 0)
    new_map = lambda b, t, qi, pj, nj, fl: (b * n_new + nj[t], 0)
    const = lambda b, t, qi, pj, nj, fl: (0, 0)
    in_specs = [pl.BlockSpec((tq, width), q_map)]
    args = [q]
    if has_past:
        n_past = P // tkp
        past_map = lambda b, t, qi, pj, nj, fl: (b * n_past + pj[t], 0)
        in_specs += [pl.BlockSpec((tkp, width), past_map)] * 2
        args += [past_k, past_v]
    in_specs += [
        pl.BlockSpec((tkn, width), new_map),
        pl.BlockSpec((tkn, width), new_map),
        pl.BlockSpec((tq, width), q_map),
        pl.BlockSpec((4, DA_HEAD_DIM), const),
        pl.BlockSpec((1, DA_V_DIM), const),
    ]
    args += [k_new, v_new, z, lam_v, subln_g.reshape(1, DA_V_DIM)]
    grid_spec = pltpu.PrefetchScalarGridSpec(
        num_scalar_prefetch=4,
        grid=(B, n_steps),
        in_specs=in_specs,
        out_specs=pl.BlockSpec((tq, width), q_map),
        scratch_shapes=[
            pltpu.VMEM((2, tq, width), F32),
            pltpu.VMEM((2 * DA_HEADS, tq, 1), F32),
            pltpu.VMEM((2 * DA_HEADS, tq, 1), F32),
        ],
    )
    return pl.pallas_call(
        functools.partial(_diff_attn_body, P=P, tq=tq, tkp=tkp, tkn=tkn, lam_init=lam_init,
                          has_past=has_past),
        grid_spec=grid_spec,
        out_shape=jax.ShapeDtypeStruct((B * T, width), F32),
        compiler_params=pltpu.CompilerParams(
            dimension_semantics=("arbitrary", "arbitrary"), vmem_limit_bytes=VMEM_LIMIT),
        name="diff_attn",
    )(qi_t, pj_t, nj_t, fl_t, *args)


_LOG2E = 1.4426950216293335
_LOG2E_BF16_PARTS = (1.4453125, -0.00262451171875, 7.033348083496094e-06)
_N_PARTS = len(_LOG2E_BF16_PARTS)


def _diff_attn_nocache_body(qi_ref, nj_ref, fl_ref, q_ref, k_ref, vt_ref, z_ref, lam_ref, sg_ref, o_ref,
                            qa_scr, acc_scr, m_scr, pos_scr, adj_scr, s_scr, p_scr, al_scr, mx_scr,
                            *, tq, tk, lam_init):
    t = pl.program_id(1)
    flags = fl_ref[t]
    q_start = qi_ref[t] * tq
    k_start = nj_ref[t] * tk

    @pl.when((flags & _FLAG_FIRST) != 0)
    def _():
        lane = lax.broadcasted_iota(jnp.int32, (tq, LANES), 1)
        half = lane < DA_HEAD_DIM
        cblk = jnp.zeros((tq, LANES), F32)
        for i, part in enumerate(_LOG2E_BF16_PARTS):
            cblk = jnp.where((lane == i) | (lane == i + _N_PARTS), part, cblk)
        for h in range(DA_HEADS):
            sl = slice(h * DA_V_DIM, (h + 1) * DA_V_DIM)
            slope = 2.0 ** (-8.0 * (h + 1) / DA_HEADS)
            qh = q_ref[:, sl] * (DA_HEAD_DIM ** -0.5 * _LOG2E)
            cb = (cblk * slope).astype(BF16)
            for c in range(2):
                keep = half if c == 0 else jnp.logical_not(half)
                qa_scr[2 * h + c, :, :LANES] = jnp.where(keep, qh, 0.0).astype(BF16)
                qa_scr[2 * h + c, :, LANES:] = cb
        acc_scr[...] = jnp.zeros_like(acc_scr)
        m_scr[...] = jnp.full_like(m_scr, NEG_BIG)

    rel0 = lax.broadcasted_iota(jnp.int32, (tk, LANES), 0) + (k_start - q_start)
    lane_k = lax.broadcasted_iota(jnp.int32, (tk, LANES), 1)
    hi = ((rel0 >> 7) << 7).astype(F32)
    lo = (rel0 & 127).astype(F32)
    pos_scr[...] = jnp.where(lane_k < _N_PARTS, hi, jnp.where(lane_k < 2 * _N_PARTS, lo, 0.0)).astype(BF16)

    n_maps = 2 * DA_HEADS

    def attend(general):
        if general:
            kidx = lax.broadcasted_iota(jnp.int32, (tk, tq), 0)
            qidx = lax.broadcasted_iota(jnp.int32, (tk, tq), 1)
            rel = kidx - qidx + (k_start - q_start)
            visible = ((kidx + k_start) >> 6) <= ((qidx + q_start) >> 6)
            adj_scr[...] = jnp.where(visible, jnp.maximum(rel, 0).astype(F32) * (-2.0 * _LOG2E), NEG_BIG)

        def scores(idx):
            h = idx // 2
            ka = jnp.concatenate([k_ref[h], pos_scr[...]], axis=1)
            s = lax.dot_general(ka, qa_scr[idx], _NT, preferred_element_type=F32)
            if general:
                s = s + adj_scr[...] * (2.0 ** (-8.0 * (h + 1) / DA_HEADS))
            s_scr[idx % 2] = s
            mx_scr[idx % 2] = jnp.max(s, axis=0, keepdims=True)

        def softmax(idx):
            s = s_scr[idx % 2]
            m_old = m_scr[idx]
            m_new = jnp.maximum(m_old, mx_scr[idx % 2])
            p = jnp.exp2(s - m_new)
            alpha = jnp.exp2(m_old - m_new)
            p_scr[idx % 2] = p.astype(BF16)
            al_scr[idx % 2] = alpha
            m_scr[idx] = m_new

        def weighted_sum(idx):
            acc_scr[idx] = al_scr[idx % 2] * acc_scr[idx] + jnp.dot(
                vt_ref[idx // 2], p_scr[idx % 2], preferred_element_type=F32)

        scores(0)
        for idx in range(n_maps):
            if idx + 1 < n_maps:
                scores(idx + 1)
            softmax(idx)
            if idx >= 1:
                weighted_sum(idx - 1)
        weighted_sum(n_maps - 1)

    is_general = (flags & _FLAG_MASKED) != 0

    @pl.when(is_general)
    def _():
        attend(True)

    @pl.when(jnp.logical_not(is_general))
    def _():
        attend(False)

    @pl.when((flags & _FLAG_LAST) != 0)
    def _():
        lv = lam_ref[...]
        lam = (jnp.exp(jnp.sum(lv[0:1] * lv[1:2], axis=-1, keepdims=True))
               - jnp.exp(jnp.sum(lv[2:3] * lv[3:4], axis=-1, keepdims=True)) + lam_init)
        for h in range(DA_HEADS):
            sl = slice(h * DA_V_DIM, (h + 1) * DA_V_DIM)
            a1 = acc_scr[2 * h]
            a2 = acc_scr[2 * h + 1]
            ot = (a1[:DA_V_DIM] * (1.0 / a1[DA_V_DIM:DA_V_DIM + 1])
                  - lam * (a2[:DA_V_DIM] * (1.0 / a2[DA_V_DIM:DA_V_DIM + 1])))
            o = ot.T
            o = o * lax.rsqrt(jnp.mean(o * o, axis=-1, keepdims=True) + EPS) * sg_ref[...] * (1.0 - lam_init)
            zh = z_ref[:, sl]
            o_ref[:, sl] = o * (zh * _sigmoid(zh))


def _diff_attn_nocache(q, z, k_heads, vt_heads, lam_v, subln_g, B, T, lam_init, tq=512):
    tk = tq
    assert T % tq == 0 and tq % CHUNK == 0
    qi_t, _, nj_t, fl_t, n_steps = _attn_steps(0, T, tq, tk, tk)
    nq = T // tq
    width = DA_HEADS * DA_V_DIM
    q_map = lambda b, t, qi, nj, fl: (b * nq + qi[t], 0)
    k_map = lambda b, t, qi, nj, fl: (0, b * nq + nj[t], 0)
    vt_map = lambda b, t, qi, nj, fl: (0, 0, b * nq + nj[t])
    const = lambda b, t, qi, nj, fl: (0, 0)
    grid_spec = pltpu.PrefetchScalarGridSpec(
        num_scalar_prefetch=3,
        grid=(B, n_steps),
        in_specs=[
            pl.BlockSpec((tq, width), q_map),
            pl.BlockSpec((DA_HEADS, tk, LANES), k_map),
            pl.BlockSpec((DA_HEADS, DA_V_DIM + ONES_ROWS, tk), vt_map),
            pl.BlockSpec((tq, width), q_map),
            pl.BlockSpec((4, DA_HEAD_DIM), const),
            pl.BlockSpec((1, DA_V_DIM), const),
        ],
        out_specs=pl.BlockSpec((tq, width), q_map),
        scratch_shapes=[
            pltpu.VMEM((2 * DA_HEADS, tq, 2 * LANES), BF16),
            pltpu.VMEM((2 * DA_HEADS, DA_V_DIM + ONES_ROWS, tq), F32),
            pltpu.VMEM((2 * DA_HEADS, 1, tq), F32),
            pltpu.VMEM((tk, LANES), BF16),
            pltpu.VMEM((tk, tq), F32),
            pltpu.VMEM((2, tk, tq), F32),
            pltpu.VMEM((2, tk, tq), BF16),
            pltpu.VMEM((2, 1, tq), F32),
            pltpu.VMEM((2, 1, tq), F32),
        ],
    )
    return pl.pallas_call(
        functools.partial(_diff_attn_nocache_body, tq=tq, tk=tk, lam_init=lam_init),
        grid_spec=grid_spec,
        out_shape=jax.ShapeDtypeStruct((B * T, width), F32),
        compiler_params=pltpu.CompilerParams(
            dimension_semantics=("arbitrary", "arbitrary"), vmem_limit_bytes=VMEM_LIMIT),
        name="diff_attn_nocache",
    )(qi_t, nj_t, fl_t, q, k_heads, vt_heads, z, lam_v, subln_g.reshape(1, DA_V_DIM))


def _trunk(x, c0, n0, m0, past_k, past_v, mem_k, mem_v, wts):
    B, T, _ = x.shape
    P = 0 if past_k is None else past_k.shape[1]
    n_tok = B * T
    x2 = x.reshape(n_tok, D_MODEL)
    tm = 256

    q, k, v, o, z, mq, mz, gates = _norm_matmul(
        x2, wts["norm_g"][0], wts["w_a"], [ML_WIDTH] * 5 + [MEM_WIDTH] * 2, gates_w=wts["w_a_gates"],
        tm=tm, name="in_proj_a")
    L = min(T, 256)
    hm, c_new, n_new, m_new = _mlstm(
        q, k, v, o, z, gates, wts["b_gate"], wts["head_g"],
        c0, n0.reshape(B, ML_HEADS, 1, ML_HEAD_DIM),
        jnp.broadcast_to(m0.reshape(B, ML_HEADS, 1, 1), (B, ML_HEADS, 1, LANES)), B, T, L)
    x1 = _epilogue(x2, hm, mq, mz, mem_k[0], mem_v[0], wts["w_out_a1"], wts["w_out_a2"], B, T,
                   name="epilogue_a")

    kv_splits = [DA_HEADS * 2 * DA_HEAD_DIM, DA_HEADS * DA_V_DIM]
    qd, zd, mq2, mz2 = _norm_matmul(x1, wts["norm_g"][1], wts["w_b"],
                                    [DA_HEADS * 2 * DA_HEAD_DIM, DA_HEADS * DA_V_DIM, MEM_WIDTH, MEM_WIDTH],
                                    tm=tm, name="in_proj_b")
    lam_init = 0.8 - 0.6 * math.exp(-0.3 * 1)
    if past_k is None:
        k_new, v_new, k_heads, vt_heads = _norm_matmul(x1, wts["kv_norm_g"], wts["w_kv"], kv_splits,
                                                       head_major=((0, False), (1, True)), tm=tm,
                                                       name="kv_proj")
        od = _diff_attn_nocache(qd, zd, k_heads, vt_heads, wts["lam_b"], wts["subln_g"], B, T, lam_init)
    else:
        k_new, v_new = _norm_matmul(x1, wts["kv_norm_g"], wts["w_kv"], kv_splits, tm=tm, name="kv_proj")
        pk = past_k.reshape(B * P, DA_HEADS * 2 * DA_HEAD_DIM)
        pv = past_v.reshape(B * P, DA_HEADS * DA_V_DIM)
        od = _diff_attn(qd, zd, k_new, v_new, pk, pv, wts["lam_b"], wts["subln_g"], B, T, P, lam_init)
    y = _epilogue(x1, od, mq2, mz2, mem_k[1], mem_v[1], wts["w_out_b1"], wts["w_out_b2"], B, T,
                  final_g=wts["final_norm_g"], name="epilogue_b")

    return (y.reshape(B, T, D_MODEL),
            c_new.reshape(1, B, ML_HEADS, ML_HEAD_DIM, ML_HEAD_DIM),
            n_new.reshape(1, B, ML_HEADS, ML_HEAD_DIM),
            m_new[..., 0, 0].reshape(1, B, ML_HEADS),
            k_new.reshape(B, T, DA_HEADS, 2 * DA_HEAD_DIM),
            v_new.reshape(B, T, DA_HEADS, DA_V_DIM))


def kernel(x_prompt, x_sample, cache_k, cache_v, cache_mem_k, cache_mem_v, state_C, state_n, state_m, mem_prompt, norm_g, final_norm_g, mem_norm_g, w_mem_kv, w_in_a, b_gate_a, head_g_a, w_out_a, kv_norm_g, w_kv, w_in_b, lam_b, subln_g_b, w_out_b):
    B = x_prompt.shape[0]
    DB = x_sample.shape[0]
    n_gate = 2 * ML_HEADS
    g0 = 5 * ML_WIDTH
    w_a = w_in_a[0]
    wts = {
        "norm_g": norm_g,
        "final_norm_g": final_norm_g,
        "kv_norm_g": kv_norm_g,
        "w_a": jnp.concatenate([w_a[:, :g0], w_a[:, g0 + n_gate:]], axis=1).astype(BF16),
        "w_a_gates": jnp.pad(w_a[:, g0:g0 + n_gate], ((0, 0), (0, LANES - n_gate))),
        "b_gate": jnp.pad(b_gate_a[0], (0, LANES - n_gate)).reshape(1, LANES),
        "head_g": head_g_a[0].reshape(1, ML_WIDTH),
        "w_out_a1": w_out_a[0, :ML_WIDTH].astype(BF16),
        "w_out_a2": w_out_a[0, ML_WIDTH:].astype(BF16),
        "w_kv": w_kv.astype(BF16),
        "w_b": w_in_b[0].astype(BF16),
        "lam_b": lam_b[0],
        "subln_g": subln_g_b[0],
        "w_out_b1": w_out_b[0, :DA_HEADS * DA_V_DIM].astype(BF16),
        "w_out_b2": w_out_b[0, DA_HEADS * DA_V_DIM:].astype(BF16),
    }

    mem2 = mem_prompt.reshape(B * MEM_LEN, D_MODEL)
    mks, mvs = [], []
    for l in range(2):
        mk, mv = _norm_matmul(mem2, mem_norm_g[l], w_mem_kv[l].astype(BF16), [MEM_WIDTH, MEM_WIDTH],
                              tm=256, name="mem_kv")
        mks.append(mk.reshape(B, MEM_LEN, MEM_WIDTH))
        mvs.append(mv.reshape(B, MEM_LEN, MEM_WIDTH))
    prompt_mem_k = jnp.stack(mks).reshape(2, B, MEM_LEN, MEM_HEADS, MEM_HEAD_DIM)
    prompt_mem_v = jnp.stack(mvs).reshape(2, B, MEM_LEN, MEM_HEADS, MEM_HEAD_DIM)

    zc = jnp.zeros((B, ML_HEADS, ML_HEAD_DIM, ML_HEAD_DIM), F32)
    zn = jnp.zeros((B, ML_HEADS, ML_HEAD_DIM), F32)
    zm = jnp.zeros((B, ML_HEADS), F32)
    y_prompt, prompt_C, prompt_n, prompt_m, prompt_k, prompt_v = _trunk(
        x_prompt, zc, zn, zm, None, None, mks, mvs, wts)

    smk = [cache_mem_k[l].reshape(DB, MEM_LEN, MEM_WIDTH) for l in range(2)]
    smv = [cache_mem_v[l].reshape(DB, MEM_LEN, MEM_WIDTH) for l in range(2)]
    y_sample, sample_C, sample_n, sample_m, sample_k, sample_v = _trunk(
        x_sample, state_C[0], state_n[0], state_m[0], cache_k, cache_v, smk, smv, wts)

    return (y_prompt, y_sample, prompt_C, prompt_n, prompt_m, prompt_k, prompt_v, prompt_mem_k, prompt_mem_v,
            sample_C, sample_n, sample_m, sample_k, sample_v)
```

```python
import functools
import math

import numpy as np
import jax
import jax.numpy as jnp
from jax import lax
from jax.experimental import pallas as pl
from jax.experimental.pallas import tpu as pltpu

F32 = jnp.float32
BF16 = jnp.bfloat16
HIGHEST = lax.Precision.HIGHEST

D_MODEL = 1024
CHUNK = 64
ML_HEADS = 4
ML_HEAD_DIM = 256
ML_WIDTH = 1024
DA_HEADS = 8
DA_HEAD_DIM = 64
DA_V_DIM = 128
MEM_LEN = 256
MEM_HEADS = 4
MEM_HEAD_DIM = 128
MEM_WIDTH = 512
EPS = 1e-6
LANES = 128
ONES_ROWS = 16
NEG_BIG = -1e30
VMEM_LIMIT = 56 * 1024 * 1024

_NT = (((1,), (1,)), ((), ()))
_TN = (((0,), (0,)), ((), ()))


def _sigmoid(x):
    return 1.0 / (1.0 + jnp.exp(-x))


def _norm_matmul_body(*refs, splits, with_gates, head_major, interleave):
    if with_gates:
        x_ref, g_ref, w_ref, wg_ref = refs[:4]
        outs = refs[4:]
        gate_out = outs[-1]
        outs = outs[:-1]
    else:
        x_ref, g_ref, w_ref = refs[:3]
        outs = refs[3:]
    hm_outs = outs[len(splits):]
    outs = outs[:len(splits)]
    x = x_ref[...]
    xn = x * lax.rsqrt(jnp.mean(x * x, axis=-1, keepdims=True) + EPS) * g_ref[...]
    xb = xn.astype(BF16)
    off = 0
    for i, (o_ref, width) in enumerate(zip(outs, splits)):
        r = jnp.dot(xb, w_ref[:, off:off + width], preferred_element_type=F32)
        if i in interleave:
            nh = width // LANES
            for h in range(nh):
                o_ref[pl.ds(h, x.shape[0], stride=nh), :] = r[:, h * LANES:(h + 1) * LANES]
        else:
            o_ref[...] = r
        for (split, transposed), hb_ref in zip(head_major, hm_outs):
            if split == i:
                for h in range(width // LANES):
                    rh = r[:, h * LANES:(h + 1) * LANES]
                    if transposed:
                        hb_ref[h, :LANES, :] = rh.T.astype(BF16)
                        extra = lax.broadcasted_iota(jnp.int32, (ONES_ROWS, rh.shape[0]), 0) == 0
                        hb_ref[h, LANES:, :] = jnp.where(extra, 1.0, 0.0).astype(BF16)
                    else:
                        hb_ref[h] = rh.astype(BF16)
        off += width
    if with_gates:
        gate_out[...] = jnp.dot(xn, wg_ref[...], precision=HIGHEST, preferred_element_type=F32)


def _norm_matmul(x, g, w_bf16, splits, gates_w=None, head_major=(), interleave=(), tm=256,
                 name="norm_matmul"):
    n, d = x.shape
    width = w_bf16.shape[1]
    assert sum(splits) == width and n % tm == 0
    with_gates = gates_w is not None
    head_major = tuple(head_major)
    in_specs = [
        pl.BlockSpec((tm, d), lambda i: (i, 0)),
        pl.BlockSpec((1, d), lambda i: (0, 0)),
        pl.BlockSpec((d, width), lambda i: (0, 0), pipeline_mode=pl.Buffered(1)),
    ]
    args = [x, g.reshape(1, d), w_bf16]
    interleave = tuple(interleave)
    out_shape, out_specs = [], []
    for i, s in enumerate(splits):
        rows, cols = (s // LANES, LANES) if i in interleave else (1, s)
        out_shape.append(jax.ShapeDtypeStruct((n * rows, cols), F32))
        out_specs.append(pl.BlockSpec((tm * rows, cols), lambda i: (i, 0)))
    for split, transposed in head_major:
        nh = splits[split] // LANES
        if transposed:
            out_shape.append(jax.ShapeDtypeStruct((nh, LANES + ONES_ROWS, n), BF16))
            out_specs.append(pl.BlockSpec((nh, LANES + ONES_ROWS, tm), lambda i: (0, 0, i)))
        else:
            out_shape.append(jax.ShapeDtypeStruct((nh, n, LANES), BF16))
            out_specs.append(pl.BlockSpec((nh, tm, LANES), lambda i: (0, i, 0)))
    if with_gates:
        in_specs.append(pl.BlockSpec((d, LANES), lambda i: (0, 0)))
        args.append(gates_w)
        out_shape.append(jax.ShapeDtypeStruct((n, LANES), F32))
        out_specs.append(pl.BlockSpec((tm, LANES), lambda i: (i, 0)))
    return pl.pallas_call(
        functools.partial(_norm_matmul_body, splits=tuple(splits), with_gates=with_gates,
                          head_major=head_major, interleave=interleave),
        grid=(n // tm,),
        in_specs=in_specs,
        out_specs=out_specs,
        out_shape=out_shape,
        compiler_params=pltpu.CompilerParams(
            dimension_semantics=("arbitrary",), vmem_limit_bytes=VMEM_LIMIT),
        name=name,
    )(*args)


def _mlstm_body(q_ref, k_ref, v_ref, o_ref, z_ref, gt_ref, bg_ref, hg_ref, c0_ref, n0_ref, m0_ref,
                h_out, c_out, n_out, m_out, c_s, n_s, m_s, *, L, nc):
    c = pl.program_id(1)

    @pl.when(c == 0)
    def _():
        c_s[...] = c0_ref[0]
        n_s[...] = n0_ref[0]
        m_s[...] = m0_ref[0]

    gc = gt_ref[...] + bg_ref[...]
    lane = lax.broadcasted_iota(jnp.int32, (L, LANES), 1)
    lf = jnp.minimum(gc, 0.0) - jnp.log1p(jnp.exp(-jnp.abs(gc)))
    row = lax.broadcasted_iota(jnp.int32, (L, L), 0)
    col = lax.broadcasted_iota(jnp.int32, (L, L), 1)
    tril = col <= row
    gcum = jnp.dot(tril.astype(F32), lf, precision=HIGHEST, preferred_element_type=F32)
    comb = jnp.where(lane < ML_HEADS, gc, gcum)
    eye8 = (lax.broadcasted_iota(jnp.int32, (8, LANES), 0)
            == lax.broadcasted_iota(jnp.int32, (8, LANES), 1)).astype(F32)
    rows = lax.dot_general(eye8, comb, _NT, precision=HIGHEST, preferred_element_type=F32)

    for h in range(ML_HEADS):
        sl = slice(h * ML_HEAD_DIM, (h + 1) * ML_HEAD_DIM)
        ig_r = rows[h:h + 1, :]
        g_r = rows[ML_HEADS + h:ML_HEADS + h + 1, :]
        ig_c = comb[:, h:h + 1]
        g_c = comb[:, ML_HEADS + h:ML_HEADS + h + 1]
        m_prev = m_s[h][:, :1]

        dmat = jnp.where(tril, g_c - g_r + ig_r, NEG_BIG)
        inter = g_c + m_prev
        m_t = jnp.maximum(inter, jnp.max(dmat, axis=-1, keepdims=True))
        w_intra = jnp.exp(dmat - m_t)
        w_inter = jnp.exp(inter - m_t)

        qh = q_ref[:, sl]
        kh = k_ref[:, sl] * (ML_HEAD_DIM ** -0.5)
        vh = v_ref[:, sl]
        qb = qh.astype(BF16)
        kb = kh.astype(BF16)
        vb = vh.astype(BF16)
        qk = lax.dot_general(qb, kb, _NT, preferred_element_type=F32)
        s = w_intra * qk
        ch = c_s[h]
        nh = n_s[h]
        cq = lax.dot_general(qb, ch.astype(BF16), _NT, preferred_element_type=F32)
        num = w_inter * cq + jnp.dot(s.astype(BF16), vb, preferred_element_type=F32)
        nq = jnp.sum(qh * nh, axis=-1, keepdims=True)
        den = w_inter * nq + jnp.sum(s, axis=-1, keepdims=True)
        hh = num * (1.0 / jnp.maximum(jnp.abs(den), jnp.exp(-m_t)))

        g_last = g_c[L - 1:L, :]
        m_new = m_t[L - 1:L, :]
        w_s = jnp.exp(g_last - g_c + ig_c - m_new)
        dec = jnp.exp(g_last + m_prev - m_new)
        vw = (vh * w_s).astype(BF16)
        c_s[h] = dec * ch + lax.dot_general(vw, kb, _TN, preferred_element_type=F32)
        n_s[h] = dec * nh + jnp.sum(kh * w_s, axis=0, keepdims=True)
        m_s[h] = jnp.broadcast_to(m_new, (1, LANES))

        oh = o_ref[:, sl]
        zh = z_ref[:, sl]
        hm = _sigmoid(oh) * hh
        hm = hm * lax.rsqrt(jnp.mean(hm * hm, axis=-1, keepdims=True) + EPS) * hg_ref[:, sl]
        h_out[:, sl] = hm * (zh * _sigmoid(zh))

    @pl.when(c == nc - 1)
    def _():
        c_out[0] = c_s[...]
        n_out[0] = n_s[...]
        m_out[0] = m_s[...]


def _mlstm(q, k, v, o, z, gates, b_gate, head_g, c0, n0, m0, B, T, L):
    nc = T // L
    dh = ML_HEAD_DIM
    tok = lambda b, c: (b * nc + c, 0)
    st4 = lambda b, c: (b, 0, 0, 0)
    return pl.pallas_call(
        functools.partial(_mlstm_body, L=L, nc=nc),
        grid=(B, nc),
        in_specs=[pl.BlockSpec((L, ML_WIDTH), tok)] * 5 + [
            pl.BlockSpec((L, LANES), tok),
            pl.BlockSpec((1, LANES), lambda b, c: (0, 0)),
            pl.BlockSpec((1, ML_WIDTH), lambda b, c: (0, 0)),
            pl.BlockSpec((1, ML_HEADS, dh, dh), st4),
            pl.BlockSpec((1, ML_HEADS, 1, dh), st4),
            pl.BlockSpec((1, ML_HEADS, 1, LANES), st4),
        ],
        out_specs=[
            pl.BlockSpec((L, ML_WIDTH), tok),
            pl.BlockSpec((1, ML_HEADS, dh, dh), st4),
            pl.BlockSpec((1, ML_HEADS, 1, dh), st4),
            pl.BlockSpec((1, ML_HEADS, 1, LANES), st4),
        ],
        out_shape=[
            jax.ShapeDtypeStruct((B * T, ML_WIDTH), F32),
            jax.ShapeDtypeStruct((B, ML_HEADS, dh, dh), F32),
            jax.ShapeDtypeStruct((B, ML_HEADS, 1, dh), F32),
            jax.ShapeDtypeStruct((B, ML_HEADS, 1, LANES), F32),
        ],
        scratch_shapes=[
            pltpu.VMEM((ML_HEADS, dh, dh), F32),
            pltpu.VMEM((ML_HEADS, 1, dh), F32),
            pltpu.VMEM((ML_HEADS, 1, LANES), F32),
        ],
        compiler_params=pltpu.CompilerParams(
            dimension_semantics=("arbitrary", "arbitrary"), vmem_limit_bytes=VMEM_LIMIT),
        name="mlstm",
    )(q, k, v, o, z, gates, b_gate, head_g, c0, n0, m0)


def _epilogue_body(*refs, final_norm):
    if final_norm:
        x_ref, a_ref, mq_ref, mz_ref, mk_ref, mv_ref, w1_ref, w2_ref, fg_ref, y_ref = refs
    else:
        x_ref, a_ref, mq_ref, mz_ref, mk_ref, mv_ref, w1_ref, w2_ref, y_ref = refs
    acc = x_ref[...] + jnp.dot(a_ref[...].astype(BF16), w1_ref[...], preferred_element_type=F32)
    mos = []
    for h in range(MEM_HEADS):
        sl = slice(h * MEM_HEAD_DIM, (h + 1) * MEM_HEAD_DIM)
        qh = mq_ref[:, sl].astype(BF16)
        rows = pl.ds(h, MEM_LEN, stride=MEM_HEADS)
        kh = mk_ref[rows, :].astype(BF16)
        vh = mv_ref[rows, :].astype(BF16)
        s = lax.dot_general(qh, kh, _NT, preferred_element_type=F32) * (MEM_HEAD_DIM ** -0.5)
        e = jnp.exp(s - jnp.max(s, axis=-1, keepdims=True))
        p = e * (1.0 / jnp.sum(e, axis=-1, keepdims=True))
        oh = jnp.dot(p.astype(BF16), vh, preferred_element_type=F32)
        zh = mz_ref[:, sl]
        mos.append((oh * (zh * _sigmoid(zh))).astype(BF16))
    mo = jnp.concatenate(mos, axis=-1)
    acc = acc + jnp.dot(mo, w2_ref[...], preferred_element_type=F32)
    if final_norm:
        acc = acc * lax.rsqrt(jnp.mean(acc * acc, axis=-1, keepdims=True) + EPS) * fg_ref[...]
    y_ref[...] = acc


def _epilogue(x, a, mq, mz, mem_k, mem_v, w1, w2, B, T, final_g=None, name="epilogue"):
    tm = min(T, 256)
    nt = T // tm
    tok = lambda b, i: (b * nt + i, 0)
    const = lambda b, i: (0, 0)
    final_norm = final_g is not None
    in_specs = [
        pl.BlockSpec((tm, D_MODEL), tok),
        pl.BlockSpec((tm, a.shape[1]), tok),
        pl.BlockSpec((tm, MEM_WIDTH), tok),
        pl.BlockSpec((tm, MEM_WIDTH), tok),
        pl.BlockSpec((None, MEM_LEN * MEM_HEADS, MEM_HEAD_DIM), lambda b, i: (b, 0, 0)),
        pl.BlockSpec((None, MEM_LEN * MEM_HEADS, MEM_HEAD_DIM), lambda b, i: (b, 0, 0)),
        pl.BlockSpec(w1.shape, const, pipeline_mode=pl.Buffered(1)),
        pl.BlockSpec(w2.shape, const, pipeline_mode=pl.Buffered(1)),
    ]
    args = [x, a, mq, mz, mem_k, mem_v, w1, w2]
    if final_norm:
        in_specs.append(pl.BlockSpec((1, D_MODEL), const))
        args.append(final_g.reshape(1, D_MODEL))
    return pl.pallas_call(
        functools.partial(_epilogue_body, final_norm=final_norm),
        grid=(B, nt),
        in_specs=in_specs,
        out_specs=pl.BlockSpec((tm, D_MODEL), tok),
        out_shape=jax.ShapeDtypeStruct((B * T, D_MODEL), F32),
        compiler_params=pltpu.CompilerParams(
            dimension_semantics=("arbitrary", "arbitrary"), vmem_limit_bytes=VMEM_LIMIT),
        name=name,
    )(*args)


_FLAG_FIRST, _FLAG_LAST, _FLAG_MASKED, _FLAG_PAST = 1, 2, 4, 8


def _attn_steps(P, T, tq, tkp, tkn):
    nq, n_past, n_new = T // tq, (P // tkp if P else 0), T // tkn
    qi_t, pj_t, nj_t, fl_t = [], [], [], []
    for qi in range(nq):
        q_lo = P + qi * tq
        q_hi = q_lo + tq - 1
        blocks = []
        for j in range(n_past):
            k_lo, k_hi = j * tkp, j * tkp + tkp - 1
            if k_lo // CHUNK <= q_hi // CHUNK:
                blocks.append((True, j, k_hi > q_lo))
        for j in range(n_new):
            k_lo, k_hi = P + j * tkn, P + j * tkn + tkn - 1
            if k_lo // CHUNK <= q_hi // CHUNK:
                blocks.append((False, j, k_hi > q_lo))
        first_new = next(j for past, j, _ in blocks if not past)
        last_past = 0
        for idx, (past, j, masked) in enumerate(blocks):
            flag = ((_FLAG_FIRST if idx == 0 else 0) | (_FLAG_LAST if idx == len(blocks) - 1 else 0)
                    | (_FLAG_MASKED if masked else 0) | (_FLAG_PAST if past else 0))
            if past:
                last_past = j
            qi_t.append(qi)
            pj_t.append(j if past else last_past)
            nj_t.append(first_new if past else j)
            fl_t.append(flag)
    as_i32 = lambda a: jnp.asarray(np.asarray(a, dtype=np.int32))
    return as_i32(qi_t), as_i32(pj_t), as_i32(nj_t), as_i32(fl_t), len(qi_t)


def _diff_attn_body(qi_ref, pj_ref, nj_ref, fl_ref, *refs, P, tq, tkp, tkn, lam_init, has_past):
    if has_past:
        q_ref, pk_ref, pv_ref, k_ref, v_ref, z_ref, lam_ref, sg_ref, o_ref, acc_ref, m_ref, l_ref = refs
    else:
        q_ref, k_ref, v_ref, z_ref, lam_ref, sg_ref, o_ref, acc_ref, m_ref, l_ref = refs
    t = pl.program_id(1)
    flags = fl_ref[t]
    q_start = P + qi_ref[t] * tq

    @pl.when((flags & _FLAG_FIRST) != 0)
    def _():
        acc_ref[...] = jnp.zeros_like(acc_ref)
        m_ref[...] = jnp.full_like(m_ref, NEG_BIG)
        l_ref[...] = jnp.zeros_like(l_ref)

    def attend(kr, vr, k_start, tk, masked):
        rel = (lax.broadcasted_iota(jnp.int32, (tq, tk), 1) - lax.broadcasted_iota(jnp.int32, (tq, tk), 0)
               + (k_start - q_start))
        if masked:
            qc = (lax.broadcasted_iota(jnp.int32, (tq, tk), 0) + q_start) // CHUNK
            kc = (lax.broadcasted_iota(jnp.int32, (tq, tk), 1) + k_start) // CHUNK
            visible = kc <= qc
            ndist = -jnp.abs(rel).astype(F32)
        else:
            ndist = rel.astype(F32)
        half = lax.broadcasted_iota(jnp.int32, (tq, DA_V_DIM), 1) < DA_HEAD_DIM
        for h in range(DA_HEADS):
            sl = slice(h * DA_V_DIM, (h + 1) * DA_V_DIM)
            slope = 2.0 ** (-8.0 * (h + 1) / DA_HEADS)
            bias = ndist * slope
            qh = q_ref[:, sl] * (DA_HEAD_DIM ** -0.5)
            kb = kr[:, sl].astype(BF16)
            vb = vr[:, sl].astype(BF16)
            for c in range(2):
                keep = half if c == 0 else jnp.logical_not(half)
                qc_b = jnp.where(keep, qh, 0.0).astype(BF16)
                s = lax.dot_general(qc_b, kb, _NT, preferred_element_type=F32) + bias
                if masked:
                    s = jnp.where(visible, s, NEG_BIG)
                idx = 2 * h + c
                m_old = m_ref[idx]
                m_new = jnp.maximum(m_old, jnp.max(s, axis=-1, keepdims=True))
                alpha = jnp.exp(m_old - m_new)
                p = jnp.exp(s - m_new)
                l_ref[idx] = alpha * l_ref[idx] + jnp.sum(p, axis=-1, keepdims=True)
                acc_ref[c, :, sl] = alpha * acc_ref[c, :, sl] + jnp.dot(
                    p.astype(BF16), vb, preferred_element_type=F32)
                m_ref[idx] = m_new

    is_masked = (flags & _FLAG_MASKED) != 0
    if has_past:
        is_past = (flags & _FLAG_PAST) != 0
        pk_start = pj_ref[t] * tkp

        @pl.when(is_past & is_masked)
        def _():
            attend(pk_ref, pv_ref, pk_start, tkp, True)

        @pl.when(is_past & jnp.logical_not(is_masked))
        def _():
            attend(pk_ref, pv_ref, pk_start, tkp, False)

        is_new = jnp.logical_not(is_past)
    else:
        is_new = True
    nk_start = P + nj_ref[t] * tkn

    @pl.when(is_new & is_masked)
    def _():
        attend(k_ref, v_ref, nk_start, tkn, True)

    @pl.when(is_new & jnp.logical_not(is_masked))
    def _():
        attend(k_ref, v_ref, nk_start, tkn, False)

    @pl.when((flags & _FLAG_LAST) != 0)
    def _():
        lv = lam_ref[...]
        lam = (jnp.exp(jnp.sum(lv[0:1] * lv[1:2], axis=-1, keepdims=True))
               - jnp.exp(jnp.sum(lv[2:3] * lv[3:4], axis=-1, keepdims=True)) + lam_init)
        for h in range(DA_HEADS):
            sl = slice(h * DA_V_DIM, (h + 1) * DA_V_DIM)
            o = (acc_ref[0, :, sl] * (1.0 / l_ref[2 * h])
                 - lam * (acc_ref[1, :, sl] * (1.0 / l_ref[2 * h + 1])))
            o = o * lax.rsqrt(jnp.mean(o * o, axis=-1, keepdims=True) + EPS) * sg_ref[...] * (1.0 - lam_init)
            zh = z_ref[:, sl]
            o_ref[:, sl] = o * (zh * _sigmoid(zh))


def _diff_attn(q, z, k_new, v_new, past_k, past_v, lam_v, subln_g, B, T, P, lam_init):
    tq = min(T, 512)
    tkn = tq
    tkp = 512
    has_past = P > 0
    assert T % tq == 0 and tq % CHUNK == 0 and (not has_past or P % tkp == 0)
    qi_t, pj_t, nj_t, fl_t, n_steps = _attn_steps(P, T, tq, tkp, tkn)
    nq, n_new = T // tq, T // tkn
    width = DA_HEADS * DA_V_DIM
    q_map = lambda b, t, qi, pj, nj, fl: (b * nq + qi[t], 0)
    new_map = lambda b, t, qi, pj, nj, fl: (b * n_new + nj[t], 0)
    const = lambda b, t, qi, pj, nj, fl: (0, 0)
    in_specs = [pl.BlockSpec((tq, width), q_map)]
    args = [q]
    if has_past:
        n_past = P // tkp
        past_map = lambda b, t, qi, pj, nj, fl: (b * n_past + pj[t], 0)
        in_specs += [pl.BlockSpec((tkp, width), past_map)] * 2
        args += [past_k, past_v]
    in_specs += [
        pl.BlockSpec((tkn, width), new_map),
        pl.BlockSpec((tkn, width), new_map),
        pl.BlockSpec((tq, width), q_map),
        pl.BlockSpec((4, DA_HEAD_DIM), const),
        pl.BlockSpec((1, DA_V_DIM), const),
    ]
    args += [k_new, v_new, z, lam_v, subln_g.reshape(1, DA_V_DIM)]
    grid_spec = pltpu.PrefetchScalarGridSpec(
        num_scalar_prefetch=4,
        grid=(B, n_steps),
        in_specs=in_specs,
        out_specs=pl.BlockSpec((tq, width), q_map),
        scratch_shapes=[
            pltpu.VMEM((2, tq, width), F32),
            pltpu.VMEM((2 * DA_HEADS, tq, 1), F32),
            pltpu.VMEM((2 * DA_HEADS, tq, 1), F32),
        ],
    )
    return pl.pallas_call(
        functools.partial(_diff_attn_body, P=P, tq=tq, tkp=tkp, tkn=tkn, lam_init=lam_init,
                          has_past=has_past),
        grid_spec=grid_spec,
        out_shape=jax.ShapeDtypeStruct((B * T, width), F32),
        compiler_params=pltpu.CompilerParams(
            dimension_semantics=("arbitrary", "arbitrary"), vmem_limit_bytes=VMEM_LIMIT),
        name="diff_attn",
    )(qi_t, pj_t, nj_t, fl_t, *args)


_LOG2E = 1.4426950216293335
_LOG2E_BF16_PARTS = (1.4453125, -0.00262451171875, 7.033348083496094e-06)
_N_PARTS = len(_LOG2E_BF16_PARTS)


def _diff_attn_nocache_body(qi_ref, nj_ref, fl_ref, q_ref, k_ref, vt_ref, z_ref, lam_ref, sg_ref, o_ref,
                            qa_scr, acc_scr, m_scr, pos_scr, adj_scr, s_scr, p_scr, al_scr, mx_scr,
                            *, tq, tk, lam_init):
    t = pl.program_id(1)
    flags = fl_ref[t]
    q_start = qi_ref[t] * tq
    k_start = nj_ref[t] * tk

    @pl.when((flags & _FLAG_FIRST) != 0)
    def _():
        lane = lax.broadcasted_iota(jnp.int32, (tq, LANES), 1)
        half = lane < DA_HEAD_DIM
        cblk = jnp.zeros((tq, LANES), F32)
        for i, part in enumerate(_LOG2E_BF16_PARTS):
            cblk = jnp.where((lane == i) | (lane == i + _N_PARTS), part, cblk)
        for h in range(DA_HEADS):
            sl = slice(h * DA_V_DIM, (h + 1) * DA_V_DIM)
            slope = 2.0 ** (-8.0 * (h + 1) / DA_HEADS)
            qh = q_ref[:, sl] * (DA_HEAD_DIM ** -0.5 * _LOG2E)
            cb = (cblk * slope).astype(BF16)
            for c in range(2):
                keep = half if c == 0 else jnp.logical_not(half)
                qa_scr[2 * h + c, :, :LANES] = jnp.where(keep, qh, 0.0).astype(BF16)
                qa_scr[2 * h + c, :, LANES:] = cb
        acc_scr[...] = jnp.zeros_like(acc_scr)
        m_scr[...] = jnp.full_like(m_scr, NEG_BIG)

    rel0 = lax.broadcasted_iota(jnp.int32, (tk, LANES), 0) + (k_start - q_start)
    lane_k = lax.broadcasted_iota(jnp.int32, (tk, LANES), 1)
    hi = ((rel0 >> 7) << 7).astype(F32)
    lo = (rel0 & 127).astype(F32)
    pos_scr[...] = jnp.where(lane_k < _N_PARTS, hi, jnp.where(lane_k < 2 * _N_PARTS, lo, 0.0)).astype(BF16)

    n_maps = 2 * DA_HEADS

    def attend(general):
        if general:
            kidx = lax.broadcasted_iota(jnp.int32, (tk, tq), 0)
            qidx = lax.broadcasted_iota(jnp.int32, (tk, tq), 1)
            rel = kidx - qidx + (k_start - q_start)
            visible = ((kidx + k_start) >> 6) <= ((qidx + q_start) >> 6)
            adj_scr[...] = jnp.where(visible, jnp.maximum(rel, 0).astype(F32) * (-2.0 * _LOG2E), NEG_BIG)

        def scores(idx):
            h = idx // 2
            ka = jnp.concatenate([k_ref[h], pos_scr[...]], axis=1)
            s = lax.dot_general(ka, qa_scr[idx], _NT, preferred_element_type=F32)
            if general:
                s = s + adj_scr[...] * (2.0 ** (-8.0 * (h + 1) / DA_HEADS))
            s_scr[idx % 2] = s
            mx_scr[idx % 2] = jnp.max(s, axis=0, keepdims=True)

        def softmax(idx):
            s = s_scr[idx % 2]
            m_old = m_scr[idx]
            m_new = jnp.maximum(m_old, mx_scr[idx % 2])
            p = jnp.exp2(s - m_new).astype(BF16)
            alpha = jnp.exp2(m_old - m_new)
            m_scr[idx] = m_new
            acc_scr[idx] = alpha * acc_scr[idx] + jnp.dot(
                vt_ref[idx // 2], p, preferred_element_type=F32)

        scores(0)
        for idx in range(n_maps):
            if idx + 1 < n_maps:
                scores(idx + 1)
            softmax(idx)

    is_general = (flags & _FLAG_MASKED) != 0

    @pl.when(is_general)
    def _():
        attend(True)

    @pl.when(jnp.logical_not(is_general))
    def _():
        attend(False)

    @pl.when((flags & _FLAG_LAST) != 0)
    def _():
        lv = lam_ref[...]
        lam = (jnp.exp(jnp.sum(lv[0:1] * lv[1:2], axis=-1, keepdims=True))
               - jnp.exp(jnp.sum(lv[2:3] * lv[3:4], axis=-1, keepdims=True)) + lam_init)
        for h in range(DA_HEADS):
            sl = slice(h * DA_V_DIM, (h + 1) * DA_V_DIM)
            a1 = acc_scr[2 * h]
            a2 = acc_scr[2 * h + 1]
            ot = (a1[:DA_V_DIM] * (1.0 / a1[DA_V_DIM:DA_V_DIM + 1])
                  - lam * (a2[:DA_V_DIM] * (1.0 / a2[DA_V_DIM:DA_V_DIM + 1])))
            o = ot.T
            o = o * lax.rsqrt(jnp.mean(o * o, axis=-1, keepdims=True) + EPS) * sg_ref[...] * (1.0 - lam_init)
            zh = z_ref[:, sl]
            o_ref[:, sl] = o * (zh * _sigmoid(zh))


def _diff_attn_nocache(q, z, k_heads, vt_heads, lam_v, subln_g, B, T, lam_init, tq=512):
    tk = tq
    assert T % tq == 0 and tq % CHUNK == 0
    qi_t, _, nj_t, fl_t, n_steps = _attn_steps(0, T, tq, tk, tk)
    nq = T // tq
    width = DA_HEADS * DA_V_DIM
    q_map = lambda b, t, qi, nj, fl: (b * nq + qi[t], 0)
    k_map = lambda b, t, qi, nj, fl: (0, b * nq + nj[t], 0)
    vt_map = lambda b, t, qi, nj, fl: (0, 0, b * nq + nj[t])
    const = lambda b, t, qi, nj, fl: (0, 0)
    grid_spec = pltpu.PrefetchScalarGridSpec(
        num_scalar_prefetch=3,
        grid=(B, n_steps),
        in_specs=[
            pl.BlockSpec((tq, width), q_map),
            pl.BlockSpec((DA_HEADS, tk, LANES), k_map),
            pl.BlockSpec((DA_HEADS, DA_V_DIM + ONES_ROWS, tk), vt_map),
            pl.BlockSpec((tq, width), q_map),
            pl.BlockSpec((4, DA_HEAD_DIM), const),
            pl.BlockSpec((1, DA_V_DIM), const),
        ],
        out_specs=pl.BlockSpec((tq, width), q_map),
        scratch_shapes=[
            pltpu.VMEM((2 * DA_HEADS, tq, 2 * LANES), BF16),
            pltpu.VMEM((2 * DA_HEADS, DA_V_DIM + ONES_ROWS, tq), F32),
            pltpu.VMEM((2 * DA_HEADS, 1, tq), F32),
            pltpu.VMEM((tk, LANES), BF16),
            pltpu.VMEM((tk, tq), F32),
            pltpu.VMEM((2, tk, tq), F32),
            pltpu.VMEM((2, tk, tq), BF16),
            pltpu.VMEM((2, 1, tq), F32),
            pltpu.VMEM((2, 1, tq), F32),
        ],
    )
    return pl.pallas_call(
        functools.partial(_diff_attn_nocache_body, tq=tq, tk=tk, lam_init=lam_init),
        grid_spec=grid_spec,
        out_shape=jax.ShapeDtypeStruct((B * T, width), F32),
        compiler_params=pltpu.CompilerParams(
            dimension_semantics=("arbitrary", "arbitrary"), vmem_limit_bytes=VMEM_LIMIT),
        name="diff_attn_nocache",
    )(qi_t, nj_t, fl_t, q, k_heads, vt_heads, z, lam_v, subln_g.reshape(1, DA_V_DIM))


def _diff_attn_cache_body(q_ref, pk_ref, pv_ref, k_ref, v_ref, z_ref, lam_ref, sg_ref, o_ref,
                          qa_scr, acc_scr, m_scr, l_scr, *, P, T, tkp, lam_init):
    t = pl.program_id(1)
    n_past = P // tkp

    @pl.when(t == 0)
    def _():
        lane = lax.broadcasted_iota(jnp.int32, (T, LANES), 1)
        half = lane < DA_HEAD_DIM
        cblk = jnp.zeros((2 * T, LANES), F32)
        lane2 = lax.broadcasted_iota(jnp.int32, (2 * T, LANES), 1)
        for i, part in enumerate(_LOG2E_BF16_PARTS):
            cblk = jnp.where((lane2 == i) | (lane2 == i + _N_PARTS), part, cblk)
        for h in range(DA_HEADS):
            sl = slice(h * DA_V_DIM, (h + 1) * DA_V_DIM)
            slope = 2.0 ** (-8.0 * (h + 1) / DA_HEADS)
            qh = q_ref[:, sl] * (DA_HEAD_DIM ** -0.5 * _LOG2E)
            both = jnp.concatenate([jnp.where(half, qh, 0.0), jnp.where(half, 0.0, qh)], axis=0)
            qa_scr[h, :, :LANES] = both.astype(BF16)
            qa_scr[h, :, LANES:] = (cblk * slope).astype(BF16)
        acc_scr[...] = jnp.zeros_like(acc_scr)
        l_scr[...] = jnp.zeros_like(l_scr)
        m_scr[...] = jnp.full_like(m_scr, NEG_BIG)

    def pos_block(tk, rel_start):
        rel0 = lax.broadcasted_iota(jnp.int32, (tk, LANES), 0) + rel_start
        lane_k = lax.broadcasted_iota(jnp.int32, (tk, LANES), 1)
        hi = ((rel0 >> 7) << 7).astype(F32)
        lo = (rel0 & 127).astype(F32)
        return jnp.where(lane_k < _N_PARTS, hi, jnp.where(lane_k < 2 * _N_PARTS, lo, 0.0)).astype(BF16)

    def head_update(h, kb, vb, pos_blk, adj):
        ka = jnp.concatenate([kb, pos_blk], axis=1)
        s = lax.dot_general(ka, qa_scr[h], _NT, preferred_element_type=F32)
        if adj is not None:
            s = s + adj * (2.0 ** (-8.0 * (h + 1) / DA_HEADS))
        m_old = m_scr[h]
        m_new = jnp.maximum(m_old, jnp.max(s, axis=0, keepdims=True))
        p = jnp.exp2(s - m_new)
        alpha = jnp.exp2(m_old - m_new)
        l_scr[h] = alpha * l_scr[h] + jnp.sum(p, axis=0, keepdims=True)
        acc_scr[h] = alpha * acc_scr[h] + lax.dot_general(
            vb, p.astype(BF16), _TN, preferred_element_type=F32)
        m_scr[h] = m_new

    @pl.when(t < n_past)
    def _():
        pos_blk = pos_block(tkp, t * tkp - P)
        for h in range(DA_HEADS):
            rows = pl.ds(h, tkp, stride=DA_HEADS)
            head_update(h, pk_ref[rows, :].astype(BF16), pv_ref[rows, :].astype(BF16), pos_blk, None)

    @pl.when(t == n_past)
    def _():
        pos_blk = pos_block(T, 0)
        kidx = lax.broadcasted_iota(jnp.int32, (T, 2 * T), 0)
        qidx = lax.broadcasted_iota(jnp.int32, (T, 2 * T), 1) & (T - 1)
        rel = kidx - qidx
        visible = ((kidx + P) >> 6) <= ((qidx + P) >> 6)
        adj = jnp.where(visible, jnp.maximum(rel, 0).astype(F32) * (-2.0 * _LOG2E), NEG_BIG)
        for h in range(DA_HEADS):
            rows = pl.ds(h, T, stride=DA_HEADS)
            head_update(h, k_ref[rows, :].astype(BF16), v_ref[rows, :].astype(BF16), pos_blk, adj)

        lv = lam_ref[...]
        lam = (jnp.exp(jnp.sum(lv[0:1] * lv[1:2], axis=-1, keepdims=True))
               - jnp.exp(jnp.sum(lv[2:3] * lv[3:4], axis=-1, keepdims=True)) + lam_init)
        for h in range(DA_HEADS):
            sl = slice(h * DA_V_DIM, (h + 1) * DA_V_DIM)
            a = (acc_scr[h] * (1.0 / l_scr[h])).T
            o = a[:T] - lam * a[T:]
            o = o * lax.rsqrt(jnp.mean(o * o, axis=-1, keepdims=True) + EPS) * sg_ref[...] * (1.0 - lam_init)
            zh = z_ref[:, sl]
            o_ref[:, sl] = o * (zh * _sigmoid(zh))


def _diff_attn_cache(q, z, k_new, v_new, past_k, past_v, lam_v, subln_g, B, T, lam_init, tkp=512):
    P = past_k.shape[1]
    assert 2 * T == LANES and T == CHUNK and P % tkp == 0 and P % CHUNK == 0
    n_past = P // tkp
    width = DA_HEADS * DA_V_DIM
    tok = lambda b, t: (b, 0)
    past = lambda b, t: (b, jnp.minimum(t, n_past - 1), 0)
    const = lambda b, t: (0, 0)
    pk = past_k.reshape(B, P * DA_HEADS, 2 * DA_HEAD_DIM)
    pv = past_v.reshape(B, P * DA_HEADS, DA_V_DIM)
    return pl.pallas_call(
        functools.partial(_diff_attn_cache_body, P=P, T=T, tkp=tkp, lam_init=lam_init),
        grid=(B, n_past + 1),
        in_specs=[
            pl.BlockSpec((T, width), tok),
            pl.BlockSpec((None, tkp * DA_HEADS, LANES), past),
            pl.BlockSpec((None, tkp * DA_HEADS, LANES), past),
            pl.BlockSpec((T * DA_HEADS, LANES), tok),
            pl.BlockSpec((T * DA_HEADS, LANES), tok),
            pl.BlockSpec((T, width), tok),
            pl.BlockSpec((4, DA_HEAD_DIM), const),
            pl.BlockSpec((1, DA_V_DIM), const),
        ],
        out_specs=pl.BlockSpec((T, width), tok),
        out_shape=jax.ShapeDtypeStruct((B * T, width), F32),
        scratch_shapes=[
            pltpu.VMEM((DA_HEADS, 2 * T, 2 * LANES), BF16),
            pltpu.VMEM((DA_HEADS, DA_V_DIM, 2 * T), F32),
            pltpu.VMEM((DA_HEADS, 1, 2 * T), F32),
            pltpu.VMEM((DA_HEADS, 1, 2 * T), F32),
        ],
        compiler_params=pltpu.CompilerParams(
            dimension_semantics=("arbitrary", "arbitrary"), vmem_limit_bytes=VMEM_LIMIT),
        name="diff_attn_cache",
    )(q, pk, pv, k_new, v_new, z, lam_v, subln_g.reshape(1, DA_V_DIM))


def _trunk(x, c0, n0, m0, past_k, past_v, mem_k, mem_v, wts):
    B, T, _ = x.shape
    P = 0 if past_k is None else past_k.shape[1]
    n_tok = B * T
    x2 = x.reshape(n_tok, D_MODEL)
    tm = 256

    q, k, v, o, z, mq, mz, gates = _norm_matmul(
        x2, wts["norm_g"][0], wts["w_a"], [ML_WIDTH] * 5 + [MEM_WIDTH] * 2, gates_w=wts["w_a_gates"],
        tm=tm, name="in_proj_a")
    L = min(T, 256)
    hm, c_new, n_new, m_new = _mlstm(
        q, k, v, o, z, gates, wts["b_gate"], wts["head_g"],
        c0, n0.reshape(B, ML_HEADS, 1, ML_HEAD_DIM),
        jnp.broadcast_to(m0.reshape(B, ML_HEADS, 1, 1), (B, ML_HEADS, 1, LANES)), B, T, L)
    x1 = _epilogue(x2, hm, mq, mz, mem_k[0], mem_v[0], wts["w_out_a1"], wts["w_out_a2"], B, T,
                   name="epilogue_a")

    kv_splits = [DA_HEADS * 2 * DA_HEAD_DIM, DA_HEADS * DA_V_DIM]
    qd, zd, mq2, mz2 = _norm_matmul(x1, wts["norm_g"][1], wts["w_b"],
                                    [DA_HEADS * 2 * DA_HEAD_DIM, DA_HEADS * DA_V_DIM, MEM_WIDTH, MEM_WIDTH],
                                    tm=tm, name="in_proj_b")
    lam_init = 0.8 - 0.6 * math.exp(-0.3 * 1)
    if past_k is None:
        k_new, v_new, k_heads, vt_heads = _norm_matmul(x1, wts["kv_norm_g"], wts["w_kv"], kv_splits,
                                                       head_major=((0, False), (1, True)),
                                                       interleave=(0, 1), tm=tm, name="kv_proj")
        od = _diff_attn_nocache(qd, zd, k_heads, vt_heads, wts["lam_b"], wts["subln_g"], B, T, lam_init)
    else:
        k_new, v_new = _norm_matmul(x1, wts["kv_norm_g"], wts["w_kv"], kv_splits, interleave=(0, 1),
                                    tm=tm, name="kv_proj")
        od = _diff_attn_cache(qd, zd, k_new, v_new, past_k, past_v, wts["lam_b"], wts["subln_g"], B, T,
                              lam_init)
    y = _epilogue(x1, od, mq2, mz2, mem_k[1], mem_v[1], wts["w_out_b1"], wts["w_out_b2"], B, T,
                  final_g=wts["final_norm_g"], name="epilogue_b")

    return (y.reshape(B, T, D_MODEL),
            c_new.reshape(1, B, ML_HEADS, ML_HEAD_DIM, ML_HEAD_DIM),
            n_new.reshape(1, B, ML_HEADS, ML_HEAD_DIM),
            m_new[..., 0, 0].reshape(1, B, ML_HEADS),
            k_new.reshape(B, T, DA_HEADS, 2 * DA_HEAD_DIM),
            v_new.reshape(B, T, DA_HEADS, DA_V_DIM))


def kernel(x_prompt, x_sample, cache_k, cache_v, cache_mem_k, cache_mem_v, state_C, state_n, state_m, mem_prompt, norm_g, final_norm_g, mem_norm_g, w_mem_kv, w_in_a, b_gate_a, head_g_a, w_out_a, kv_norm_g, w_kv, w_in_b, lam_b, subln_g_b, w_out_b):
    B = x_prompt.shape[0]
    DB = x_sample.shape[0]
    n_gate = 2 * ML_HEADS
    g0 = 5 * ML_WIDTH
    w_a = w_in_a[0]
    wts = {
        "norm_g": norm_g,
        "final_norm_g": final_norm_g,
        "kv_norm_g": kv_norm_g,
        "w_a": jnp.concatenate([w_a[:, :g0], w_a[:, g0 + n_gate:]], axis=1).astype(BF16),
        "w_a_gates": jnp.pad(w_a[:, g0:g0 + n_gate], ((0, 0), (0, LANES - n_gate))),
        "b_gate": jnp.pad(b_gate_a[0], (0, LANES - n_gate)).reshape(1, LANES),
        "head_g": head_g_a[0].reshape(1, ML_WIDTH),
        "w_out_a1": w_out_a[0, :ML_WIDTH].astype(BF16),
        "w_out_a2": w_out_a[0, ML_WIDTH:].astype(BF16),
        "w_kv": w_kv.astype(BF16),
        "w_b": w_in_b[0].astype(BF16),
        "lam_b": lam_b[0],
        "subln_g": subln_g_b[0],
        "w_out_b1": w_out_b[0, :DA_HEADS * DA_V_DIM].astype(BF16),
        "w_out_b2": w_out_b[0, DA_HEADS * DA_V_DIM:].astype(BF16),
    }

    mem2 = mem_prompt.reshape(B * MEM_LEN, D_MODEL)
    mks, mvs = [], []
    for l in range(2):
        mk, mv = _norm_matmul(mem2, mem_norm_g[l], w_mem_kv[l].astype(BF16), [MEM_WIDTH, MEM_WIDTH],
                              interleave=(0, 1), tm=256, name="mem_kv")
        mks.append(mk.reshape(B, MEM_LEN * MEM_HEADS, MEM_HEAD_DIM))
        mvs.append(mv.reshape(B, MEM_LEN * MEM_HEADS, MEM_HEAD_DIM))
    prompt_mem_k = jnp.stack(mks).reshape(2, B, MEM_LEN, MEM_HEADS, MEM_HEAD_DIM)
    prompt_mem_v = jnp.stack(mvs).reshape(2, B, MEM_LEN, MEM_HEADS, MEM_HEAD_DIM)

    zc = jnp.zeros((B, ML_HEADS, ML_HEAD_DIM, ML_HEAD_DIM), F32)
    zn = jnp.zeros((B, ML_HEADS, ML_HEAD_DIM), F32)
    zm = jnp.zeros((B, ML_HEADS), F32)
    y_prompt, prompt_C, prompt_n, prompt_m, prompt_k, prompt_v = _trunk(
        x_prompt, zc, zn, zm, None, None, mks, mvs, wts)

    smk = [cache_mem_k[l].reshape(DB, MEM_LEN * MEM_HEADS, MEM_HEAD_DIM) for l in range(2)]
    smv = [cache_mem_v[l].reshape(DB, MEM_LEN * MEM_HEADS, MEM_HEAD_DIM) for l in range(2)]
    y_sample, sample_C, sample_n, sample_m, sample_k, sample_v = _trunk(
        x_sample, state_C[0], state_n[0], state_m[0], cache_k, cache_v, smk, smv, wts)

    return (y_prompt, y_sample, prompt_C, prompt_n, prompt_m, prompt_k, prompt_v, prompt_mem_k, prompt_mem_v,
            sample_C, sample_n, sample_m, sample_k, sample_v)
```

```python
import functools
import math

import numpy as np
import jax
import jax.numpy as jnp
from jax import lax
from jax.experimental import pallas as pl
from jax.experimental.pallas import tpu as pltpu

F32 = jnp.float32
BF16 = jnp.bfloat16
HIGHEST = lax.Precision.HIGHEST

D_MODEL = 1024
CHUNK = 64
ML_HEADS = 4
ML_HEAD_DIM = 256
ML_WIDTH = 1024
DA_HEADS = 8
DA_HEAD_DIM = 64
DA_V_DIM = 128
MEM_LEN = 256
MEM_HEADS = 4
MEM_HEAD_DIM = 128
MEM_WIDTH = 512
EPS = 1e-6
LANES = 128
ONES_ROWS = 16
NEG_BIG = -1e30
VMEM_LIMIT = 56 * 1024 * 1024

_NT = (((1,), (1,)), ((), ()))
_TN = (((0,), (0,)), ((), ()))


def _sigmoid(x):
    return 1.0 / (1.0 + jnp.exp(-x))


def _norm_matmul_body(*refs, splits, with_gates, head_major, interleave):
    if with_gates:
        x_ref, g_ref, w_ref, wg_ref = refs[:4]
        outs = refs[4:]
        gate_out = outs[-1]
        outs = outs[:-1]
    else:
        x_ref, g_ref, w_ref = refs[:3]
        outs = refs[3:]
    hm_outs = outs[len(splits):]
    outs = outs[:len(splits)]
    x = x_ref[...]
    xn = x * lax.rsqrt(jnp.mean(x * x, axis=-1, keepdims=True) + EPS) * g_ref[...]
    xb = xn.astype(BF16)
    off = 0
    for i, (o_ref, width) in enumerate(zip(outs, splits)):
        r = jnp.dot(xb, w_ref[:, off:off + width], preferred_element_type=F32)
        if i in interleave:
            nh = width // LANES
            for h in range(nh):
                o_ref[pl.ds(h, x.shape[0], stride=nh), :] = r[:, h * LANES:(h + 1) * LANES]
        else:
            o_ref[...] = r
        for (split, transposed), hb_ref in zip(head_major, hm_outs):
            if split == i:
                for h in range(width // LANES):
                    rh = r[:, h * LANES:(h + 1) * LANES]
                    if transposed:
                        hb_ref[h, :LANES, :] = rh.T.astype(BF16)
                        extra = lax.broadcasted_iota(jnp.int32, (ONES_ROWS, rh.shape[0]), 0) == 0
                        hb_ref[h, LANES:, :] = jnp.where(extra, 1.0, 0.0).astype(BF16)
                    else:
                        hb_ref[h] = rh.astype(BF16)
        off += width
    if with_gates:
        gate_out[...] = jnp.dot(xn, wg_ref[...], precision=HIGHEST, preferred_element_type=F32)


def _norm_matmul(x, g, w_bf16, splits, gates_w=None, head_major=(), interleave=(), tm=256,
                 name="norm_matmul"):
    n, d = x.shape
    width = w_bf16.shape[1]
    assert sum(splits) == width and n % tm == 0
    with_gates = gates_w is not None
    head_major = tuple(head_major)
    in_specs = [
        pl.BlockSpec((tm, d), lambda i: (i, 0)),
        pl.BlockSpec((1, d), lambda i: (0, 0)),
        pl.BlockSpec((d, width), lambda i: (0, 0), pipeline_mode=pl.Buffered(1)),
    ]
    args = [x, g.reshape(1, d), w_bf16]
    interleave = tuple(interleave)
    out_shape, out_specs = [], []
    for i, s in enumerate(splits):
        rows, cols = (s // LANES, LANES) if i in interleave else (1, s)
        out_shape.append(jax.ShapeDtypeStruct((n * rows, cols), F32))
        out_specs.append(pl.BlockSpec((tm * rows, cols), lambda i: (i, 0)))
    for split, transposed in head_major:
        nh = splits[split] // LANES
        if transposed:
            out_shape.append(jax.ShapeDtypeStruct((nh, LANES + ONES_ROWS, n), BF16))
            out_specs.append(pl.BlockSpec((nh, LANES + ONES_ROWS, tm), lambda i: (0, 0, i)))
        else:
            out_shape.append(jax.ShapeDtypeStruct((nh, n, LANES), BF16))
            out_specs.append(pl.BlockSpec((nh, tm, LANES), lambda i: (0, i, 0)))
    if with_gates:
        in_specs.append(pl.BlockSpec((d, LANES), lambda i: (0, 0)))
        args.append(gates_w)
        out_shape.append(jax.ShapeDtypeStruct((n, LANES), F32))
        out_specs.append(pl.BlockSpec((tm, LANES), lambda i: (i, 0)))
    return pl.pallas_call(
        functools.partial(_norm_matmul_body, splits=tuple(splits), with_gates=with_gates,
                          head_major=head_major, interleave=interleave),
        grid=(n // tm,),
        in_specs=in_specs,
        out_specs=out_specs,
        out_shape=out_shape,
        compiler_params=pltpu.CompilerParams(
            dimension_semantics=("arbitrary",), vmem_limit_bytes=VMEM_LIMIT),
        name=name,
    )(*args)


def _mlstm_body(q_ref, k_ref, v_ref, o_ref, z_ref, gt_ref, bg_ref, hg_ref, c0_ref, n0_ref, m0_ref,
                h_out, c_out, n_out, m_out, c_s, n_s, m_s, *, L, nc):
    c = pl.program_id(1)

    @pl.when(c == 0)
    def _():
        c_s[...] = c0_ref[0]
        n_s[...] = n0_ref[0]
        m_s[...] = m0_ref[0]

    gc = gt_ref[...] + bg_ref[...]
    lane = lax.broadcasted_iota(jnp.int32, (L, LANES), 1)
    lf = jnp.minimum(gc, 0.0) - jnp.log1p(jnp.exp(-jnp.abs(gc)))
    row = lax.broadcasted_iota(jnp.int32, (L, L), 0)
    col = lax.broadcasted_iota(jnp.int32, (L, L), 1)
    tril = col <= row
    gcum = jnp.dot(tril.astype(F32), lf, precision=HIGHEST, preferred_element_type=F32)
    comb = jnp.where(lane < ML_HEADS, gc, gcum)
    eye8 = (lax.broadcasted_iota(jnp.int32, (8, LANES), 0)
            == lax.broadcasted_iota(jnp.int32, (8, LANES), 1)).astype(F32)
    rows = lax.dot_general(eye8, comb, _NT, precision=HIGHEST, preferred_element_type=F32)

    for h in range(ML_HEADS):
        sl = slice(h * ML_HEAD_DIM, (h + 1) * ML_HEAD_DIM)
        ig_r = rows[h:h + 1, :]
        g_r = rows[ML_HEADS + h:ML_HEADS + h + 1, :]
        ig_c = comb[:, h:h + 1]
        g_c = comb[:, ML_HEADS + h:ML_HEADS + h + 1]
        m_prev = m_s[h][:, :1]

        dmat = jnp.where(tril, g_c - g_r + ig_r, NEG_BIG)
        inter = g_c + m_prev
        m_t = jnp.maximum(inter, jnp.max(dmat, axis=-1, keepdims=True))
        w_intra = jnp.exp(dmat - m_t)
        w_inter = jnp.exp(inter - m_t)

        qh = q_ref[:, sl]
        kh = k_ref[:, sl] * (ML_HEAD_DIM ** -0.5)
        vh = v_ref[:, sl]
        qb = qh.astype(BF16)
        kb = kh.astype(BF16)
        vb = vh.astype(BF16)
        qk = lax.dot_general(qb, kb, _NT, preferred_element_type=F32)
        s = w_intra * qk
        ch = c_s[h]
        nh = n_s[h]
        cq = lax.dot_general(qb, ch.astype(BF16), _NT, preferred_element_type=F32)
        num = w_inter * cq + jnp.dot(s.astype(BF16), vb, preferred_element_type=F32)
        nq = jnp.sum(qh * nh, axis=-1, keepdims=True)
        den = w_inter * nq + jnp.sum(s, axis=-1, keepdims=True)
        hh = num * (1.0 / jnp.maximum(jnp.abs(den), jnp.exp(-m_t)))

        g_last = g_c[L - 1:L, :]
        m_new = m_t[L - 1:L, :]
        w_s = jnp.exp(g_last - g_c + ig_c - m_new)
        dec = jnp.exp(g_last + m_prev - m_new)
        vw = (vh * w_s).astype(BF16)
        c_s[h] = dec * ch + lax.dot_general(vw, kb, _TN, preferred_element_type=F32)
        n_s[h] = dec * nh + jnp.sum(kh * w_s, axis=0, keepdims=True)
        m_s[h] = jnp.broadcast_to(m_new, (1, LANES))

        oh = o_ref[:, sl]
        zh = z_ref[:, sl]
        hm = _sigmoid(oh) * hh
        hm = hm * lax.rsqrt(jnp.mean(hm * hm, axis=-1, keepdims=True) + EPS) * hg_ref[:, sl]
        h_out[:, sl] = hm * (zh * _sigmoid(zh))

    @pl.when(c == nc - 1)
    def _():
        c_out[0] = c_s[...]
        n_out[0] = n_s[...]
        m_out[0] = m_s[...]


def _mlstm(q, k, v, o, z, gates, b_gate, head_g, c0, n0, m0, B, T, L):
    nc = T // L
    dh = ML_HEAD_DIM
    tok = lambda b, c: (b * nc + c, 0)
    st4 = lambda b, c: (b, 0, 0, 0)
    return pl.pallas_call(
        functools.partial(_mlstm_body, L=L, nc=nc),
        grid=(B, nc),
        in_specs=[pl.BlockSpec((L, ML_WIDTH), tok)] * 5 + [
            pl.BlockSpec((L, LANES), tok),
            pl.BlockSpec((1, LANES), lambda b, c: (0, 0)),
            pl.BlockSpec((1, ML_WIDTH), lambda b, c: (0, 0)),
            pl.BlockSpec((1, ML_HEADS, dh, dh), st4),
            pl.BlockSpec((1, ML_HEADS, 1, dh), st4),
            pl.BlockSpec((1, ML_HEADS, 1, LANES), st4),
        ],
        out_specs=[
            pl.BlockSpec((L, ML_WIDTH), tok),
            pl.BlockSpec((1, ML_HEADS, dh, dh), st4),
            pl.BlockSpec((1, ML_HEADS, 1, dh), st4),
            pl.BlockSpec((1, ML_HEADS, 1, LANES), st4),
        ],
        out_shape=[
            jax.ShapeDtypeStruct((B * T, ML_WIDTH), F32),
            jax.ShapeDtypeStruct((B, ML_HEADS, dh, dh), F32),
            jax.ShapeDtypeStruct((B, ML_HEADS, 1, dh), F32),
            jax.ShapeDtypeStruct((B, ML_HEADS, 1, LANES), F32),
        ],
        scratch_shapes=[
            pltpu.VMEM((ML_HEADS, dh, dh), F32),
            pltpu.VMEM((ML_HEADS, 1, dh), F32),
            pltpu.VMEM((ML_HEADS, 1, LANES), F32),
        ],
        compiler_params=pltpu.CompilerParams(
            dimension_semantics=("arbitrary", "arbitrary"), vmem_limit_bytes=VMEM_LIMIT),
        name="mlstm",
    )(q, k, v, o, z, gates, b_gate, head_g, c0, n0, m0)


def _epilogue_body(*refs, final_norm):
    if final_norm:
        x_ref, a_ref, mq_ref, mz_ref, mk_ref, mv_ref, w1_ref, w2_ref, fg_ref, y_ref = refs
    else:
        x_ref, a_ref, mq_ref, mz_ref, mk_ref, mv_ref, w1_ref, w2_ref, y_ref = refs
    acc = x_ref[...] + jnp.dot(a_ref[...].astype(BF16), w1_ref[...], preferred_element_type=F32)
    mos = []
    for h in range(MEM_HEADS):
        sl = slice(h * MEM_HEAD_DIM, (h + 1) * MEM_HEAD_DIM)
        qh = mq_ref[:, sl].astype(BF16)
        rows = pl.ds(h, MEM_LEN, stride=MEM_HEADS)
        kh = mk_ref[rows, :].astype(BF16)
        vh = mv_ref[rows, :].astype(BF16)
        s = lax.dot_general(qh, kh, _NT, preferred_element_type=F32) * (MEM_HEAD_DIM ** -0.5)
        e = jnp.exp(s - jnp.max(s, axis=-1, keepdims=True))
        p = e * (1.0 / jnp.sum(e, axis=-1, keepdims=True))
        oh = jnp.dot(p.astype(BF16), vh, preferred_element_type=F32)
        zh = mz_ref[:, sl]
        mos.append((oh * (zh * _sigmoid(zh))).astype(BF16))
    mo = jnp.concatenate(mos, axis=-1)
    acc = acc + jnp.dot(mo, w2_ref[...], preferred_element_type=F32)
    if final_norm:
        acc = acc * lax.rsqrt(jnp.mean(acc * acc, axis=-1, keepdims=True) + EPS) * fg_ref[...]
    y_ref[...] = acc


def _epilogue(x, a, mq, mz, mem_k, mem_v, w1, w2, B, T, final_g=None, name="epilogue"):
    tm = min(T, 512)
    nt = T // tm
    tok = lambda b, i: (b * nt + i, 0)
    const = lambda b, i: (0, 0)
    final_norm = final_g is not None
    in_specs = [
        pl.BlockSpec((tm, D_MODEL), tok),
        pl.BlockSpec((tm, a.shape[1]), tok),
        pl.BlockSpec((tm, MEM_WIDTH), tok),
        pl.BlockSpec((tm, MEM_WIDTH), tok),
        pl.BlockSpec((None, MEM_LEN * MEM_HEADS, MEM_HEAD_DIM), lambda b, i: (b, 0, 0)),
        pl.BlockSpec((None, MEM_LEN * MEM_HEADS, MEM_HEAD_DIM), lambda b, i: (b, 0, 0)),
        pl.BlockSpec(w1.shape, const, pipeline_mode=pl.Buffered(1)),
        pl.BlockSpec(w2.shape, const, pipeline_mode=pl.Buffered(1)),
    ]
    args = [x, a, mq, mz, mem_k, mem_v, w1, w2]
    if final_norm:
        in_specs.append(pl.BlockSpec((1, D_MODEL), const))
        args.append(final_g.reshape(1, D_MODEL))
    return pl.pallas_call(
        functools.partial(_epilogue_body, final_norm=final_norm),
        grid=(B, nt),
        in_specs=in_specs,
        out_specs=pl.BlockSpec((tm, D_MODEL), tok),
        out_shape=jax.ShapeDtypeStruct((B * T, D_MODEL), F32),
        compiler_params=pltpu.CompilerParams(
            dimension_semantics=("arbitrary", "arbitrary"), vmem_limit_bytes=VMEM_LIMIT),
        name=name,
    )(*args)


_FLAG_FIRST, _FLAG_LAST, _FLAG_MASKED, _FLAG_PAST = 1, 2, 4, 8


def _attn_steps(P, T, tq, tkp, tkn):
    nq, n_past, n_new = T // tq, (P // tkp if P else 0), T // tkn
    qi_t, pj_t, nj_t, fl_t = [], [], [], []
    for qi in range(nq):
        q_lo = P + qi * tq
        q_hi = q_lo + tq - 1
        blocks = []
        for j in range(n_past):
            k_lo, k_hi = j * tkp, j * tkp + tkp - 1
            if k_lo // CHUNK <= q_hi // CHUNK:
                blocks.append((True, j, k_hi > q_lo))
        for j in range(n_new):
            k_lo, k_hi = P + j * tkn, P + j * tkn + tkn - 1
            if k_lo // CHUNK <= q_hi // CHUNK:
                blocks.append((False, j, k_hi > q_lo))
        first_new = next(j for past, j, _ in blocks if not past)
        last_past = 0
        for idx, (past, j, masked) in enumerate(blocks):
            flag = ((_FLAG_FIRST if idx == 0 else 0) | (_FLAG_LAST if idx == len(blocks) - 1 else 0)
                    | (_FLAG_MASKED if masked else 0) | (_FLAG_PAST if past else 0))
            if past:
                last_past = j
            qi_t.append(qi)
            pj_t.append(j if past else last_past)
            nj_t.append(first_new if past else j)
            fl_t.append(flag)
    as_i32 = lambda a: jnp.asarray(np.asarray(a, dtype=np.int32))
    return as_i32(qi_t), as_i32(pj_t), as_i32(nj_t), as_i32(fl_t), len(qi_t)


def _diff_attn_body(qi_ref, pj_ref, nj_ref, fl_ref, *refs, P, tq, tkp, tkn, lam_init, has_past):
    if has_past:
        q_ref, pk_ref, pv_ref, k_ref, v_ref, z_ref, lam_ref, sg_ref, o_ref, acc_ref, m_ref, l_ref = refs
    else:
        q_ref, k_ref, v_ref, z_ref, lam_ref, sg_ref, o_ref, acc_ref, m_ref, l_ref = refs
    t = pl.program_id(1)
    flags = fl_ref[t]
    q_start = P + qi_ref[t] * tq

    @pl.when((flags & _FLAG_FIRST) != 0)
    def _():
        acc_ref[...] = jnp.zeros_like(acc_ref)
        m_ref[...] = jnp.full_like(m_ref, NEG_BIG)
        l_ref[...] = jnp.zeros_like(l_ref)

    def attend(kr, vr, k_start, tk, masked):
        rel = (lax.broadcasted_iota(jnp.int32, (tq, tk), 1) - lax.broadcasted_iota(jnp.int32, (tq, tk), 0)
               + (k_start - q_start))
        if masked:
            qc = (lax.broadcasted_iota(jnp.int32, (tq, tk), 0) + q_start) // CHUNK
            kc = (lax.broadcasted_iota(jnp.int32, (tq, tk), 1) + k_start) // CHUNK
            visible = kc <= qc
            ndist = -jnp.abs(rel).astype(F32)
        else:
            ndist = rel.astype(F32)
        half = lax.broadcasted_iota(jnp.int32, (tq, DA_V_DIM), 1) < DA_HEAD_DIM
        for h in range(DA_HEADS):
            sl = slice(h * DA_V_DIM, (h + 1) * DA_V_DIM)
            slope = 2.0 ** (-8.0 * (h + 1) / DA_HEADS)
            bias = ndist * slope
            qh = q_ref[:, sl] * (DA_HEAD_DIM ** -0.5)
            kb = kr[:, sl].astype(BF16)
            vb = vr[:, sl].astype(BF16)
            for c in range(2):
                keep = half if c == 0 else jnp.logical_not(half)
                qc_b = jnp.where(keep, qh, 0.0).astype(BF16)
                s = lax.dot_general(qc_b, kb, _NT, preferred_element_type=F32) + bias
                if masked:
                    s = jnp.where(visible, s, NEG_BIG)
                idx = 2 * h + c
                m_old = m_ref[idx]
                m_new = jnp.maximum(m_old, jnp.max(s, axis=-1, keepdims=True))
                alpha = jnp.exp(m_old - m_new)
                p = jnp.exp(s - m_new)
                l_ref[idx] = alpha * l_ref[idx] + jnp.sum(p, axis=-1, keepdims=True)
                acc_ref[c, :, sl] = alpha * acc_ref[c, :, sl] + jnp.dot(
                    p.astype(BF16), vb, preferred_element_type=F32)
                m_ref[idx] = m_new

    is_masked = (flags & _FLAG_MASKED) != 0
    if has_past:
        is_past = (flags & _FLAG_PAST) != 0
        pk_start = pj_ref[t] * tkp

        @pl.when(is_past & is_masked)
        def _():
            attend(pk_ref, pv_ref, pk_start, tkp, True)

        @pl.when(is_past & jnp.logical_not(is_masked))
        def _():
            attend(pk_ref, pv_ref, pk_start, tkp, False)

        is_new = jnp.logical_not(is_past)
    else:
        is_new = True
    nk_start = P + nj_ref[t] * tkn

    @pl.when(is_new & is_masked)
    def _():
        attend(k_ref, v_ref, nk_start, tkn, True)

    @pl.when(is_new & jnp.logical_not(is_masked))
    def _():
        attend(k_ref, v_ref, nk_start, tkn, False)

    @pl.when((flags & _FLAG_LAST) != 0)
    def _():
        lv = lam_ref[...]
        lam = (jnp.exp(jnp.sum(lv[0:1] * lv[1:2], axis=-1, keepdims=True))
               - jnp.exp(jnp.sum(lv[2:3] * lv[3:4], axis=-1, keepdims=True)) + lam_init)
        for h in range(DA_HEADS):
            sl = slice(h * DA_V_DIM, (h + 1) * DA_V_DIM)
            o = (acc_ref[0, :, sl] * (1.0 / l_ref[2 * h])
                 - lam * (acc_ref[1, :, sl] * (1.0 / l_ref[2 * h + 1])))
            o = o * lax.rsqrt(jnp.mean(o * o, axis=-1, keepdims=True) + EPS) * sg_ref[...] * (1.0 - lam_init)
            zh = z_ref[:, sl]
            o_ref[:, sl] = o * (zh * _sigmoid(zh))


def _diff_attn(q, z, k_new, v_new, past_k, past_v, lam_v, subln_g, B, T, P, lam_init):
    tq = min(T, 512)
    tkn = tq
    tkp = 512
    has_past = P > 0
    assert T % tq == 0 and tq % CHUNK == 0 and (not has_past or P % tkp == 0)
    qi_t, pj_t, nj_t, fl_t, n_steps = _attn_steps(P, T, tq, tkp, tkn)
    nq, n_new = T // tq, T // tkn
    width = DA_HEADS * DA_V_DIM
    q_map = lambda b, t, qi, pj, nj, fl: (b * nq + qi[t], 0)
    new_map = lambda b, t, qi, pj, nj, fl: (b * n_new + nj[t], 0)
    const = lambda b, t, qi, pj, nj, fl: (0, 0)
    in_specs = [pl.BlockSpec((tq, width), q_map)]
    args = [q]
    if has_past:
        n_past = P // tkp
        past_map = lambda b, t, qi, pj, nj, fl: (b * n_past + pj[t], 0)
        in_specs += [pl.BlockSpec((tkp, width), past_map)] * 2
        args += [past_k, past_v]
    in_specs += [
        pl.BlockSpec((tkn, width), new_map),
        pl.BlockSpec((tkn, width), new_map),
        pl.BlockSpec((tq, width), q_map),
        pl.BlockSpec((4, DA_HEAD_DIM), const),
        pl.BlockSpec((1, DA_V_DIM), const),
    ]
    args += [k_new, v_new, z, lam_v, subln_g.reshape(1, DA_V_DIM)]
    grid_spec = pltpu.PrefetchScalarGridSpec(
        num_scalar_prefetch=4,
        grid=(B, n_steps),
        in_specs=in_specs,
        out_specs=pl.BlockSpec((tq, width), q_map),
        scratch_shapes=[
            pltpu.VMEM((2, tq, width), F32),
            pltpu.VMEM((2 * DA_HEADS, tq, 1), F32),
            pltpu.VMEM((2 * DA_HEADS, tq, 1), F32),
        ],
    )
    return pl.pallas_call(
        functools.partial(_diff_attn_body, P=P, tq=tq, tkp=tkp, tkn=tkn, lam_init=lam_init,
                          has_past=has_past),
        grid_spec=grid_spec,
        out_shape=jax.ShapeDtypeStruct((B * T, width), F32),
        compiler_params=pltpu.CompilerParams(
            dimension_semantics=("arbitrary", "arbitrary"), vmem_limit_bytes=VMEM_LIMIT),
        name="diff_attn",
    )(qi_t, pj_t, nj_t, fl_t, *args)


_LOG2E = 1.4426950216293335
_LOG2E_BF16_PARTS = (1.4453125, -0.00262451171875, 7.033348083496094e-06)
_N_PARTS = len(_LOG2E_BF16_PARTS)


def _diff_attn_nocache_body(qi_ref, nj_ref, fl_ref, q_ref, k_ref, vt_ref, z_ref, lam_ref, sg_ref, o_ref,
                            qa_scr, acc_scr, m_scr, pos_scr, adj_scr, s_scr, p_scr, al_scr, mx_scr,
                            *, tq, tk, lam_init):
    t = pl.program_id(1)
    flags = fl_ref[t]
    q_start = qi_ref[t] * tq
    k_start = nj_ref[t] * tk

    @pl.when((flags & _FLAG_FIRST) != 0)
    def _():
        row = lax.broadcasted_iota(jnp.int32, (LANES, tq), 0)
        half = row < DA_HEAD_DIM
        cblk = jnp.zeros((LANES, tq), F32)
        for i, part in enumerate(_LOG2E_BF16_PARTS):
            cblk = jnp.where((row == i) | (row == i + _N_PARTS), part, cblk)
        for h in range(DA_HEADS):
            sl = slice(h * DA_V_DIM, (h + 1) * DA_V_DIM)
            slope = 2.0 ** (-8.0 * (h + 1) / DA_HEADS)
            qh = (q_ref[:, sl] * (DA_HEAD_DIM ** -0.5 * _LOG2E)).T
            cb = (cblk * slope).astype(BF16)
            for c in range(2):
                keep = half if c == 0 else jnp.logical_not(half)
                qa_scr[2 * h + c, :LANES, :] = jnp.where(keep, qh, 0.0).astype(BF16)
                qa_scr[2 * h + c, LANES:, :] = cb
        acc_scr[...] = jnp.zeros_like(acc_scr)
        m_scr[...] = jnp.full_like(m_scr, NEG_BIG)

    rel0 = lax.broadcasted_iota(jnp.int32, (tk, LANES), 0) + (k_start - q_start)
    lane_k = lax.broadcasted_iota(jnp.int32, (tk, LANES), 1)
    hi = ((rel0 >> 7) << 7).astype(F32)
    lo = (rel0 & 127).astype(F32)
    pos_scr[...] = jnp.where(lane_k < _N_PARTS, hi, jnp.where(lane_k < 2 * _N_PARTS, lo, 0.0)).astype(BF16)

    n_maps = 2 * DA_HEADS

    def attend(general):
        if general:
            kidx = lax.broadcasted_iota(jnp.int32, (tk, tq), 0)
            qidx = lax.broadcasted_iota(jnp.int32, (tk, tq), 1)
            rel = kidx - qidx + (k_start - q_start)
            visible = ((kidx + k_start) >> 6) <= ((qidx + q_start) >> 6)
            adj_scr[...] = jnp.where(visible, jnp.maximum(rel, 0).astype(F32) * (-2.0 * _LOG2E), NEG_BIG)

        def scores(idx):
            h = idx // 2
            ka = jnp.concatenate([k_ref[h], pos_scr[...]], axis=1)
            s = jnp.dot(ka, qa_scr[idx], preferred_element_type=F32)
            if general:
                s = s + adj_scr[...] * (2.0 ** (-8.0 * (h + 1) / DA_HEADS))
            s_scr[idx % 2] = s
            mx_scr[idx % 2] = jnp.max(s, axis=0, keepdims=True)

        def softmax(idx):
            s = s_scr[idx % 2]
            m_old = m_scr[idx]
            m_new = jnp.maximum(m_old, mx_scr[idx % 2])
            p = jnp.exp2(s - m_new).astype(BF16)
            alpha = jnp.exp2(m_old - m_new)
            m_scr[idx] = m_new
            acc_scr[idx] = alpha * acc_scr[idx] + jnp.dot(
                vt_ref[idx // 2], p, preferred_element_type=F32)

        scores(0)
        for idx in range(n_maps):
            if idx + 1 < n_maps:
                scores(idx + 1)
            softmax(idx)

    is_general = (flags & _FLAG_MASKED) != 0

    @pl.when(is_general)
    def _():
        attend(True)

    @pl.when(jnp.logical_not(is_general))
    def _():
        attend(False)

    @pl.when((flags & _FLAG_LAST) != 0)
    def _():
        lv = lam_ref[...]
        lam = (jnp.exp(jnp.sum(lv[0:1] * lv[1:2], axis=-1, keepdims=True))
               - jnp.exp(jnp.sum(lv[2:3] * lv[3:4], axis=-1, keepdims=True)) + lam_init)
        for h in range(DA_HEADS):
            sl = slice(h * DA_V_DIM, (h + 1) * DA_V_DIM)
            a1 = acc_scr[2 * h]
            a2 = acc_scr[2 * h + 1]
            ot = (a1[:DA_V_DIM] * (1.0 / a1[DA_V_DIM:DA_V_DIM + 1])
                  - lam * (a2[:DA_V_DIM] * (1.0 / a2[DA_V_DIM:DA_V_DIM + 1])))
            o = ot.T
            o = o * lax.rsqrt(jnp.mean(o * o, axis=-1, keepdims=True) + EPS) * sg_ref[...] * (1.0 - lam_init)
            zh = z_ref[:, sl]
            o_ref[:, sl] = o * (zh * _sigmoid(zh))


def _diff_attn_nocache(q, z, k_heads, vt_heads, lam_v, subln_g, B, T, lam_init, tq=512):
    tk = tq
    assert T % tq == 0 and tq % CHUNK == 0
    qi_t, _, nj_t, fl_t, n_steps = _attn_steps(0, T, tq, tk, tk)
    nq = T // tq
    width = DA_HEADS * DA_V_DIM
    q_map = lambda b, t, qi, nj, fl: (b * nq + qi[t], 0)
    k_map = lambda b, t, qi, nj, fl: (0, b * nq + nj[t], 0)
    vt_map = lambda b, t, qi, nj, fl: (0, 0, b * nq + nj[t])
    const = lambda b, t, qi, nj, fl: (0, 0)
    grid_spec = pltpu.PrefetchScalarGridSpec(
        num_scalar_prefetch=3,
        grid=(B, n_steps),
        in_specs=[
            pl.BlockSpec((tq, width), q_map),
            pl.BlockSpec((DA_HEADS, tk, LANES), k_map),
            pl.BlockSpec((DA_HEADS, DA_V_DIM + ONES_ROWS, tk), vt_map),
            pl.BlockSpec((tq, width), q_map),
            pl.BlockSpec((4, DA_HEAD_DIM), const),
            pl.BlockSpec((1, DA_V_DIM), const),
        ],
        out_specs=pl.BlockSpec((tq, width), q_map),
        scratch_shapes=[
            pltpu.VMEM((2 * DA_HEADS, 2 * LANES, tq), BF16),
            pltpu.VMEM((2 * DA_HEADS, DA_V_DIM + ONES_ROWS, tq), F32),
            pltpu.VMEM((2 * DA_HEADS, 1, tq), F32),
            pltpu.VMEM((tk, LANES), BF16),
            pltpu.VMEM((tk, tq), F32),
            pltpu.VMEM((2, tk, tq), F32),
            pltpu.VMEM((2, tk, tq), BF16),
            pltpu.VMEM((2, 1, tq), F32),
            pltpu.VMEM((2, 1, tq), F32),
        ],
    )
    return pl.pallas_call(
        functools.partial(_diff_attn_nocache_body, tq=tq, tk=tk, lam_init=lam_init),
        grid_spec=grid_spec,
        out_shape=jax.ShapeDtypeStruct((B * T, width), F32),
        compiler_params=pltpu.CompilerParams(
            dimension_semantics=("arbitrary", "arbitrary"), vmem_limit_bytes=VMEM_LIMIT),
        name="diff_attn_nocache",
    )(qi_t, nj_t, fl_t, q, k_heads, vt_heads, z, lam_v, subln_g.reshape(1, DA_V_DIM))


def _diff_attn_cache_body(q_ref, pk_ref, pv_ref, k_ref, v_ref, z_ref, lam_ref, sg_ref, o_ref,
                          qa_scr, acc_scr, m_scr, l_scr, s_scr, mx_scr, *, P, T, tkp, lam_init):
    t = pl.program_id(1)
    n_past = P // tkp

    @pl.when(t == 0)
    def _():
        lane = lax.broadcasted_iota(jnp.int32, (T, LANES), 1)
        half = lane < DA_HEAD_DIM
        cblk = jnp.zeros((2 * T, LANES), F32)
        lane2 = lax.broadcasted_iota(jnp.int32, (2 * T, LANES), 1)
        for i, part in enumerate(_LOG2E_BF16_PARTS):
            cblk = jnp.where((lane2 == i) | (lane2 == i + _N_PARTS), part, cblk)
        for h in range(DA_HEADS):
            sl = slice(h * DA_V_DIM, (h + 1) * DA_V_DIM)
            slope = 2.0 ** (-8.0 * (h + 1) / DA_HEADS)
            qh = q_ref[:, sl] * (DA_HEAD_DIM ** -0.5 * _LOG2E)
            both = jnp.concatenate([jnp.where(half, qh, 0.0), jnp.where(half, 0.0, qh)], axis=0)
            qa_scr[h, :, :LANES] = both.astype(BF16)
            qa_scr[h, :, LANES:] = (cblk * slope).astype(BF16)
        acc_scr[...] = jnp.zeros_like(acc_scr)
        l_scr[...] = jnp.zeros_like(l_scr)
        m_scr[...] = jnp.full_like(m_scr, NEG_BIG)

    def pos_block(tk, rel_start):
        rel0 = lax.broadcasted_iota(jnp.int32, (tk, LANES), 0) + rel_start
        lane_k = lax.broadcasted_iota(jnp.int32, (tk, LANES), 1)
        hi = ((rel0 >> 7) << 7).astype(F32)
        lo = (rel0 & 127).astype(F32)
        return jnp.where(lane_k < _N_PARTS, hi, jnp.where(lane_k < 2 * _N_PARTS, lo, 0.0)).astype(BF16)

    def head_update(h, kb, vb, pos_blk, adj):
        ka = jnp.concatenate([kb, pos_blk], axis=1)
        s = lax.dot_general(ka, qa_scr[h], _NT, preferred_element_type=F32)
        if adj is not None:
            s = s + adj * (2.0 ** (-8.0 * (h + 1) / DA_HEADS))
        m_old = m_scr[h]
        m_new = jnp.maximum(m_old, jnp.max(s, axis=0, keepdims=True))
        p = jnp.exp2(s - m_new)
        alpha = jnp.exp2(m_old - m_new)
        l_scr[h] = alpha * l_scr[h] + jnp.sum(p, axis=0, keepdims=True)
        acc_scr[h] = alpha * acc_scr[h] + lax.dot_general(
            vb, p.astype(BF16), _TN, preferred_element_type=F32)
        m_scr[h] = m_new

    @pl.when(t < n_past)
    def _():
        pos_blk = pos_block(tkp, t * tkp - P)
        for h in range(DA_HEADS):
            rows = pl.ds(h, tkp, stride=DA_HEADS)
            ka = jnp.concatenate([pk_ref[rows, :].astype(BF16), pos_blk], axis=1)
            s = lax.dot_general(ka, qa_scr[h], _NT, preferred_element_type=F32)
            s_scr[h] = s
            mx_scr[h] = jnp.max(s, axis=0, keepdims=True)
        for h in range(DA_HEADS):
            rows = pl.ds(h, tkp, stride=DA_HEADS)
            m_old = m_scr[h]
            m_new = jnp.maximum(m_old, mx_scr[h])
            p = jnp.exp2(s_scr[h] - m_new)
            alpha = jnp.exp2(m_old - m_new)
            l_scr[h] = alpha * l_scr[h] + jnp.sum(p, axis=0, keepdims=True)
            acc_scr[h] = alpha * acc_scr[h] + lax.dot_general(
                pv_ref[rows, :].astype(BF16), p.astype(BF16), _TN, preferred_element_type=F32)
            m_scr[h] = m_new

    @pl.when(t == n_past)
    def _():
        pos_blk = pos_block(T, 0)
        kidx = lax.broadcasted_iota(jnp.int32, (T, 2 * T), 0)
        qidx = lax.broadcasted_iota(jnp.int32, (T, 2 * T), 1) & (T - 1)
        rel = kidx - qidx
        visible = ((kidx + P) >> 6) <= ((qidx + P) >> 6)
        adj = jnp.where(visible, jnp.maximum(rel, 0).astype(F32) * (-2.0 * _LOG2E), NEG_BIG)
        for h in range(DA_HEADS):
            rows = pl.ds(h, T, stride=DA_HEADS)
            head_update(h, k_ref[rows, :].astype(BF16), v_ref[rows, :].astype(BF16), pos_blk, adj)

        lv = lam_ref[...]
        lam = (jnp.exp(jnp.sum(lv[0:1] * lv[1:2], axis=-1, keepdims=True))
               - jnp.exp(jnp.sum(lv[2:3] * lv[3:4], axis=-1, keepdims=True)) + lam_init)
        for h in range(DA_HEADS):
            sl = slice(h * DA_V_DIM, (h + 1) * DA_V_DIM)
            a = (acc_scr[h] * (1.0 / l_scr[h])).T
            o = a[:T] - lam * a[T:]
            o = o * lax.rsqrt(jnp.mean(o * o, axis=-1, keepdims=True) + EPS) * sg_ref[...] * (1.0 - lam_init)
            zh = z_ref[:, sl]
            o_ref[:, sl] = o * (zh * _sigmoid(zh))


def _diff_attn_cache(q, z, k_new, v_new, past_k, past_v, lam_v, subln_g, B, T, lam_init, tkp=512):
    P = past_k.shape[1]
    assert 2 * T == LANES and T == CHUNK and P % tkp == 0 and P % CHUNK == 0
    n_past = P // tkp
    width = DA_HEADS * DA_V_DIM
    tok = lambda b, t: (b, 0)
    past = lambda b, t: (b, jnp.minimum(t, n_past - 1), 0)
    const = lambda b, t: (0, 0)
    pk = past_k.reshape(B, P * DA_HEADS, 2 * DA_HEAD_DIM)
    pv = past_v.reshape(B, P * DA_HEADS, DA_V_DIM)
    return pl.pallas_call(
        functools.partial(_diff_attn_cache_body, P=P, T=T, tkp=tkp, lam_init=lam_init),
        grid=(B, n_past + 1),
        in_specs=[
            pl.BlockSpec((T, width), tok),
            pl.BlockSpec((None, tkp * DA_HEADS, LANES), past),
            pl.BlockSpec((None, tkp * DA_HEADS, LANES), past),
            pl.BlockSpec((T * DA_HEADS, LANES), tok),
            pl.BlockSpec((T * DA_HEADS, LANES), tok),
            pl.BlockSpec((T, width), tok),
            pl.BlockSpec((4, DA_HEAD_DIM), const),
            pl.BlockSpec((1, DA_V_DIM), const),
        ],
        out_specs=pl.BlockSpec((T, width), tok),
        out_shape=jax.ShapeDtypeStruct((B * T, width), F32),
        scratch_shapes=[
            pltpu.VMEM((DA_HEADS, 2 * T, 2 * LANES), BF16),
            pltpu.VMEM((DA_HEADS, DA_V_DIM, 2 * T), F32),
            pltpu.VMEM((DA_HEADS, 1, 2 * T), F32),
            pltpu.VMEM((DA_HEADS, 1, 2 * T), F32),
            pltpu.VMEM((DA_HEADS, tkp, 2 * T), F32),
            pltpu.VMEM((DA_HEADS, 1, 2 * T), F32),
        ],
        compiler_params=pltpu.CompilerParams(
            dimension_semantics=("arbitrary", "arbitrary"), vmem_limit_bytes=VMEM_LIMIT),
        name="diff_attn_cache",
    )(q, pk, pv, k_new, v_new, z, lam_v, subln_g.reshape(1, DA_V_DIM))


def _trunk(x, c0, n0, m0, past_k, past_v, mem_k, mem_v, wts):
    B, T, _ = x.shape
    P = 0 if past_k is None else past_k.shape[1]
    n_tok = B * T
    x2 = x.reshape(n_tok, D_MODEL)
    tm = 512

    q, k, v, o, z, mq, mz, gates = _norm_matmul(
        x2, wts["norm_g"][0], wts["w_a"], [ML_WIDTH] * 5 + [MEM_WIDTH] * 2, gates_w=wts["w_a_gates"],
        tm=tm, name="in_proj_a")
    L = min(T, 256)
    hm, c_new, n_new, m_new = _mlstm(
        q, k, v, o, z, gates, wts["b_gate"], wts["head_g"],
        c0, n0.reshape(B, ML_HEADS, 1, ML_HEAD_DIM),
        jnp.broadcast_to(m0.reshape(B, ML_HEADS, 1, 1), (B, ML_HEADS, 1, LANES)), B, T, L)
    x1 = _epilogue(x2, hm, mq, mz, mem_k[0], mem_v[0], wts["w_out_a1"], wts["w_out_a2"], B, T,
                   name="epilogue_a")

    kv_splits = [DA_HEADS * 2 * DA_HEAD_DIM, DA_HEADS * DA_V_DIM]
    qd, zd, mq2, mz2 = _norm_matmul(x1, wts["norm_g"][1], wts["w_b"],
                                    [DA_HEADS * 2 * DA_HEAD_DIM, DA_HEADS * DA_V_DIM, MEM_WIDTH, MEM_WIDTH],
                                    tm=tm, name="in_proj_b")
    lam_init = 0.8 - 0.6 * math.exp(-0.3 * 1)
    if past_k is None:
        k_new, v_new, k_heads, vt_heads = _norm_matmul(x1, wts["kv_norm_g"], wts["w_kv"], kv_splits,
                                                       head_major=((0, False), (1, True)),
                                                       interleave=(0, 1), tm=tm, name="kv_proj")
        od = _diff_attn_nocache(qd, zd, k_heads, vt_heads, wts["lam_b"], wts["subln_g"], B, T, lam_init)
    else:
        k_new, v_new = _norm_matmul(x1, wts["kv_norm_g"], wts["w_kv"], kv_splits, interleave=(0, 1),
                                    tm=tm, name="kv_proj")
        od = _diff_attn_cache(qd, zd, k_new, v_new, past_k, past_v, wts["lam_b"], wts["subln_g"], B, T,
                              lam_init)
    y = _epilogue(x1, od, mq2, mz2, mem_k[1], mem_v[1], wts["w_out_b1"], wts["w_out_b2"], B, T,
                  final_g=wts["final_norm_g"], name="epilogue_b")

    return (y.reshape(B, T, D_MODEL),
            c_new.reshape(1, B, ML_HEADS, ML_HEAD_DIM, ML_HEAD_DIM),
            n_new.reshape(1, B, ML_HEADS, ML_HEAD_DIM),
            m_new[..., 0, 0].reshape(1, B, ML_HEADS),
            k_new.reshape(B, T, DA_HEADS, 2 * DA_HEAD_DIM),
            v_new.reshape(B, T, DA_HEADS, DA_V_DIM))


def kernel(x_prompt, x_sample, cache_k, cache_v, cache_mem_k, cache_mem_v, state_C, state_n, state_m, mem_prompt, norm_g, final_norm_g, mem_norm_g, w_mem_kv, w_in_a, b_gate_a, head_g_a, w_out_a, kv_norm_g, w_kv, w_in_b, lam_b, subln_g_b, w_out_b):
    B = x_prompt.shape[0]
    DB = x_sample.shape[0]
    n_gate = 2 * ML_HEADS
    g0 = 5 * ML_WIDTH
    w_a = w_in_a[0]
    wts = {
        "norm_g": norm_g,
        "final_norm_g": final_norm_g,
        "kv_norm_g": kv_norm_g,
        "w_a": jnp.concatenate([w_a[:, :g0], w_a[:, g0 + n_gate:]], axis=1).astype(BF16),
        "w_a_gates": jnp.pad(w_a[:, g0:g0 + n_gate], ((0, 0), (0, LANES - n_gate))),
        "b_gate": jnp.pad(b_gate_a[0], (0, LANES - n_gate)).reshape(1, LANES),
        "head_g": head_g_a[0].reshape(1, ML_WIDTH),
        "w_out_a1": w_out_a[0, :ML_WIDTH].astype(BF16),
        "w_out_a2": w_out_a[0, ML_WIDTH:].astype(BF16),
        "w_kv": w_kv.astype(BF16),
        "w_b": w_in_b[0].astype(BF16),
        "lam_b": lam_b[0],
        "subln_g": subln_g_b[0],
        "w_out_b1": w_out_b[0, :DA_HEADS * DA_V_DIM].astype(BF16),
        "w_out_b2": w_out_b[0, DA_HEADS * DA_V_DIM:].astype(BF16),
    }

    mem2 = mem_prompt.reshape(B * MEM_LEN, D_MODEL)
    mks, mvs = [], []
    for l in range(2):
        mk, mv = _norm_matmul(mem2, mem_norm_g[l], w_mem_kv[l].astype(BF16), [MEM_WIDTH, MEM_WIDTH],
                              interleave=(0, 1), tm=256, name="mem_kv")
        mks.append(mk.reshape(B, MEM_LEN * MEM_HEADS, MEM_HEAD_DIM))
        mvs.append(mv.reshape(B, MEM_LEN * MEM_HEADS, MEM_HEAD_DIM))
    prompt_mem_k = jnp.stack(mks).reshape(2, B, MEM_LEN, MEM_HEADS, MEM_HEAD_DIM)
    prompt_mem_v = jnp.stack(mvs).reshape(2, B, MEM_LEN, MEM_HEADS, MEM_HEAD_DIM)

    zc = jnp.zeros((B, ML_HEADS, ML_HEAD_DIM, ML_HEAD_DIM), F32)
    zn = jnp.zeros((B, ML_HEADS, ML_HEAD_DIM), F32)
    zm = jnp.zeros((B, ML_HEADS), F32)
    y_prompt, prompt_C, prompt_n, prompt_m, prompt_k, prompt_v = _trunk(
        x_prompt, zc, zn, zm, None, None, mks, mvs, wts)

    smk = [cache_mem_k[l].reshape(DB, MEM_LEN * MEM_HEADS, MEM_HEAD_DIM) for l in range(2)]
    smv = [cache_mem_v[l].reshape(DB, MEM_LEN * MEM_HEADS, MEM_HEAD_DIM) for l in range(2)]
    y_sample, sample_C, sample_n, sample_m, sample_k, sample_v = _trunk(
        x_sample, state_C[0], state_n[0], state_m[0], cache_k, cache_v, smk, smv, wts)

    return (y_prompt, y_sample, prompt_C, prompt_n, prompt_m, prompt_k, prompt_v, prompt_mem_k, prompt_mem_v,
            sample_C, sample_n, sample_m, sample_k, sample_v)
```

```python
import functools
import math

import numpy as np
import jax
import jax.numpy as jnp
from jax import lax
from jax.experimental import pallas as pl
from jax.experimental.pallas import tpu as pltpu

F32 = jnp.float32
BF16 = jnp.bfloat16
HIGHEST = lax.Precision.HIGHEST

D_MODEL = 1024
CHUNK = 64
ML_HEADS = 4
ML_HEAD_DIM = 256
ML_WIDTH = 1024
DA_HEADS = 8
DA_HEAD_DIM = 64
DA_V_DIM = 128
MEM_LEN = 256
MEM_HEADS = 4
MEM_HEAD_DIM = 128
MEM_WIDTH = 512
EPS = 1e-6
LANES = 128
ONES_ROWS = 16
ROW_TILE = 512
NEG_BIG = -1e30
VMEM_LIMIT = 56 * 1024 * 1024

_NT = (((1,), (1,)), ((), ()))
_TN = (((0,), (0,)), ((), ()))


def _sigmoid(x):
    return 1.0 / (1.0 + jnp.exp(-x))


def _norm_matmul_body(*refs, splits, with_gates, head_major, interleave):
    if with_gates:
        x_ref, g_ref, w_ref, wg_ref = refs[:4]
        outs = refs[4:]
        gate_out = outs[-1]
        outs = outs[:-1]
    else:
        x_ref, g_ref, w_ref = refs[:3]
        outs = refs[3:]
    hm_outs = outs[len(splits):]
    outs = outs[:len(splits)]
    x = x_ref[...]
    xn = x * lax.rsqrt(jnp.mean(x * x, axis=-1, keepdims=True) + EPS) * g_ref[...]
    xb = xn.astype(BF16)
    off = 0
    for i, (o_ref, width) in enumerate(zip(outs, splits)):
        r = jnp.dot(xb, w_ref[:, off:off + width], preferred_element_type=F32)
        if i in interleave:
            nh = width // LANES
            for h in range(nh):
                o_ref[pl.ds(h, x.shape[0], stride=nh), :] = r[:, h * LANES:(h + 1) * LANES]
        else:
            o_ref[...] = r
        for (split, transposed), hb_ref in zip(head_major, hm_outs):
            if split == i:
                for h in range(width // LANES):
                    rh = r[:, h * LANES:(h + 1) * LANES]
                    if transposed:
                        hb_ref[h, :LANES, :] = rh.T.astype(BF16)
                        extra = lax.broadcasted_iota(jnp.int32, (ONES_ROWS, rh.shape[0]), 0) == 0
                        hb_ref[h, LANES:, :] = jnp.where(extra, 1.0, 0.0).astype(BF16)
                    else:
                        hb_ref[h] = rh.astype(BF16)
        off += width
    if with_gates:
        x_lo = (xn - xb.astype(F32)).astype(BF16)
        g_hi = jnp.dot(xb, wg_ref[...], preferred_element_type=F32)
        g_lo = jnp.dot(x_lo, wg_ref[:, :LANES], preferred_element_type=F32)
        gate_out[...] = g_hi[:, :LANES] + g_hi[:, LANES:] + g_lo


def _norm_matmul(x, g, w_bf16, splits, gates_w=None, head_major=(), interleave=(), tm=256,
                 name="norm_matmul"):
    n, d = x.shape
    width = w_bf16.shape[1]
    assert sum(splits) == width and n % tm == 0
    with_gates = gates_w is not None
    head_major = tuple(head_major)
    in_specs = [
        pl.BlockSpec((tm, d), lambda i: (i, 0)),
        pl.BlockSpec((1, d), lambda i: (0, 0)),
        pl.BlockSpec((d, width), lambda i: (0, 0), pipeline_mode=pl.Buffered(1)),
    ]
    args = [x, g.reshape(1, d), w_bf16]
    interleave = tuple(interleave)
    out_shape, out_specs = [], []
    for i, s in enumerate(splits):
        rows, cols = (s // LANES, LANES) if i in interleave else (1, s)
        out_shape.append(jax.ShapeDtypeStruct((n * rows, cols), F32))
        out_specs.append(pl.BlockSpec((tm * rows, cols), lambda i: (i, 0)))
    for split, transposed in head_major:
        nh = splits[split] // LANES
        if transposed:
            out_shape.append(jax.ShapeDtypeStruct((nh, LANES + ONES_ROWS, n), BF16))
            out_specs.append(pl.BlockSpec((nh, LANES + ONES_ROWS, tm), lambda i: (0, 0, i)))
        else:
            out_shape.append(jax.ShapeDtypeStruct((nh, n, LANES), BF16))
            out_specs.append(pl.BlockSpec((nh, tm, LANES), lambda i: (0, i, 0)))
    if with_gates:
        in_specs.append(pl.BlockSpec((d, 2 * LANES), lambda i: (0, 0)))
        args.append(gates_w)
        out_shape.append(jax.ShapeDtypeStruct((n, LANES), F32))
        out_specs.append(pl.BlockSpec((tm, LANES), lambda i: (i, 0)))
    return pl.pallas_call(
        functools.partial(_norm_matmul_body, splits=tuple(splits), with_gates=with_gates,
                          head_major=head_major, interleave=interleave),
        grid=(n // tm,),
        in_specs=in_specs,
        out_specs=out_specs,
        out_shape=out_shape,
        compiler_params=pltpu.CompilerParams(
            dimension_semantics=("arbitrary",), vmem_limit_bytes=VMEM_LIMIT),
        name=name,
    )(*args)


def _mlstm_body(q_ref, k_ref, v_ref, o_ref, z_ref, gt_ref, bg_ref, hg_ref, c0_ref, n0_ref, m0_ref,
                h_out, c_out, n_out, m_out, c_s, n_s, m_s, *, L, nc):
    c = pl.program_id(1)

    @pl.when(c == 0)
    def _():
        c_s[...] = c0_ref[0]
        n_s[...] = n0_ref[0]
        m_s[...] = m0_ref[0]

    gc = gt_ref[...] + bg_ref[...]
    lane = lax.broadcasted_iota(jnp.int32, (L, LANES), 1)
    lf = jnp.minimum(gc, 0.0) - jnp.log1p(jnp.exp(-jnp.abs(gc)))
    row = lax.broadcasted_iota(jnp.int32, (L, L), 0)
    col = lax.broadcasted_iota(jnp.int32, (L, L), 1)
    tril = col <= row
    lf_hi = lf.astype(BF16)
    rem = lf - lf_hi.astype(F32)
    lf_mid = rem.astype(BF16)
    lf_lo = (rem - lf_mid.astype(F32)).astype(BF16)
    parts = jnp.dot(jnp.where(tril, 1.0, 0.0).astype(BF16), jnp.concatenate([lf_hi, lf_mid, lf_lo], axis=1),
                    preferred_element_type=F32)
    gcum = parts[:, :LANES] + parts[:, LANES:2 * LANES] + parts[:, 2 * LANES:]
    comb = jnp.where(lane < ML_HEADS, gc, gcum)
    rows = comb.T

    stash = []
    for h in range(ML_HEADS):
        sl = slice(h * ML_HEAD_DIM, (h + 1) * ML_HEAD_DIM)
        src_r = rows[h:h + 1, :] - rows[ML_HEADS + h:ML_HEADS + h + 1, :]
        g_c = comb[:, ML_HEADS + h:ML_HEADS + h + 1]
        m_prev = m_s[h][:, :1]

        dmat = jnp.where(tril, g_c + src_r, NEG_BIG)
        inter = g_c + m_prev
        m_t = jnp.maximum(inter, jnp.max(dmat, axis=-1, keepdims=True))
        w_intra = jnp.exp(dmat - m_t)
        w_inter = jnp.exp(inter - m_t)

        qb = q_ref[:, sl].astype(BF16)
        kb = (k_ref[:, sl] * (ML_HEAD_DIM ** -0.5)).astype(BF16)
        qk = lax.dot_general(qb, kb, _NT, preferred_element_type=F32)
        s = w_intra * qk
        stash.append((m_t, w_inter, s.astype(BF16), jnp.sum(s, axis=-1, keepdims=True)))

    for h in range(ML_HEADS):
        sl = slice(h * ML_HEAD_DIM, (h + 1) * ML_HEAD_DIM)
        m_t, w_inter, sb, s_sum = stash[h]
        ig_c = comb[:, h:h + 1]
        g_c = comb[:, ML_HEADS + h:ML_HEADS + h + 1]
        m_prev = m_s[h][:, :1]
        qh = q_ref[:, sl]
        kh = k_ref[:, sl] * (ML_HEAD_DIM ** -0.5)
        vh = v_ref[:, sl]
        qb = qh.astype(BF16)
        kb = kh.astype(BF16)
        vb = vh.astype(BF16)
        ch = c_s[h]
        nh = n_s[h]
        cq = lax.dot_general(qb, ch.astype(BF16), _NT, preferred_element_type=F32)
        num = w_inter * cq + jnp.dot(sb, vb, preferred_element_type=F32)
        nq = jnp.sum(qh * nh, axis=-1, keepdims=True)
        den = w_inter * nq + s_sum
        hh = num * (1.0 / jnp.maximum(jnp.abs(den), jnp.exp(-m_t)))

        g_last = g_c[L - 1:L, :]
        m_new = m_t[L - 1:L, :]
        w_s = jnp.exp(g_last - g_c + ig_c - m_new)
        dec = jnp.exp(g_last + m_prev - m_new)
        vw = (vh * w_s).astype(BF16)
        c_s[h] = dec * ch + lax.dot_general(vw, kb, _TN, preferred_element_type=F32)
        n_s[h] = dec * nh + jnp.sum(kh * w_s, axis=0, keepdims=True)
        m_s[h] = jnp.broadcast_to(m_new, (1, LANES))

        oh = o_ref[:, sl]
        zh = z_ref[:, sl]
        hm = _sigmoid(oh) * hh
        hm = hm * lax.rsqrt(jnp.mean(hm * hm, axis=-1, keepdims=True) + EPS) * hg_ref[:, sl]
        h_out[:, sl] = hm * (zh * _sigmoid(zh))

    @pl.when(c == nc - 1)
    def _():
        c_out[0] = c_s[...]
        n_out[0] = n_s[...]
        m_out[0] = m_s[...]


def _mlstm(q, k, v, o, z, gates, b_gate, head_g, c0, n0, m0, B, T, L):
    nc = T // L
    dh = ML_HEAD_DIM
    tok = lambda b, c: (b * nc + c, 0)
    st4 = lambda b, c: (b, 0, 0, 0)
    return pl.pallas_call(
        functools.partial(_mlstm_body, L=L, nc=nc),
        grid=(B, nc),
        in_specs=[pl.BlockSpec((L, ML_WIDTH), tok)] * 5 + [
            pl.BlockSpec((L, LANES), tok),
            pl.BlockSpec((1, LANES), lambda b, c: (0, 0)),
            pl.BlockSpec((1, ML_WIDTH), lambda b, c: (0, 0)),
            pl.BlockSpec((1, ML_HEADS, dh, dh), st4),
            pl.BlockSpec((1, ML_HEADS, 1, dh), st4),
            pl.BlockSpec((1, ML_HEADS, 1, LANES), st4),
        ],
        out_specs=[
            pl.BlockSpec((L, ML_WIDTH), tok),
            pl.BlockSpec((1, ML_HEADS, dh, dh), st4),
            pl.BlockSpec((1, ML_HEADS, 1, dh), st4),
            pl.BlockSpec((1, ML_HEADS, 1, LANES), st4),
        ],
        out_shape=[
            jax.ShapeDtypeStruct((B * T, ML_WIDTH), F32),
            jax.ShapeDtypeStruct((B, ML_HEADS, dh, dh), F32),
            jax.ShapeDtypeStruct((B, ML_HEADS, 1, dh), F32),
            jax.ShapeDtypeStruct((B, ML_HEADS, 1, LANES), F32),
        ],
        scratch_shapes=[
            pltpu.VMEM((ML_HEADS, dh, dh), F32),
            pltpu.VMEM((ML_HEADS, 1, dh), F32),
            pltpu.VMEM((ML_HEADS, 1, LANES), F32),
        ],
        compiler_params=pltpu.CompilerParams(
            dimension_semantics=("arbitrary", "arbitrary"), vmem_limit_bytes=VMEM_LIMIT),
        name="mlstm",
    )(q, k, v, o, z, gates, b_gate, head_g, c0, n0, m0)


def _epilogue_body(*refs, final_norm, nb, rows_per_batch):
    if final_norm:
        x_ref, a_ref, mq_ref, mz_ref, mk_ref, mv_ref, w1_ref, w2_ref, fg_ref, y_ref = refs
    else:
        x_ref, a_ref, mq_ref, mz_ref, mk_ref, mv_ref, w1_ref, w2_ref, y_ref = refs
    acc = x_ref[...] + jnp.dot(a_ref[...].astype(BF16), w1_ref[...], preferred_element_type=F32)
    def tok(bi):
        return slice(bi * rows_per_batch, (bi + 1) * rows_per_batch)

    def head_rows(h):
        return pl.ds(h, MEM_LEN, stride=MEM_HEADS)

    scores = {}
    for bi in range(nb):
        for h in range(MEM_HEADS):
            sl = slice(h * MEM_HEAD_DIM, (h + 1) * MEM_HEAD_DIM)
            qh = mq_ref[tok(bi), sl].astype(BF16)
            kh = mk_ref.at[bi][head_rows(h), :].astype(BF16)
            scores[bi, h] = lax.dot_general(qh, kh, _NT, preferred_element_type=F32) * (MEM_HEAD_DIM ** -0.5)
    mo_rows = []
    for bi in range(nb):
        mos = []
        for h in range(MEM_HEADS):
            sl = slice(h * MEM_HEAD_DIM, (h + 1) * MEM_HEAD_DIM)
            s = scores[bi, h]
            e = jnp.exp(s - jnp.max(s, axis=-1, keepdims=True))
            p = e * (1.0 / jnp.sum(e, axis=-1, keepdims=True))
            vh = mv_ref.at[bi][head_rows(h), :].astype(BF16)
            oh = jnp.dot(p.astype(BF16), vh, preferred_element_type=F32)
            zh = mz_ref[tok(bi), sl]
            mos.append((oh * (zh * _sigmoid(zh))).astype(BF16))
        mo_rows.append(jnp.concatenate(mos, axis=-1))
    mo = mo_rows[0] if nb == 1 else jnp.concatenate(mo_rows, axis=0)
    acc = acc + jnp.dot(mo, w2_ref[...], preferred_element_type=F32)
    if final_norm:
        acc = acc * lax.rsqrt(jnp.mean(acc * acc, axis=-1, keepdims=True) + EPS) * fg_ref[...]
    y_ref[...] = acc


def _epilogue(x, a, mq, mz, mem_k, mem_v, w1, w2, B, T, final_g=None, name="epilogue"):
    if T >= ROW_TILE:
        nb, rows_per_batch, tm = 1, ROW_TILE, ROW_TILE
    else:
        nb, rows_per_batch, tm = min(B, ROW_TILE // T), T, min(B, ROW_TILE // T) * T
    assert T % rows_per_batch == 0 and B % nb == 0
    nt = T // rows_per_batch
    tok = lambda b, i: (b * nt + i, 0)
    const = lambda b, i: (0, 0)
    final_norm = final_g is not None
    in_specs = [
        pl.BlockSpec((tm, D_MODEL), tok),
        pl.BlockSpec((tm, a.shape[1]), tok),
        pl.BlockSpec((tm, MEM_WIDTH), tok),
        pl.BlockSpec((tm, MEM_WIDTH), tok),
        pl.BlockSpec((nb, MEM_LEN * MEM_HEADS, MEM_HEAD_DIM), lambda b, i: (b, 0, 0)),
        pl.BlockSpec((nb, MEM_LEN * MEM_HEADS, MEM_HEAD_DIM), lambda b, i: (b, 0, 0)),
        pl.BlockSpec(w1.shape, const, pipeline_mode=pl.Buffered(1)),
        pl.BlockSpec(w2.shape, const, pipeline_mode=pl.Buffered(1)),
    ]
    args = [x, a, mq, mz, mem_k, mem_v, w1, w2]
    if final_norm:
        in_specs.append(pl.BlockSpec((1, D_MODEL), const))
        args.append(final_g.reshape(1, D_MODEL))
    return pl.pallas_call(
        functools.partial(_epilogue_body, final_norm=final_norm, nb=nb, rows_per_batch=rows_per_batch),
        grid=(B // nb, nt),
        in_specs=in_specs,
        out_specs=pl.BlockSpec((tm, D_MODEL), tok),
        out_shape=jax.ShapeDtypeStruct((B * T, D_MODEL), F32),
        compiler_params=pltpu.CompilerParams(
            dimension_semantics=("arbitrary", "arbitrary"), vmem_limit_bytes=VMEM_LIMIT),
        name=name,
    )(*args)


_FLAG_FIRST, _FLAG_LAST, _FLAG_MASKED, _FLAG_PAST = 1, 2, 4, 8


def _attn_steps(P, T, tq, tkp, tkn):
    nq, n_past, n_new = T // tq, (P // tkp if P else 0), T // tkn
    qi_t, pj_t, nj_t, fl_t = [], [], [], []
    for qi in range(nq):
        q_lo = P + qi * tq
        q_hi = q_lo + tq - 1
        blocks = []
        for j in range(n_past):
            k_lo, k_hi = j * tkp, j * tkp + tkp - 1
            if k_lo // CHUNK <= q_hi // CHUNK:
                blocks.append((True, j, k_hi > q_lo))
        for j in range(n_new):
            k_lo, k_hi = P + j * tkn, P + j * tkn + tkn - 1
            if k_lo // CHUNK <= q_hi // CHUNK:
                blocks.append((False, j, k_hi > q_lo))
        first_new = next(j for past, j, _ in blocks if not past)
        last_past = 0
        for idx, (past, j, masked) in enumerate(blocks):
            flag = ((_FLAG_FIRST if idx == 0 else 0) | (_FLAG_LAST if idx == len(blocks) - 1 else 0)
                    | (_FLAG_MASKED if masked else 0) | (_FLAG_PAST if past else 0))
            if past:
                last_past = j
            qi_t.append(qi)
            pj_t.append(j if past else last_past)
            nj_t.append(first_new if past else j)
            fl_t.append(flag)
    as_i32 = lambda a: jnp.asarray(np.asarray(a, dtype=np.int32))
    return as_i32(qi_t), as_i32(pj_t), as_i32(nj_t), as_i32(fl_t), len(qi_t)


def _diff_attn_body(qi_ref, pj_ref, nj_ref, fl_ref, *refs, P, tq, tkp, tkn, lam_init, has_past):
    if has_past:
        q_ref, pk_ref, pv_ref, k_ref, v_ref, z_ref, lam_ref, sg_ref, o_ref, acc_ref, m_ref, l_ref = refs
    else:
        q_ref, k_ref, v_ref, z_ref, lam_ref, sg_ref, o_ref, acc_ref, m_ref, l_ref = refs
    t = pl.program_id(1)
    flags = fl_ref[t]
    q_start = P + qi_ref[t] * tq

    @pl.when((flags & _FLAG_FIRST) != 0)
    def _():
        acc_ref[...] = jnp.zeros_like(acc_ref)
        m_ref[...] = jnp.full_like(m_ref, NEG_BIG)
        l_ref[...] = jnp.zeros_like(l_ref)

    def attend(kr, vr, k_start, tk, masked):
        rel = (lax.broadcasted_iota(jnp.int32, (tq, tk), 1) - lax.broadcasted_iota(jnp.int32, (tq, tk), 0)
               + (k_start - q_start))
        if masked:
            qc = (lax.broadcasted_iota(jnp.int32, (tq, tk), 0) + q_start) // CHUNK
            kc = (lax.broadcasted_iota(jnp.int32, (tq, tk), 1) + k_start) // CHUNK
            visible = kc <= qc
            ndist = -jnp.abs(rel).astype(F32)
        else:
            ndist = rel.astype(F32)
        half = lax.broadcasted_iota(jnp.int32, (tq, DA_V_DIM), 1) < DA_HEAD_DIM
        for h in range(DA_HEADS):
            sl = slice(h * DA_V_DIM, (h + 1) * DA_V_DIM)
            slope = 2.0 ** (-8.0 * (h + 1) / DA_HEADS)
            bias = ndist * slope
            qh = q_ref[:, sl] * (DA_HEAD_DIM ** -0.5)
            kb = kr[:, sl].astype(BF16)
            vb = vr[:, sl].astype(BF16)
            for c in range(2):
                keep = half if c == 0 else jnp.logical_not(half)
                qc_b = jnp.where(keep, qh, 0.0).astype(BF16)
                s = lax.dot_general(qc_b, kb, _NT, preferred_element_type=F32) + bias
                if masked:
                    s = jnp.where(visible, s, NEG_BIG)
                idx = 2 * h + c
                m_old = m_ref[idx]
                m_new = jnp.maximum(m_old, jnp.max(s, axis=-1, keepdims=True))
                alpha = jnp.exp(m_old - m_new)
                p = jnp.exp(s - m_new)
                l_ref[idx] = alpha * l_ref[idx] + jnp.sum(p, axis=-1, keepdims=True)
                acc_ref[c, :, sl] = alpha * acc_ref[c, :, sl] + jnp.dot(
                    p.astype(BF16), vb, preferred_element_type=F32)
                m_ref[idx] = m_new

    is_masked = (flags & _FLAG_MASKED) != 0
    if has_past:
        is_past = (flags & _FLAG_PAST) != 0
        pk_start = pj_ref[t] * tkp

        @pl.when(is_past & is_masked)
        def _():
            attend(pk_ref, pv_ref, pk_start, tkp, True)

        @pl.when(is_past & jnp.logical_not(is_masked))
        def _():
            attend(pk_ref, pv_ref, pk_start, tkp, False)

        is_new = jnp.logical_not(is_past)
    else:
        is_new = True
    nk_start = P + nj_ref[t] * tkn

    @pl.when(is_new & is_masked)
    def _():
        attend(k_ref, v_ref, nk_start, tkn, True)

    @pl.when(is_new & jnp.logical_not(is_masked))
    def _():
        attend(k_ref, v_ref, nk_start, tkn, False)

    @pl.when((flags & _FLAG_LAST) != 0)
    def _():
        lv = lam_ref[...]
        lam = (jnp.exp(jnp.sum(lv[0:1] * lv[1:2], axis=-1, keepdims=True))
               - jnp.exp(jnp.sum(lv[2:3] * lv[3:4], axis=-1, keepdims=True)) + lam_init)
        for h in range(DA_HEADS):
            sl = slice(h * DA_V_DIM, (h + 1) * DA_V_DIM)
            o = (acc_ref[0, :, sl] * (1.0 / l_ref[2 * h])
                 - lam * (acc_ref[1, :, sl] * (1.0 / l_ref[2 * h + 1])))
            o = o * lax.rsqrt(jnp.mean(o * o, axis=-1, keepdims=True) + EPS) * sg_ref[...] * (1.0 - lam_init)
            zh = z_ref[:, sl]
            o_ref[:, sl] = o * (zh * _sigmoid(zh))


def _diff_attn(q, z, k_new, v_new, past_k, past_v, lam_v, subln_g, B, T, P, lam_init):
    tq = min(T, 512)
    tkn = tq
    tkp = 512
    has_past = P > 0
    assert T % tq == 0 and tq % CHUNK == 0 and (not has_past or P % tkp == 0)
    qi_t, pj_t, nj_t, fl_t, n_steps = _attn_steps(P, T, tq, tkp, tkn)
    nq, n_new = T // tq, T // tkn
    width = DA_HEADS * DA_V_DIM
    q_map = lambda b, t, qi, pj, nj, fl: (b * nq + qi[t], 0)
    new_map = lambda b, t, qi, pj, nj, fl: (b * n_new + nj[t], 0)
    const = lambda b, t, qi, pj, nj, fl: (0, 0)
    in_specs = [pl.BlockSpec((tq, width), q_map)]
    args = [q]
    if has_past:
        n_past = P // tkp
        past_map = lambda b, t, qi, pj, nj, fl: (b * n_past + pj[t], 0)
        in_specs += [pl.BlockSpec((tkp, width), past_map)] * 2
        args += [past_k, past_v]
    in_specs += [
        pl.BlockSpec((tkn, width), new_map),
        pl.BlockSpec((tkn, width), new_map),
        pl.BlockSpec((tq, width), q_map),
        pl.BlockSpec((4, DA_HEAD_DIM), const),
        pl.BlockSpec((1, DA_V_DIM), const),
    ]
    args += [k_new, v_new, z, lam_v, subln_g.reshape(1, DA_V_DIM)]
    grid_spec = pltpu.PrefetchScalarGridSpec(
        num_scalar_prefetch=4,
        grid=(B, n_steps),
        in_specs=in_specs,
        out_specs=pl.BlockSpec((tq, width), q_map),
        scratch_shapes=[
            pltpu.VMEM((2, tq, width), F32),
            pltpu.VMEM((2 * DA_HEADS, tq, 1), F32),
            pltpu.VMEM((2 * DA_HEADS, tq, 1), F32),
        ],
    )
    return pl.pallas_call(
        functools.partial(_diff_attn_body, P=P, tq=tq, tkp=tkp, tkn=tkn, lam_init=lam_init,
                          has_past=has_past),
        grid_spec=grid_spec,
        out_shape=jax.ShapeDtypeStruct((B * T, width), F32),
        compiler_params=pltpu.CompilerParams(
            dimension_semantics=("arbitrary", "arbitrary"), vmem_limit_bytes=VMEM_LIMIT),
        name="diff_attn",
    )(qi_t, pj_t, nj_t, fl_t, *args)


_LOG2E = 1.4426950216293335
_LOG2E_BF16_PARTS = (1.4453125, -0.00262451171875, 7.033348083496094e-06)
_N_PARTS = len(_LOG2E_BF16_PARTS)


def _diff_attn_nocache_body(qi_ref, nj_ref, fl_ref, q_ref, k_ref, vt_ref, z_ref, lam_ref, sg_ref, o_ref,
                            qa_scr, acc_scr, m_scr, pos_scr, adj_scr, s_scr, p_scr, al_scr, mx_scr,
                            *, tq, tk, lam_init):
    t = pl.program_id(1)
    flags = fl_ref[t]
    q_start = qi_ref[t] * tq
    k_start = nj_ref[t] * tk

    @pl.when((flags & _FLAG_FIRST) != 0)
    def _():
        row = lax.broadcasted_iota(jnp.int32, (LANES, tq), 0)
        half = row < DA_HEAD_DIM
        cblk = jnp.zeros((LANES, tq), F32)
        for i, part in enumerate(_LOG2E_BF16_PARTS):
            cblk = jnp.where((row == i) | (row == i + _N_PARTS), part, cblk)
        for h in range(DA_HEADS):
            sl = slice(h * DA_V_DIM, (h + 1) * DA_V_DIM)
            slope = 2.0 ** (-8.0 * (h + 1) / DA_HEADS)
            qh = (q_ref[:, sl] * (DA_HEAD_DIM ** -0.5 * _LOG2E)).T
            cb = (cblk * slope).astype(BF16)
            for c in range(2):
                keep = half if c == 0 else jnp.logical_not(half)
                qa_scr[2 * h + c, :LANES, :] = jnp.where(keep, qh, 0.0).astype(BF16)
                qa_scr[2 * h + c, LANES:, :] = cb
        acc_scr[...] = jnp.zeros_like(acc_scr)
        m_scr[...] = jnp.full_like(m_scr, NEG_BIG)

    rel0 = lax.broadcasted_iota(jnp.int32, (tk, LANES), 0) + (k_start - q_start)
    lane_k = lax.broadcasted_iota(jnp.int32, (tk, LANES), 1)
    hi = ((rel0 >> 7) << 7).astype(F32)
    lo = (rel0 & 127).astype(F32)
    pos_scr[...] = jnp.where(lane_k < _N_PARTS, hi, jnp.where(lane_k < 2 * _N_PARTS, lo, 0.0)).astype(BF16)

    n_maps = 2 * DA_HEADS

    def attend(general):
        if general:
            kidx = lax.broadcasted_iota(jnp.int32, (tk, tq), 0)
            qidx = lax.broadcasted_iota(jnp.int32, (tk, tq), 1)
            rel = kidx - qidx + (k_start - q_start)
            visible = ((kidx + k_start) >> 6) <= ((qidx + q_start) >> 6)
            adj_scr[...] = jnp.where(visible, jnp.maximum(rel, 0).astype(F32) * (-2.0 * _LOG2E), NEG_BIG)

        def scores(idx):
            h = idx // 2
            ka = jnp.concatenate([k_ref[h], pos_scr[...]], axis=1)
            s = jnp.dot(ka, qa_scr[idx], preferred_element_type=F32)
            if general:
                s = s + adj_scr[...] * (2.0 ** (-8.0 * (h + 1) / DA_HEADS))
            s_scr[idx % 2] = s
            mx_scr[idx % 2] = jnp.max(s, axis=0, keepdims=True)

        def softmax(idx):
            s = s_scr[idx % 2]
            m_old = m_scr[idx]
            m_new = jnp.maximum(m_old, mx_scr[idx % 2])
            p = jnp.exp2(s - m_new).astype(BF16)
            alpha = jnp.exp2(m_old - m_new)
            m_scr[idx] = m_new
            acc_scr[idx] = alpha * acc_scr[idx] + jnp.dot(
                vt_ref[idx // 2], p, preferred_element_type=F32)

        scores(0)
        for idx in range(n_maps):
            if idx + 1 < n_maps:
                scores(idx + 1)
            softmax(idx)

    is_general = (flags & _FLAG_MASKED) != 0

    @pl.when(is_general)
    def _():
        attend(True)

    @pl.when(jnp.logical_not(is_general))
    def _():
        attend(False)

    @pl.when((flags & _FLAG_LAST) != 0)
    def _():
        lv = lam_ref[...]
        lam = (jnp.exp(jnp.sum(lv[0:1] * lv[1:2], axis=-1, keepdims=True))
               - jnp.exp(jnp.sum(lv[2:3] * lv[3:4], axis=-1, keepdims=True)) + lam_init)
        for h in range(DA_HEADS):
            sl = slice(h * DA_V_DIM, (h + 1) * DA_V_DIM)
            a1 = acc_scr[2 * h]
            a2 = acc_scr[2 * h + 1]
            ot = (a1[:DA_V_DIM] * (1.0 / a1[DA_V_DIM:DA_V_DIM + 1])
                  - lam * (a2[:DA_V_DIM] * (1.0 / a2[DA_V_DIM:DA_V_DIM + 1])))
            o = ot.T
            o = o * lax.rsqrt(jnp.mean(o * o, axis=-1, keepdims=True) + EPS) * sg_ref[...] * (1.0 - lam_init)
            zh = z_ref[:, sl]
            o_ref[:, sl] = o * (zh * _sigmoid(zh))


def _diff_attn_nocache(q, z, k_heads, vt_heads, lam_v, subln_g, B, T, lam_init, tq=512, tk=512):
    tk = min(tk, T)
    assert T % tq == 0 and T % tk == 0 and tk % tq == 0 and tq % CHUNK == 0
    qi_t, _, nj_t, fl_t, n_steps = _attn_steps(0, T, tq, tk, tk)
    nq = T // tq
    nkv = T // tk
    width = DA_HEADS * DA_V_DIM
    q_map = lambda b, t, qi, nj, fl: (b * nq + qi[t], 0)
    k_map = lambda b, t, qi, nj, fl: (0, b * nkv + nj[t], 0)
    vt_map = lambda b, t, qi, nj, fl: (0, 0, b * nkv + nj[t])
    const = lambda b, t, qi, nj, fl: (0, 0)
    grid_spec = pltpu.PrefetchScalarGridSpec(
        num_scalar_prefetch=3,
        grid=(B, n_steps),
        in_specs=[
            pl.BlockSpec((tq, width), q_map),
            pl.BlockSpec((DA_HEADS, tk, LANES), k_map),
            pl.BlockSpec((DA_HEADS, DA_V_DIM + ONES_ROWS, tk), vt_map),
            pl.BlockSpec((tq, width), q_map),
            pl.BlockSpec((4, DA_HEAD_DIM), const),
            pl.BlockSpec((1, DA_V_DIM), const),
        ],
        out_specs=pl.BlockSpec((tq, width), q_map),
        scratch_shapes=[
            pltpu.VMEM((2 * DA_HEADS, 2 * LANES, tq), BF16),
            pltpu.VMEM((2 * DA_HEADS, DA_V_DIM + ONES_ROWS, tq), F32),
            pltpu.VMEM((2 * DA_HEADS, 1, tq), F32),
            pltpu.VMEM((tk, LANES), BF16),
            pltpu.VMEM((tk, tq), F32),
            pltpu.VMEM((2, tk, tq), F32),
            pltpu.VMEM((2, tk, tq), BF16),
            pltpu.VMEM((2, 1, tq), F32),
            pltpu.VMEM((2, 1, tq), F32),
        ],
    )
    return pl.pallas_call(
        functools.partial(_diff_attn_nocache_body, tq=tq, tk=tk, lam_init=lam_init),
        grid_spec=grid_spec,
        out_shape=jax.ShapeDtypeStruct((B * T, width), F32),
        compiler_params=pltpu.CompilerParams(
            dimension_semantics=("arbitrary", "arbitrary"), vmem_limit_bytes=VMEM_LIMIT),
        name="diff_attn_nocache",
    )(qi_t, nj_t, fl_t, q, k_heads, vt_heads, z, lam_v, subln_g.reshape(1, DA_V_DIM))


def _diff_attn_cache_body(q_ref, pk_ref, pv_ref, k_ref, v_ref, z_ref, lam_ref, sg_ref, o_ref,
                          qa_scr, acc_scr, m_scr, l_scr, s_scr, mx_scr, *, P, T, tkp, lam_init):
    t = pl.program_id(1)
    n_past = P // tkp

    @pl.when(t == 0)
    def _():
        lane = lax.broadcasted_iota(jnp.int32, (T, LANES), 1)
        half = lane < DA_HEAD_DIM
        cblk = jnp.zeros((2 * T, LANES), F32)
        lane2 = lax.broadcasted_iota(jnp.int32, (2 * T, LANES), 1)
        for i, part in enumerate(_LOG2E_BF16_PARTS):
            cblk = jnp.where((lane2 == i) | (lane2 == i + _N_PARTS), part, cblk)
        for h in range(DA_HEADS):
            sl = slice(h * DA_V_DIM, (h + 1) * DA_V_DIM)
            slope = 2.0 ** (-8.0 * (h + 1) / DA_HEADS)
            qh = q_ref[:, sl] * (DA_HEAD_DIM ** -0.5 * _LOG2E)
            both = jnp.concatenate([jnp.where(half, qh, 0.0), jnp.where(half, 0.0, qh)], axis=0)
            qa_scr[h, :, :LANES] = both.astype(BF16)
            qa_scr[h, :, LANES:] = (cblk * slope).astype(BF16)
        acc_scr[...] = jnp.zeros_like(acc_scr)
        l_scr[...] = jnp.zeros_like(l_scr)
        m_scr[...] = jnp.full_like(m_scr, NEG_BIG)

    def pos_block(tk, rel_start):
        rel0 = lax.broadcasted_iota(jnp.int32, (tk, LANES), 0) + rel_start
        lane_k = lax.broadcasted_iota(jnp.int32, (tk, LANES), 1)
        hi = ((rel0 >> 7) << 7).astype(F32)
        lo = (rel0 & 127).astype(F32)
        return jnp.where(lane_k < _N_PARTS, hi, jnp.where(lane_k < 2 * _N_PARTS, lo, 0.0)).astype(BF16)

    def head_update(h, kb, vb, pos_blk, adj):
        ka = jnp.concatenate([kb, pos_blk], axis=1)
        s = lax.dot_general(ka, qa_scr[h], _NT, preferred_element_type=F32)
        if adj is not None:
            s = s + adj * (2.0 ** (-8.0 * (h + 1) / DA_HEADS))
        m_old = m_scr[h]
        m_new = jnp.maximum(m_old, jnp.max(s, axis=0, keepdims=True))
        p = jnp.exp2(s - m_new)
        alpha = jnp.exp2(m_old - m_new)
        l_scr[h] = alpha * l_scr[h] + jnp.sum(p, axis=0, keepdims=True)
        acc_scr[h] = alpha * acc_scr[h] + lax.dot_general(
            vb, p.astype(BF16), _TN, preferred_element_type=F32)
        m_scr[h] = m_new

    @pl.when(t < n_past)
    def _():
        pos_blk = pos_block(tkp, t * tkp - P)
        for h in range(DA_HEADS):
            rows = pl.ds(h, tkp, stride=DA_HEADS)
            ka = jnp.concatenate([pk_ref[rows, :].astype(BF16), pos_blk], axis=1)
            s = lax.dot_general(ka, qa_scr[h], _NT, preferred_element_type=F32)
            s_scr[h] = s
            mx_scr[h] = jnp.max(s, axis=0, keepdims=True)
        for h in range(DA_HEADS):
            rows = pl.ds(h, tkp, stride=DA_HEADS)
            m_old = m_scr[h]
            m_new = jnp.maximum(m_old, mx_scr[h])
            p = jnp.exp2(s_scr[h] - m_new)
            alpha = jnp.exp2(m_old - m_new)
            l_scr[h] = alpha * l_scr[h] + jnp.sum(p, axis=0, keepdims=True)
            acc_scr[h] = alpha * acc_scr[h] + lax.dot_general(
                pv_ref[rows, :].astype(BF16), p.astype(BF16), _TN, preferred_element_type=F32)
            m_scr[h] = m_new

    @pl.when(t == n_past)
    def _():
        pos_blk = pos_block(T, 0)
        kidx = lax.broadcasted_iota(jnp.int32, (T, 2 * T), 0)
        qidx = lax.broadcasted_iota(jnp.int32, (T, 2 * T), 1) & (T - 1)
        rel = kidx - qidx
        visible = ((kidx + P) >> 6) <= ((qidx + P) >> 6)
        adj = jnp.where(visible, jnp.maximum(rel, 0).astype(F32) * (-2.0 * _LOG2E), NEG_BIG)
        for h in range(DA_HEADS):
            rows = pl.ds(h, T, stride=DA_HEADS)
            head_update(h, k_ref[rows, :].astype(BF16), v_ref[rows, :].astype(BF16), pos_blk, adj)

        lv = lam_ref[...]
        lam = (jnp.exp(jnp.sum(lv[0:1] * lv[1:2], axis=-1, keepdims=True))
               - jnp.exp(jnp.sum(lv[2:3] * lv[3:4], axis=-1, keepdims=True)) + lam_init)
        for h in range(DA_HEADS):
            sl = slice(h * DA_V_DIM, (h + 1) * DA_V_DIM)
            a = (acc_scr[h] * (1.0 / l_scr[h])).T
            o = a[:T] - lam * a[T:]
            o = o * lax.rsqrt(jnp.mean(o * o, axis=-1, keepdims=True) + EPS) * sg_ref[...] * (1.0 - lam_init)
            zh = z_ref[:, sl]
            o_ref[:, sl] = o * (zh * _sigmoid(zh))


def _diff_attn_cache(q, z, k_new, v_new, past_k, past_v, lam_v, subln_g, B, T, lam_init, tkp=1024):
    P = past_k.shape[1]
    assert 2 * T == LANES and T == CHUNK and P % tkp == 0 and P % CHUNK == 0
    n_past = P // tkp
    width = DA_HEADS * DA_V_DIM
    tok = lambda b, t: (b, 0)
    past = lambda b, t: (b, jnp.minimum(t, n_past - 1), 0)
    const = lambda b, t: (0, 0)
    pk = past_k.reshape(B, P * DA_HEADS, 2 * DA_HEAD_DIM)
    pv = past_v.reshape(B, P * DA_HEADS, DA_V_DIM)
    return pl.pallas_call(
        functools.partial(_diff_attn_cache_body, P=P, T=T, tkp=tkp, lam_init=lam_init),
        grid=(B, n_past + 1),
        in_specs=[
            pl.BlockSpec((T, width), tok),
            pl.BlockSpec((None, tkp * DA_HEADS, LANES), past),
            pl.BlockSpec((None, tkp * DA_HEADS, LANES), past),
            pl.BlockSpec((T * DA_HEADS, LANES), tok),
            pl.BlockSpec((T * DA_HEADS, LANES), tok),
            pl.BlockSpec((T, width), tok),
            pl.BlockSpec((4, DA_HEAD_DIM), const),
            pl.BlockSpec((1, DA_V_DIM), const),
        ],
        out_specs=pl.BlockSpec((T, width), tok),
        out_shape=jax.ShapeDtypeStruct((B * T, width), F32),
        scratch_shapes=[
            pltpu.VMEM((DA_HEADS, 2 * T, 2 * LANES), BF16),
            pltpu.VMEM((DA_HEADS, DA_V_DIM, 2 * T), F32),
            pltpu.VMEM((DA_HEADS, 1, 2 * T), F32),
            pltpu.VMEM((DA_HEADS, 1, 2 * T), F32),
            pltpu.VMEM((DA_HEADS, tkp, 2 * T), F32),
            pltpu.VMEM((DA_HEADS, 1, 2 * T), F32),
        ],
        compiler_params=pltpu.CompilerParams(
            dimension_semantics=("arbitrary", "arbitrary"), vmem_limit_bytes=VMEM_LIMIT),
        name="diff_attn_cache",
    )(q, pk, pv, k_new, v_new, z, lam_v, subln_g.reshape(1, DA_V_DIM))


def _trunk(x, c0, n0, m0, past_k, past_v, mem_k, mem_v, wts):
    B, T, _ = x.shape
    P = 0 if past_k is None else past_k.shape[1]
    n_tok = B * T
    x2 = x.reshape(n_tok, D_MODEL)
    tm = ROW_TILE

    q, k, v, o, z, mq, mz, gates = _norm_matmul(
        x2, wts["norm_g"][0], wts["w_a"], [ML_WIDTH] * 5 + [MEM_WIDTH] * 2, gates_w=wts["w_a_gates"],
        tm=tm, name="in_proj_a")
    L = min(T, 256)
    hm, c_new, n_new, m_new = _mlstm(
        q, k, v, o, z, gates, wts["b_gate"], wts["head_g"],
        c0, n0.reshape(B, ML_HEADS, 1, ML_HEAD_DIM),
        jnp.broadcast_to(m0.reshape(B, ML_HEADS, 1, 1), (B, ML_HEADS, 1, LANES)), B, T, L)
    x1 = _epilogue(x2, hm, mq, mz, mem_k[0], mem_v[0], wts["w_out_a1"], wts["w_out_a2"], B, T,
                   name="epilogue_a")

    kv_splits = [DA_HEADS * 2 * DA_HEAD_DIM, DA_HEADS * DA_V_DIM]
    qd, zd, mq2, mz2 = _norm_matmul(x1, wts["norm_g"][1], wts["w_b"],
                                    [DA_HEADS * 2 * DA_HEAD_DIM, DA_HEADS * DA_V_DIM, MEM_WIDTH, MEM_WIDTH],
                                    tm=tm, name="in_proj_b")
    lam_init = 0.8 - 0.6 * math.exp(-0.3 * 1)
    if past_k is None:
        k_new, v_new, k_heads, vt_heads = _norm_matmul(x1, wts["kv_norm_g"], wts["w_kv"], kv_splits,
                                                       head_major=((0, False), (1, True)),
                                                       interleave=(0, 1), tm=tm, name="kv_proj")
        od = _diff_attn_nocache(qd, zd, k_heads, vt_heads, wts["lam_b"], wts["subln_g"], B, T, lam_init)
    else:
        k_new, v_new = _norm_matmul(x1, wts["kv_norm_g"], wts["w_kv"], kv_splits, interleave=(0, 1),
                                    tm=tm, name="kv_proj")
        od = _diff_attn_cache(qd, zd, k_new, v_new, past_k, past_v, wts["lam_b"], wts["subln_g"], B, T,
                              lam_init)
    y = _epilogue(x1, od, mq2, mz2, mem_k[1], mem_v[1], wts["w_out_b1"], wts["w_out_b2"], B, T,
                  final_g=wts["final_norm_g"], name="epilogue_b")

    return (y.reshape(B, T, D_MODEL),
            c_new.reshape(1, B, ML_HEADS, ML_HEAD_DIM, ML_HEAD_DIM),
            n_new.reshape(1, B, ML_HEADS, ML_HEAD_DIM),
            m_new[..., 0, 0].reshape(1, B, ML_HEADS),
            k_new.reshape(B, T, DA_HEADS, 2 * DA_HEAD_DIM),
            v_new.reshape(B, T, DA_HEADS, DA_V_DIM))


def kernel(x_prompt, x_sample, cache_k, cache_v, cache_mem_k, cache_mem_v, state_C, state_n, state_m, mem_prompt, norm_g, final_norm_g, mem_norm_g, w_mem_kv, w_in_a, b_gate_a, head_g_a, w_out_a, kv_norm_g, w_kv, w_in_b, lam_b, subln_g_b, w_out_b):
    B = x_prompt.shape[0]
    DB = x_sample.shape[0]
    n_gate = 2 * ML_HEADS
    g0 = 5 * ML_WIDTH
    w_a = w_in_a[0]
    w_gates = jnp.pad(w_a[:, g0:g0 + n_gate], ((0, 0), (0, LANES - n_gate)))
    w_gates_hi = w_gates.astype(BF16)
    w_gates_lo = (w_gates - w_gates_hi.astype(F32)).astype(BF16)
    wts = {
        "norm_g": norm_g,
        "final_norm_g": final_norm_g,
        "kv_norm_g": kv_norm_g,
        "w_a": jnp.concatenate([w_a[:, :g0], w_a[:, g0 + n_gate:]], axis=1).astype(BF16),
        "w_a_gates": jnp.concatenate([w_gates_hi, w_gates_lo], axis=1),
        "b_gate": jnp.pad(b_gate_a[0], (0, LANES - n_gate)).reshape(1, LANES),
        "head_g": head_g_a[0].reshape(1, ML_WIDTH),
        "w_out_a1": w_out_a[0, :ML_WIDTH].astype(BF16),
        "w_out_a2": w_out_a[0, ML_WIDTH:].astype(BF16),
        "w_kv": w_kv.astype(BF16),
        "w_b": w_in_b[0].astype(BF16),
        "lam_b": lam_b[0],
        "subln_g": subln_g_b[0],
        "w_out_b1": w_out_b[0, :DA_HEADS * DA_V_DIM].astype(BF16),
        "w_out_b2": w_out_b[0, DA_HEADS * DA_V_DIM:].astype(BF16),
    }

    mem2 = mem_prompt.reshape(B * MEM_LEN, D_MODEL)
    mks, mvs = [], []
    for l in range(2):
        mk, mv = _norm_matmul(mem2, mem_norm_g[l], w_mem_kv[l].astype(BF16), [MEM_WIDTH, MEM_WIDTH],
                              interleave=(0, 1), tm=256, name="mem_kv")
        mks.append(mk.reshape(B, MEM_LEN * MEM_HEADS, MEM_HEAD_DIM))
        mvs.append(mv.reshape(B, MEM_LEN * MEM_HEADS, MEM_HEAD_DIM))
    prompt_mem_k = jnp.stack(mks).reshape(2, B, MEM_LEN, MEM_HEADS, MEM_HEAD_DIM)
    prompt_mem_v = jnp.stack(mvs).reshape(2, B, MEM_LEN, MEM_HEADS, MEM_HEAD_DIM)

    zc = jnp.zeros((B, ML_HEADS, ML_HEAD_DIM, ML_HEAD_DIM), F32)
    zn = jnp.zeros((B, ML_HEADS, ML_HEAD_DIM), F32)
    zm = jnp.zeros((B, ML_HEADS), F32)
    y_prompt, prompt_C, prompt_n, prompt_m, prompt_k, prompt_v = _trunk(
        x_prompt, zc, zn, zm, None, None, mks, mvs, wts)

    smk = [cache_mem_k[l].reshape(DB, MEM_LEN * MEM_HEADS, MEM_HEAD_DIM) for l in range(2)]
    smv = [cache_mem_v[l].reshape(DB, MEM_LEN * MEM_HEADS, MEM_HEAD_DIM) for l in range(2)]
    y_sample, sample_C, sample_n, sample_m, sample_k, sample_v = _trunk(
        x_sample, state_C[0], state_n[0], state_m[0], cache_k, cache_v, smk, smv, wts)

    return (y_prompt, y_sample, prompt_C, prompt_n, prompt_m, prompt_k, prompt_v, prompt_mem_k, prompt_mem_v,
            sample_C, sample_n, sample_m, sample_k, sample_v)
```

```python
import functools
import math

import numpy as np
import jax
import jax.numpy as jnp
from jax import lax
from jax.experimental import pallas as pl
from jax.experimental.pallas import tpu as pltpu

F32 = jnp.float32
BF16 = jnp.bfloat16
HIGHEST = lax.Precision.HIGHEST

D_MODEL = 1024
CHUNK = 64
ML_HEADS = 4
ML_HEAD_DIM = 256
ML_WIDTH = 1024
DA_HEADS = 8
DA_HEAD_DIM = 64
DA_V_DIM = 128
MEM_LEN = 256
MEM_HEADS = 4
MEM_HEAD_DIM = 128
MEM_WIDTH = 512
EPS = 1e-6
LANES = 128
ONES_ROWS = 16
ROW_TILE = 512
NEG_BIG = -1e30
VMEM_LIMIT = 56 * 1024 * 1024

_NT = (((1,), (1,)), ((), ()))
_TN = (((0,), (0,)), ((), ()))


def _sigmoid(x):
    return 1.0 / (1.0 + jnp.exp(-x))


def _norm_matmul_body(*refs, splits, with_gates, head_major, interleave):
    if with_gates:
        x_ref, g_ref, w_ref, wg_ref = refs[:4]
        outs = refs[4:]
        gate_out = outs[-1]
        outs = outs[:-1]
    else:
        x_ref, g_ref, w_ref = refs[:3]
        outs = refs[3:]
    hm_outs = outs[len(splits):]
    outs = outs[:len(splits)]
    x = x_ref[...]
    xn = x * lax.rsqrt(jnp.mean(x * x, axis=-1, keepdims=True) + EPS) * g_ref[...]
    xb = xn.astype(BF16)
    off = 0
    for i, (o_ref, width) in enumerate(zip(outs, splits)):
        r = jnp.dot(xb, w_ref[:, off:off + width], preferred_element_type=F32)
        if i in interleave:
            nh = width // LANES
            for h in range(nh):
                o_ref[pl.ds(h, x.shape[0], stride=nh), :] = r[:, h * LANES:(h + 1) * LANES]
        else:
            o_ref[...] = r
        for (split, transposed), hb_ref in zip(head_major, hm_outs):
            if split == i:
                for h in range(width // LANES):
                    rh = r[:, h * LANES:(h + 1) * LANES]
                    if transposed:
                        hb_ref[h, :LANES, :] = rh.T.astype(BF16)
                        extra = lax.broadcasted_iota(jnp.int32, (ONES_ROWS, rh.shape[0]), 0) == 0
                        hb_ref[h, LANES:, :] = jnp.where(extra, 1.0, 0.0).astype(BF16)
                    else:
                        hb_ref[h] = rh.astype(BF16)
        off += width
    if with_gates:
        x_lo = (xn - xb.astype(F32)).astype(BF16)
        g_hi = jnp.dot(xb, wg_ref[...], preferred_element_type=F32)
        g_lo = jnp.dot(x_lo, wg_ref[:, :LANES], preferred_element_type=F32)
        gate_out[...] = g_hi[:, :LANES] + g_hi[:, LANES:] + g_lo


def _norm_matmul(x, g, w_bf16, splits, gates_w=None, head_major=(), interleave=(), tm=256,
                 name="norm_matmul"):
    n, d = x.shape
    width = w_bf16.shape[1]
    assert sum(splits) == width and n % tm == 0
    with_gates = gates_w is not None
    head_major = tuple(head_major)
    in_specs = [
        pl.BlockSpec((tm, d), lambda i: (i, 0)),
        pl.BlockSpec((1, d), lambda i: (0, 0)),
        pl.BlockSpec((d, width), lambda i: (0, 0), pipeline_mode=pl.Buffered(1)),
    ]
    args = [x, g.reshape(1, d), w_bf16]
    interleave = tuple(interleave)
    out_shape, out_specs = [], []
    for i, s in enumerate(splits):
        rows, cols = (s // LANES, LANES) if i in interleave else (1, s)
        out_shape.append(jax.ShapeDtypeStruct((n * rows, cols), F32))
        out_specs.append(pl.BlockSpec((tm * rows, cols), lambda i: (i, 0)))
    for split, transposed in head_major:
        nh = splits[split] // LANES
        if transposed:
            out_shape.append(jax.ShapeDtypeStruct((nh, LANES + ONES_ROWS, n), BF16))
            out_specs.append(pl.BlockSpec((nh, LANES + ONES_ROWS, tm), lambda i: (0, 0, i)))
        else:
            out_shape.append(jax.ShapeDtypeStruct((nh, n, LANES), BF16))
            out_specs.append(pl.BlockSpec((nh, tm, LANES), lambda i: (0, i, 0)))
    if with_gates:
        in_specs.append(pl.BlockSpec((d, 2 * LANES), lambda i: (0, 0)))
        args.append(gates_w)
        out_shape.append(jax.ShapeDtypeStruct((n, LANES), F32))
        out_specs.append(pl.BlockSpec((tm, LANES), lambda i: (i, 0)))
    return pl.pallas_call(
        functools.partial(_norm_matmul_body, splits=tuple(splits), with_gates=with_gates,
                          head_major=head_major, interleave=interleave),
        grid=(n // tm,),
        in_specs=in_specs,
        out_specs=out_specs,
        out_shape=out_shape,
        compiler_params=pltpu.CompilerParams(
            dimension_semantics=("arbitrary",), vmem_limit_bytes=VMEM_LIMIT),
        name=name,
    )(*args)


def _mlstm_body(q_ref, k_ref, v_ref, o_ref, z_ref, gt_ref, bg_ref, hg_ref, c0_ref, n0_ref, m0_ref,
                h_out, c_out, n_out, m_out, c_s, n_s, m_s, *, L, nc):
    c = pl.program_id(1)

    @pl.when(c == 0)
    def _():
        c_s[...] = c0_ref[0]
        n_s[...] = n0_ref[0]
        m_s[...] = m0_ref[0]

    gc = gt_ref[...] + bg_ref[...]
    lane = lax.broadcasted_iota(jnp.int32, (L, LANES), 1)
    lf = jnp.minimum(gc, 0.0) - jnp.log1p(jnp.exp(-jnp.abs(gc)))
    row = lax.broadcasted_iota(jnp.int32, (L, L), 0)
    col = lax.broadcasted_iota(jnp.int32, (L, L), 1)
    tril = col <= row
    lf_hi = lf.astype(BF16)
    rem = lf - lf_hi.astype(F32)
    lf_mid = rem.astype(BF16)
    lf_lo = (rem - lf_mid.astype(F32)).astype(BF16)
    parts = jnp.dot(jnp.where(tril, 1.0, 0.0).astype(BF16), jnp.concatenate([lf_hi, lf_mid, lf_lo], axis=1),
                    preferred_element_type=F32)
    gcum = parts[:, :LANES] + parts[:, LANES:2 * LANES] + parts[:, 2 * LANES:]
    comb = jnp.where(lane < ML_HEADS, gc, gcum)
    rows = comb.T

    stash = []
    for h in range(ML_HEADS):
        sl = slice(h * ML_HEAD_DIM, (h + 1) * ML_HEAD_DIM)
        src_r = rows[h:h + 1, :] - rows[ML_HEADS + h:ML_HEADS + h + 1, :]
        g_c = comb[:, ML_HEADS + h:ML_HEADS + h + 1]
        m_prev = m_s[h][:, :1]

        dmat = jnp.where(tril, g_c + src_r, NEG_BIG)
        inter = g_c + m_prev
        m_t = jnp.maximum(inter, jnp.max(dmat, axis=-1, keepdims=True))
        w_intra = jnp.exp(dmat - m_t)
        w_inter = jnp.exp(inter - m_t)

        qb = q_ref[:, sl].astype(BF16)
        kb = (k_ref[:, sl] * (ML_HEAD_DIM ** -0.5)).astype(BF16)
        qk = lax.dot_general(qb, kb, _NT, preferred_element_type=F32)
        s = w_intra * qk
        stash.append((m_t, w_inter, s.astype(BF16), jnp.sum(s, axis=-1, keepdims=True)))

    for h in range(ML_HEADS):
        sl = slice(h * ML_HEAD_DIM, (h + 1) * ML_HEAD_DIM)
        m_t, w_inter, sb, s_sum = stash[h]
        ig_c = comb[:, h:h + 1]
        g_c = comb[:, ML_HEADS + h:ML_HEADS + h + 1]
        m_prev = m_s[h][:, :1]
        qh = q_ref[:, sl]
        kh = k_ref[:, sl] * (ML_HEAD_DIM ** -0.5)
        vh = v_ref[:, sl]
        qb = qh.astype(BF16)
        kb = kh.astype(BF16)
        vb = vh.astype(BF16)
        ch = c_s[h]
        nh = n_s[h]
        cq = lax.dot_general(qb, ch.astype(BF16), _NT, preferred_element_type=F32)
        num = w_inter * cq + jnp.dot(sb, vb, preferred_element_type=F32)
        nq = jnp.sum(qh * nh, axis=-1, keepdims=True)
        den = w_inter * nq + s_sum
        hh = num * (1.0 / jnp.maximum(jnp.abs(den), jnp.exp(-m_t)))

        g_last = g_c[L - 1:L, :]
        m_new = m_t[L - 1:L, :]
        w_s = jnp.exp(g_last - g_c + ig_c - m_new)
        dec = jnp.exp(g_last + m_prev - m_new)
        vw = (vh * w_s).astype(BF16)
        c_s[h] = dec * ch + lax.dot_general(vw, kb, _TN, preferred_element_type=F32)
        n_s[h] = dec * nh + jnp.sum(kh * w_s, axis=0, keepdims=True)
        m_s[h] = jnp.broadcast_to(m_new, (1, LANES))

        oh = o_ref[:, sl]
        zh = z_ref[:, sl]
        hm = _sigmoid(oh) * hh
        hm = hm * lax.rsqrt(jnp.mean(hm * hm, axis=-1, keepdims=True) + EPS) * hg_ref[:, sl]
        h_out[:, sl] = hm * (zh * _sigmoid(zh))

    @pl.when(c == nc - 1)
    def _():
        c_out[0] = c_s[...]
        n_out[0] = n_s[...]
        m_out[0] = m_s[...]


def _mlstm(q, k, v, o, z, gates, b_gate, head_g, c0, n0, m0, B, T, L):
    nc = T // L
    dh = ML_HEAD_DIM
    tok = lambda b, c: (b * nc + c, 0)
    st4 = lambda b, c: (b, 0, 0, 0)
    return pl.pallas_call(
        functools.partial(_mlstm_body, L=L, nc=nc),
        grid=(B, nc),
        in_specs=[pl.BlockSpec((L, ML_WIDTH), tok)] * 5 + [
            pl.BlockSpec((L, LANES), tok),
            pl.BlockSpec((1, LANES), lambda b, c: (0, 0)),
            pl.BlockSpec((1, ML_WIDTH), lambda b, c: (0, 0)),
            pl.BlockSpec((1, ML_HEADS, dh, dh), st4),
            pl.BlockSpec((1, ML_HEADS, 1, dh), st4),
            pl.BlockSpec((1, ML_HEADS, 1, LANES), st4),
        ],
        out_specs=[
            pl.BlockSpec((L, ML_WIDTH), tok),
            pl.BlockSpec((1, ML_HEADS, dh, dh), st4),
            pl.BlockSpec((1, ML_HEADS, 1, dh), st4),
            pl.BlockSpec((1, ML_HEADS, 1, LANES), st4),
        ],
        out_shape=[
            jax.ShapeDtypeStruct((B * T, ML_WIDTH), F32),
            jax.ShapeDtypeStruct((B, ML_HEADS, dh, dh), F32),
            jax.ShapeDtypeStruct((B, ML_HEADS, 1, dh), F32),
            jax.ShapeDtypeStruct((B, ML_HEADS, 1, LANES), F32),
        ],
        scratch_shapes=[
            pltpu.VMEM((ML_HEADS, dh, dh), F32),
            pltpu.VMEM((ML_HEADS, 1, dh), F32),
            pltpu.VMEM((ML_HEADS, 1, LANES), F32),
        ],
        compiler_params=pltpu.CompilerParams(
            dimension_semantics=("arbitrary", "arbitrary"), vmem_limit_bytes=VMEM_LIMIT),
        name="mlstm",
    )(q, k, v, o, z, gates, b_gate, head_g, c0, n0, m0)


def _epilogue_body(*refs, final_norm, nb, rows_per_batch):
    if final_norm:
        x_ref, a_ref, mq_ref, mz_ref, mk_ref, mv_ref, w1_ref, w2_ref, fg_ref, y_ref = refs
    else:
        x_ref, a_ref, mq_ref, mz_ref, mk_ref, mv_ref, w1_ref, w2_ref, y_ref = refs
    acc = x_ref[...] + jnp.dot(a_ref[...].astype(BF16), w1_ref[...], preferred_element_type=F32)
    def tok(bi):
        return slice(bi * rows_per_batch, (bi + 1) * rows_per_batch)

    def head_rows(h):
        return pl.ds(h, MEM_LEN, stride=MEM_HEADS)

    scores = {}
    for bi in range(nb):
        for h in range(MEM_HEADS):
            sl = slice(h * MEM_HEAD_DIM, (h + 1) * MEM_HEAD_DIM)
            qh = mq_ref[tok(bi), sl].astype(BF16)
            kh = mk_ref.at[bi][head_rows(h), :].astype(BF16)
            scores[bi, h] = lax.dot_general(qh, kh, _NT, preferred_element_type=F32) * (MEM_HEAD_DIM ** -0.5)
    mo_rows = []
    for bi in range(nb):
        mos = []
        for h in range(MEM_HEADS):
            sl = slice(h * MEM_HEAD_DIM, (h + 1) * MEM_HEAD_DIM)
            s = scores[bi, h]
            e = jnp.exp(s - jnp.max(s, axis=-1, keepdims=True))
            p = e * (1.0 / jnp.sum(e, axis=-1, keepdims=True))
            vh = mv_ref.at[bi][head_rows(h), :].astype(BF16)
            oh = jnp.dot(p.astype(BF16), vh, preferred_element_type=F32)
            zh = mz_ref[tok(bi), sl]
            mos.append((oh * (zh * _sigmoid(zh))).astype(BF16))
        mo_rows.append(jnp.concatenate(mos, axis=-1))
    mo = mo_rows[0] if nb == 1 else jnp.concatenate(mo_rows, axis=0)
    acc = acc + jnp.dot(mo, w2_ref[...], preferred_element_type=F32)
    if final_norm:
        acc = acc * lax.rsqrt(jnp.mean(acc * acc, axis=-1, keepdims=True) + EPS) * fg_ref[...]
    y_ref[...] = acc


def _epilogue(x, a, mq, mz, mem_k, mem_v, w1, w2, B, T, final_g=None, mem_layer=0, name="epilogue"):
    if T >= ROW_TILE:
        nb, rows_per_batch, tm = 1, ROW_TILE, ROW_TILE
    else:
        nb, rows_per_batch, tm = min(B, ROW_TILE // T), T, min(B, ROW_TILE // T) * T
    assert T % rows_per_batch == 0 and B % nb == 0
    nt = T // rows_per_batch
    tok = lambda b, i: (b * nt + i, 0)
    const = lambda b, i: (0, 0)
    mem_map = lambda b, i: (mem_layer * (B // nb) + b, 0, 0)
    final_norm = final_g is not None
    in_specs = [
        pl.BlockSpec((tm, D_MODEL), tok),
        pl.BlockSpec((tm, a.shape[1]), tok),
        pl.BlockSpec((tm, MEM_WIDTH), tok),
        pl.BlockSpec((tm, MEM_WIDTH), tok),
        pl.BlockSpec((nb, MEM_LEN * MEM_HEADS, MEM_HEAD_DIM), mem_map),
        pl.BlockSpec((nb, MEM_LEN * MEM_HEADS, MEM_HEAD_DIM), mem_map),
        pl.BlockSpec(w1.shape, const, pipeline_mode=pl.Buffered(1)),
        pl.BlockSpec(w2.shape, const, pipeline_mode=pl.Buffered(1)),
    ]
    args = [x, a, mq, mz, mem_k, mem_v, w1, w2]
    if final_norm:
        in_specs.append(pl.BlockSpec((1, D_MODEL), const))
        args.append(final_g.reshape(1, D_MODEL))
    return pl.pallas_call(
        functools.partial(_epilogue_body, final_norm=final_norm, nb=nb, rows_per_batch=rows_per_batch),
        grid=(B // nb, nt),
        in_specs=in_specs,
        out_specs=pl.BlockSpec((tm, D_MODEL), tok),
        out_shape=jax.ShapeDtypeStruct((B * T, D_MODEL), F32),
        compiler_params=pltpu.CompilerParams(
            dimension_semantics=("arbitrary", "arbitrary"), vmem_limit_bytes=VMEM_LIMIT),
        name=name,
    )(*args)


_FLAG_FIRST, _FLAG_LAST, _FLAG_MASKED, _FLAG_PAST = 1, 2, 4, 8
_KIND_SHIFT = 2


def _attn_steps(P, T, tq, tkp, tkn):
    nq, n_past, n_new = T // tq, (P // tkp if P else 0), T // tkn
    qi_t, pj_t, nj_t, fl_t = [], [], [], []
    for qi in range(nq):
        q_lo = P + qi * tq
        q_hi = q_lo + tq - 1
        blocks = []
        for j in range(n_past):
            k_lo, k_hi = j * tkp, j * tkp + tkp - 1
            if k_lo // CHUNK <= q_hi // CHUNK:
                blocks.append((True, j, k_hi > q_lo))
        for j in range(n_new):
            k_lo, k_hi = P + j * tkn, P + j * tkn + tkn - 1
            if k_lo // CHUNK <= q_hi // CHUNK:
                blocks.append((False, j, k_hi > q_lo))
        first_new = next(j for past, j, _ in blocks if not past)
        last_past = 0
        for idx, (past, j, masked) in enumerate(blocks):
            flag = ((_FLAG_FIRST if idx == 0 else 0) | (_FLAG_LAST if idx == len(blocks) - 1 else 0)
                    | (_FLAG_MASKED if masked else 0) | (_FLAG_PAST if past else 0))
            if past:
                last_past = j
            qi_t.append(qi)
            pj_t.append(j if past else last_past)
            nj_t.append(first_new if past else j)
            fl_t.append(flag)
    as_i32 = lambda a: jnp.asarray(np.asarray(a, dtype=np.int32))
    return as_i32(qi_t), as_i32(pj_t), as_i32(nj_t), as_i32(fl_t), len(qi_t)


def _diff_attn_body(qi_ref, pj_ref, nj_ref, fl_ref, *refs, P, tq, tkp, tkn, lam_init, has_past):
    if has_past:
        q_ref, pk_ref, pv_ref, k_ref, v_ref, z_ref, lam_ref, sg_ref, o_ref, acc_ref, m_ref, l_ref = refs
    else:
        q_ref, k_ref, v_ref, z_ref, lam_ref, sg_ref, o_ref, acc_ref, m_ref, l_ref = refs
    t = pl.program_id(1)
    flags = fl_ref[t]
    q_start = P + qi_ref[t] * tq

    @pl.when((flags & _FLAG_FIRST) != 0)
    def _():
        acc_ref[...] = jnp.zeros_like(acc_ref)
        m_ref[...] = jnp.full_like(m_ref, NEG_BIG)
        l_ref[...] = jnp.zeros_like(l_ref)

    def attend(kr, vr, k_start, tk, masked):
        rel = (lax.broadcasted_iota(jnp.int32, (tq, tk), 1) - lax.broadcasted_iota(jnp.int32, (tq, tk), 0)
               + (k_start - q_start))
        if masked:
            qc = (lax.broadcasted_iota(jnp.int32, (tq, tk), 0) + q_start) // CHUNK
            kc = (lax.broadcasted_iota(jnp.int32, (tq, tk), 1) + k_start) // CHUNK
            visible = kc <= qc
            ndist = -jnp.abs(rel).astype(F32)
        else:
            ndist = rel.astype(F32)
        half = lax.broadcasted_iota(jnp.int32, (tq, DA_V_DIM), 1) < DA_HEAD_DIM
        for h in range(DA_HEADS):
            sl = slice(h * DA_V_DIM, (h + 1) * DA_V_DIM)
            slope = 2.0 ** (-8.0 * (h + 1) / DA_HEADS)
            bias = ndist * slope
            qh = q_ref[:, sl] * (DA_HEAD_DIM ** -0.5)
            kb = kr[:, sl].astype(BF16)
            vb = vr[:, sl].astype(BF16)
            for c in range(2):
                keep = half if c == 0 else jnp.logical_not(half)
                qc_b = jnp.where(keep, qh, 0.0).astype(BF16)
                s = lax.dot_general(qc_b, kb, _NT, preferred_element_type=F32) + bias
                if masked:
                    s = jnp.where(visible, s, NEG_BIG)
                idx = 2 * h + c
                m_old = m_ref[idx]
                m_new = jnp.maximum(m_old, jnp.max(s, axis=-1, keepdims=True))
                alpha = jnp.exp(m_old - m_new)
                p = jnp.exp(s - m_new)
                l_ref[idx] = alpha * l_ref[idx] + jnp.sum(p, axis=-1, keepdims=True)
                acc_ref[c, :, sl] = alpha * acc_ref[c, :, sl] + jnp.dot(
                    p.astype(BF16), vb, preferred_element_type=F32)
                m_ref[idx] = m_new

    is_masked = (flags & _FLAG_MASKED) != 0
    if has_past:
        is_past = (flags & _FLAG_PAST) != 0
        pk_start = pj_ref[t] * tkp

        @pl.when(is_past & is_masked)
        def _():
            attend(pk_ref, pv_ref, pk_start, tkp, True)

        @pl.when(is_past & jnp.logical_not(is_masked))
        def _():
            attend(pk_ref, pv_ref, pk_start, tkp, False)

        is_new = jnp.logical_not(is_past)
    else:
        is_new = True
    nk_start = P + nj_ref[t] * tkn

    @pl.when(is_new & is_masked)
    def _():
        attend(k_ref, v_ref, nk_start, tkn, True)

    @pl.when(is_new & jnp.logical_not(is_masked))
    def _():
        attend(k_ref, v_ref, nk_start, tkn, False)

    @pl.when((flags & _FLAG_LAST) != 0)
    def _():
        lv = lam_ref[...]
        lam = (jnp.exp(jnp.sum(lv[0:1] * lv[1:2], axis=-1, keepdims=True))
               - jnp.exp(jnp.sum(lv[2:3] * lv[3:4], axis=-1, keepdims=True)) + lam_init)
        for h in range(DA_HEADS):
            sl = slice(h * DA_V_DIM, (h + 1) * DA_V_DIM)
            o = (acc_ref[0, :, sl] * (1.0 / l_ref[2 * h])
                 - lam * (acc_ref[1, :, sl] * (1.0 / l_ref[2 * h + 1])))
            o = o * lax.rsqrt(jnp.mean(o * o, axis=-1, keepdims=True) + EPS) * sg_ref[...] * (1.0 - lam_init)
            zh = z_ref[:, sl]
            o_ref[:, sl] = o * (zh * _sigmoid(zh))


def _diff_attn(q, z, k_new, v_new, past_k, past_v, lam_v, subln_g, B, T, P, lam_init):
    tq = min(T, 512)
    tkn = tq
    tkp = 512
    has_past = P > 0
    assert T % tq == 0 and tq % CHUNK == 0 and (not has_past or P % tkp == 0)
    qi_t, pj_t, nj_t, fl_t, n_steps = _attn_steps(P, T, tq, tkp, tkn)
    nq, n_new = T // tq, T // tkn
    width = DA_HEADS * DA_V_DIM
    q_map = lambda b, t, qi, pj, nj, fl: (b * nq + qi[t], 0)
    new_map = lambda b, t, qi, pj, nj, fl: (b * n_new + nj[t], 0)
    const = lambda b, t, qi, pj, nj, fl: (0, 0)
    in_specs = [pl.BlockSpec((tq, width), q_map)]
    args = [q]
    if has_past:
        n_past = P // tkp
        past_map = lambda b, t, qi, pj, nj, fl: (b * n_past + pj[t], 0)
        in_specs += [pl.BlockSpec((tkp, width), past_map)] * 2
        args += [past_k, past_v]
    in_specs += [
        pl.BlockSpec((tkn, width), new_map),
        pl.BlockSpec((tkn, width), new_map),
        pl.BlockSpec((tq, width), q_map),
        pl.BlockSpec((4, DA_HEAD_DIM), const),
        pl.BlockSpec((1, DA_V_DIM), const),
    ]
    args += [k_new, v_new, z, lam_v, subln_g.reshape(1, DA_V_DIM)]
    grid_spec = pltpu.PrefetchScalarGridSpec(
        num_scalar_prefetch=4,
        grid=(B, n_steps),
        in_specs=in_specs,
        out_specs=pl.BlockSpec((tq, width), q_map),
        scratch_shapes=[
            pltpu.VMEM((2, tq, width), F32),
            pltpu.VMEM((2 * DA_HEADS, tq, 1), F32),
            pltpu.VMEM((2 * DA_HEADS, tq, 1), F32),
        ],
    )
    return pl.pallas_call(
        functools.partial(_diff_attn_body, P=P, tq=tq, tkp=tkp, tkn=tkn, lam_init=lam_init,
                          has_past=has_past),
        grid_spec=grid_spec,
        out_shape=jax.ShapeDtypeStruct((B * T, width), F32),
        compiler_params=pltpu.CompilerParams(
            dimension_semantics=("arbitrary", "arbitrary"), vmem_limit_bytes=VMEM_LIMIT),
        name="diff_attn",
    )(qi_t, pj_t, nj_t, fl_t, *args)


_LOG2E = 1.4426950216293335
_LOG2E_BF16_PARTS = (1.4453125, -0.00262451171875, 7.033348083496094e-06)
_N_PARTS = len(_LOG2E_BF16_PARTS)


def _diff_attn_nocache_body(qi_ref, nj_ref, fl_ref, q_ref, k_ref, vt_ref, z_ref, lam_ref, sg_ref, o_ref,
                            qa_scr, acc_scr, m_scr, pos_scr, adj_scr, s_scr, mx_scr,
                            *, tq, tk, lam_init):
    t = pl.program_id(1)
    flags = fl_ref[t]
    q_start = qi_ref[t] * tq
    k_start = nj_ref[t] * tk

    @pl.when((flags & _FLAG_FIRST) != 0)
    def _():
        row = lax.broadcasted_iota(jnp.int32, (LANES, tq), 0)
        half = row < DA_HEAD_DIM
        cblk = jnp.zeros((LANES, tq), F32)
        for i, part in enumerate(_LOG2E_BF16_PARTS):
            cblk = jnp.where((row == i) | (row == i + _N_PARTS), part, cblk)
        for h in range(DA_HEADS):
            sl = slice(h * DA_V_DIM, (h + 1) * DA_V_DIM)
            slope = 2.0 ** (-8.0 * (h + 1) / DA_HEADS)
            qh = (q_ref[:, sl] * (DA_HEAD_DIM ** -0.5 * _LOG2E)).T
            cb = (cblk * slope).astype(BF16)
            for c in range(2):
                keep = half if c == 0 else jnp.logical_not(half)
                qa_scr[2 * h + c, :LANES, :] = jnp.where(keep, qh, 0.0).astype(BF16)
                qa_scr[2 * h + c, LANES:, :] = cb
        acc_scr[...] = jnp.zeros_like(acc_scr)
        m_scr[...] = jnp.full_like(m_scr, NEG_BIG)

    n_sub = tk // tq
    n_maps = 2 * DA_HEADS

    def attend(modes):
        live = [sb for sb in range(n_sub) if modes[sb] is not None]
        for sb in live:
            rel0 = lax.broadcasted_iota(jnp.int32, (tq, LANES), 0) + (k_start + sb * tq - q_start)
            lane_k = lax.broadcasted_iota(jnp.int32, (tq, LANES), 1)
            hi = ((rel0 >> 7) << 7).astype(F32)
            lo = (rel0 & 127).astype(F32)
            pos_scr[sb] = jnp.where(lane_k < _N_PARTS, hi,
                                    jnp.where(lane_k < 2 * _N_PARTS, lo, 0.0)).astype(BF16)
            if modes[sb] == 'diag':
                kidx = lax.broadcasted_iota(jnp.int32, (tq, tq), 0)
                qidx = lax.broadcasted_iota(jnp.int32, (tq, tq), 1)
                rel = kidx - qidx
                visible = ((kidx + q_start) >> 6) <= ((qidx + q_start) >> 6)
                adj_scr[...] = jnp.where(visible, jnp.maximum(rel, 0).astype(F32) * (-2.0 * _LOG2E), NEG_BIG)

        tasks = [(sb, idx) for sb in live for idx in range(n_maps)]

        def scores(n):
            sb, idx = tasks[n]
            h = idx // 2
            ka = jnp.concatenate([k_ref[h, sb * tq:(sb + 1) * tq, :], pos_scr[sb]], axis=1)
            s = jnp.dot(ka, qa_scr[idx], preferred_element_type=F32)
            if modes[sb] == 'diag':
                s = s + adj_scr[...] * (2.0 ** (-8.0 * (h + 1) / DA_HEADS))
            s_scr[n % 2] = s
            mx_scr[n % 2] = jnp.max(s, axis=0, keepdims=True)

        def softmax(n):
            sb, idx = tasks[n]
            s = s_scr[n % 2]
            m_old = m_scr[idx]
            m_new = jnp.maximum(m_old, mx_scr[n % 2])
            p = jnp.exp2(s - m_new).astype(BF16)
            alpha = jnp.exp2(m_old - m_new)
            m_scr[idx] = m_new
            acc_scr[idx] = alpha * acc_scr[idx] + jnp.dot(
                vt_ref[idx // 2, :, sb * tq:(sb + 1) * tq], p, preferred_element_type=F32)

        scores(0)
        for n in range(len(tasks)):
            if n + 1 < len(tasks):
                scores(n + 1)
            softmax(n)

    kind = flags >> _KIND_SHIFT
    for d in range(n_sub + 1):
        modes = ['past'] * n_sub if d == n_sub else ['past'] * d + ['diag'] + [None] * (n_sub - d - 1)
        pl.when(kind == d)(functools.partial(attend, modes))

    @pl.when((flags & _FLAG_LAST) != 0)
    def _():
        lv = lam_ref[...]
        lam = (jnp.exp(jnp.sum(lv[0:1] * lv[1:2], axis=-1, keepdims=True))
               - jnp.exp(jnp.sum(lv[2:3] * lv[3:4], axis=-1, keepdims=True)) + lam_init)
        for h in range(DA_HEADS):
            sl = slice(h * DA_V_DIM, (h + 1) * DA_V_DIM)
            a1 = acc_scr[2 * h]
            a2 = acc_scr[2 * h + 1]
            ot = (a1[:DA_V_DIM] * (1.0 / a1[DA_V_DIM:DA_V_DIM + 1])
                  - lam * (a2[:DA_V_DIM] * (1.0 / a2[DA_V_DIM:DA_V_DIM + 1])))
            o = ot.T
            o = o * lax.rsqrt(jnp.mean(o * o, axis=-1, keepdims=True) + EPS) * sg_ref[...] * (1.0 - lam_init)
            zh = z_ref[:, sl]
            o_ref[:, sl] = o * (zh * _sigmoid(zh))


def _nocache_steps(T, tq, n_sub):
    qi_t, nj_t, fl_t = [], [], []
    for qi in range(T // tq):
        last_j = qi // n_sub
        for j in range(last_j + 1):
            kind = n_sub if j < last_j else qi - j * n_sub
            qi_t.append(qi)
            nj_t.append(j)
            fl_t.append((_FLAG_FIRST if j == 0 else 0) | (_FLAG_LAST if j == last_j else 0)
                        | (kind << _KIND_SHIFT))
    as_i32 = lambda a: jnp.asarray(np.asarray(a, dtype=np.int32))
    return as_i32(qi_t), as_i32(nj_t), as_i32(fl_t), len(qi_t)


def _diff_attn_nocache(q, z, k_heads, vt_heads, lam_v, subln_g, B, T, lam_init, tq=512, tk=1024):
    tk = min(tk, T)
    assert T % tq == 0 and T % tk == 0 and tk % tq == 0 and tq % CHUNK == 0
    qi_t, nj_t, fl_t, n_steps = _nocache_steps(T, tq, tk // tq)
    nq = T // tq
    nkv = T // tk
    width = DA_HEADS * DA_V_DIM
    q_map = lambda b, t, qi, nj, fl: (b * nq + qi[t], 0)
    k_map = lambda b, t, qi, nj, fl: (0, b * nkv + nj[t], 0)
    vt_map = lambda b, t, qi, nj, fl: (0, 0, b * nkv + nj[t])
    const = lambda b, t, qi, nj, fl: (0, 0)
    grid_spec = pltpu.PrefetchScalarGridSpec(
        num_scalar_prefetch=3,
        grid=(B, n_steps),
        in_specs=[
            pl.BlockSpec((tq, width), q_map),
            pl.BlockSpec((DA_HEADS, tk, LANES), k_map),
            pl.BlockSpec((DA_HEADS, DA_V_DIM + ONES_ROWS, tk), vt_map),
            pl.BlockSpec((tq, width), q_map),
            pl.BlockSpec((4, DA_HEAD_DIM), const),
            pl.BlockSpec((1, DA_V_DIM), const),
        ],
        out_specs=pl.BlockSpec((tq, width), q_map),
        scratch_shapes=[
            pltpu.VMEM((2 * DA_HEADS, 2 * LANES, tq), BF16),
            pltpu.VMEM((2 * DA_HEADS, DA_V_DIM + ONES_ROWS, tq), F32),
            pltpu.VMEM((2 * DA_HEADS, 1, tq), F32),
            pltpu.VMEM((tk // tq, tq, LANES), BF16),
            pltpu.VMEM((tq, tq), F32),
            pltpu.VMEM((2, tq, tq), F32),
            pltpu.VMEM((2, 1, tq), F32),
        ],
    )
    return pl.pallas_call(
        functools.partial(_diff_attn_nocache_body, tq=tq, tk=tk, lam_init=lam_init),
        grid_spec=grid_spec,
        out_shape=jax.ShapeDtypeStruct((B * T, width), F32),
        compiler_params=pltpu.CompilerParams(
            dimension_semantics=("arbitrary", "arbitrary"), vmem_limit_bytes=VMEM_LIMIT),
        name="diff_attn_nocache",
    )(qi_t, nj_t, fl_t, q, k_heads, vt_heads, z, lam_v, subln_g.reshape(1, DA_V_DIM))


def _diff_attn_cache_body(q_ref, pk_ref, pv_ref, k_ref, v_ref, z_ref, lam_ref, sg_ref, o_ref,
                          qa_scr, acc_scr, m_scr, l_scr, s_scr, mx_scr, *, P, T, tkp, lam_init):
    t = pl.program_id(1)
    n_past = P // tkp

    @pl.when(t == 0)
    def _():
        lane = lax.broadcasted_iota(jnp.int32, (T, LANES), 1)
        half = lane < DA_HEAD_DIM
        cblk = jnp.zeros((2 * T, LANES), F32)
        lane2 = lax.broadcasted_iota(jnp.int32, (2 * T, LANES), 1)
        for i, part in enumerate(_LOG2E_BF16_PARTS):
            cblk = jnp.where((lane2 == i) | (lane2 == i + _N_PARTS), part, cblk)
        for h in range(DA_HEADS):
            sl = slice(h * DA_V_DIM, (h + 1) * DA_V_DIM)
            slope = 2.0 ** (-8.0 * (h + 1) / DA_HEADS)
            qh = q_ref[:, sl] * (DA_HEAD_DIM ** -0.5 * _LOG2E)
            both = jnp.concatenate([jnp.where(half, qh, 0.0), jnp.where(half, 0.0, qh)], axis=0)
            qa_scr[h, :, :LANES] = both.astype(BF16)
            qa_scr[h, :, LANES:] = (cblk * slope).astype(BF16)
        acc_scr[...] = jnp.zeros_like(acc_scr)
        l_scr[...] = jnp.zeros_like(l_scr)
        m_scr[...] = jnp.full_like(m_scr, NEG_BIG)

    def pos_block(tk, rel_start):
        rel0 = lax.broadcasted_iota(jnp.int32, (tk, LANES), 0) + rel_start
        lane_k = lax.broadcasted_iota(jnp.int32, (tk, LANES), 1)
        hi = ((rel0 >> 7) << 7).astype(F32)
        lo = (rel0 & 127).astype(F32)
        return jnp.where(lane_k < _N_PARTS, hi, jnp.where(lane_k < 2 * _N_PARTS, lo, 0.0)).astype(BF16)

    def head_update(h, kb, vb, pos_blk, adj):
        ka = jnp.concatenate([kb, pos_blk], axis=1)
        s = lax.dot_general(ka, qa_scr[h], _NT, preferred_element_type=F32)
        if adj is not None:
            s = s + adj * (2.0 ** (-8.0 * (h + 1) / DA_HEADS))
        m_old = m_scr[h]
        m_new = jnp.maximum(m_old, jnp.max(s, axis=0, keepdims=True))
        p = jnp.exp2(s - m_new)
        alpha = jnp.exp2(m_old - m_new)
        l_scr[h] = alpha * l_scr[h] + jnp.sum(p, axis=0, keepdims=True)
        acc_scr[h] = alpha * acc_scr[h] + lax.dot_general(
            vb, p.astype(BF16), _TN, preferred_element_type=F32)
        m_scr[h] = m_new

    @pl.when(t < n_past)
    def _():
        pos_blk = pos_block(tkp, t * tkp - P)
        for h in range(DA_HEADS):
            rows = pl.ds(h, tkp, stride=DA_HEADS)
            ka = jnp.concatenate([pk_ref[rows, :].astype(BF16), pos_blk], axis=1)
            s = lax.dot_general(ka, qa_scr[h], _NT, preferred_element_type=F32)
            s_scr[h] = s
            mx_scr[h] = jnp.max(s, axis=0, keepdims=True)
        for h in range(DA_HEADS):
            rows = pl.ds(h, tkp, stride=DA_HEADS)
            m_old = m_scr[h]
            m_new = jnp.maximum(m_old, mx_scr[h])
            p = jnp.exp2(s_scr[h] - m_new)
            alpha = jnp.exp2(m_old - m_new)
            l_scr[h] = alpha * l_scr[h] + jnp.sum(p, axis=0, keepdims=True)
            acc_scr[h] = alpha * acc_scr[h] + lax.dot_general(
                pv_ref[rows, :].astype(BF16), p.astype(BF16), _TN, preferred_element_type=F32)
            m_scr[h] = m_new

    @pl.when(t == n_past)
    def _():
        pos_blk = pos_block(T, 0)
        kidx = lax.broadcasted_iota(jnp.int32, (T, 2 * T), 0)
        qidx = lax.broadcasted_iota(jnp.int32, (T, 2 * T), 1) & (T - 1)
        rel = kidx - qidx
        visible = ((kidx + P) >> 6) <= ((qidx + P) >> 6)
        adj = jnp.where(visible, jnp.maximum(rel, 0).astype(F32) * (-2.0 * _LOG2E), NEG_BIG)
        for h in range(DA_HEADS):
            rows = pl.ds(h, T, stride=DA_HEADS)
            head_update(h, k_ref[rows, :].astype(BF16), v_ref[rows, :].astype(BF16), pos_blk, adj)

        lv = lam_ref[...]
        lam = (jnp.exp(jnp.sum(lv[0:1] * lv[1:2], axis=-1, keepdims=True))
               - jnp.exp(jnp.sum(lv[2:3] * lv[3:4], axis=-1, keepdims=True)) + lam_init)
        for h in range(DA_HEADS):
            sl = slice(h * DA_V_DIM, (h + 1) * DA_V_DIM)
            a = (acc_scr[h] * (1.0 / l_scr[h])).T
            o = a[:T] - lam * a[T:]
            o = o * lax.rsqrt(jnp.mean(o * o, axis=-1, keepdims=True) + EPS) * sg_ref[...] * (1.0 - lam_init)
            zh = z_ref[:, sl]
            o_ref[:, sl] = o * (zh * _sigmoid(zh))


def _diff_attn_cache(q, z, k_new, v_new, past_k, past_v, lam_v, subln_g, B, T, lam_init, tkp=1024):
    P = past_k.shape[1]
    assert 2 * T == LANES and T == CHUNK and P % tkp == 0 and P % CHUNK == 0
    n_past = P // tkp
    width = DA_HEADS * DA_V_DIM
    tok = lambda b, t: (b, 0)
    past = lambda b, t: (b, jnp.minimum(t, n_past - 1), 0)
    const = lambda b, t: (0, 0)
    pk = past_k.reshape(B, P * DA_HEADS, 2 * DA_HEAD_DIM)
    pv = past_v.reshape(B, P * DA_HEADS, DA_V_DIM)
    return pl.pallas_call(
        functools.partial(_diff_attn_cache_body, P=P, T=T, tkp=tkp, lam_init=lam_init),
        grid=(B, n_past + 1),
        in_specs=[
            pl.BlockSpec((T, width), tok),
            pl.BlockSpec((None, tkp * DA_HEADS, LANES), past),
            pl.BlockSpec((None, tkp * DA_HEADS, LANES), past),
            pl.BlockSpec((T * DA_HEADS, LANES), tok),
            pl.BlockSpec((T * DA_HEADS, LANES), tok),
            pl.BlockSpec((T, width), tok),
            pl.BlockSpec((4, DA_HEAD_DIM), const),
            pl.BlockSpec((1, DA_V_DIM), const),
        ],
        out_specs=pl.BlockSpec((T, width), tok),
        out_shape=jax.ShapeDtypeStruct((B * T, width), F32),
        scratch_shapes=[
            pltpu.VMEM((DA_HEADS, 2 * T, 2 * LANES), BF16),
            pltpu.VMEM((DA_HEADS, DA_V_DIM, 2 * T), F32),
            pltpu.VMEM((DA_HEADS, 1, 2 * T), F32),
            pltpu.VMEM((DA_HEADS, 1, 2 * T), F32),
            pltpu.VMEM((DA_HEADS, tkp, 2 * T), F32),
            pltpu.VMEM((DA_HEADS, 1, 2 * T), F32),
        ],
        compiler_params=pltpu.CompilerParams(
            dimension_semantics=("arbitrary", "arbitrary"), vmem_limit_bytes=VMEM_LIMIT),
        name="diff_attn_cache",
    )(q, pk, pv, k_new, v_new, z, lam_v, subln_g.reshape(1, DA_V_DIM))


def _trunk(x, c0, n0, m0, past_k, past_v, mem_k, mem_v, wts):
    B, T, _ = x.shape
    P = 0 if past_k is None else past_k.shape[1]
    n_tok = B * T
    x2 = x.reshape(n_tok, D_MODEL)
    tm = ROW_TILE

    q, k, v, o, z, mq, mz, gates = _norm_matmul(
        x2, wts["norm_g"][0], wts["w_a"], [ML_WIDTH] * 5 + [MEM_WIDTH] * 2, gates_w=wts["w_a_gates"],
        tm=tm, name="in_proj_a")
    L = min(T, 256)
    hm, c_new, n_new, m_new = _mlstm(
        q, k, v, o, z, gates, wts["b_gate"], wts["head_g"],
        c0, n0.reshape(B, ML_HEADS, 1, ML_HEAD_DIM),
        jnp.broadcast_to(m0.reshape(B, ML_HEADS, 1, 1), (B, ML_HEADS, 1, LANES)), B, T, L)
    x1 = _epilogue(x2, hm, mq, mz, mem_k[0][0], mem_v[0][0], wts["w_out_a1"], wts["w_out_a2"], B, T,
                   mem_layer=mem_k[0][1], name="epilogue_a")

    kv_splits = [DA_HEADS * 2 * DA_HEAD_DIM, DA_HEADS * DA_V_DIM]
    qd, zd, mq2, mz2 = _norm_matmul(x1, wts["norm_g"][1], wts["w_b"],
                                    [DA_HEADS * 2 * DA_HEAD_DIM, DA_HEADS * DA_V_DIM, MEM_WIDTH, MEM_WIDTH],
                                    tm=tm, name="in_proj_b")
    lam_init = 0.8 - 0.6 * math.exp(-0.3 * 1)
    if past_k is None:
        k_new, v_new, k_heads, vt_heads = _norm_matmul(x1, wts["kv_norm_g"], wts["w_kv"], kv_splits,
                                                       head_major=((0, False), (1, True)),
                                                       interleave=(0, 1), tm=tm, name="kv_proj")
        od = _diff_attn_nocache(qd, zd, k_heads, vt_heads, wts["lam_b"], wts["subln_g"], B, T, lam_init)
    else:
        k_new, v_new = _norm_matmul(x1, wts["kv_norm_g"], wts["w_kv"], kv_splits, interleave=(0, 1),
                                    tm=tm, name="kv_proj")
        od = _diff_attn_cache(qd, zd, k_new, v_new, past_k, past_v, wts["lam_b"], wts["subln_g"], B, T,
                              lam_init)
    y = _epilogue(x1, od, mq2, mz2, mem_k[1][0], mem_v[1][0], wts["w_out_b1"], wts["w_out_b2"], B, T,
                  final_g=wts["final_norm_g"], mem_layer=mem_k[1][1], name="epilogue_b")

    return (y.reshape(B, T, D_MODEL),
            c_new.reshape(1, B, ML_HEADS, ML_HEAD_DIM, ML_HEAD_DIM),
            n_new.reshape(1, B, ML_HEADS, ML_HEAD_DIM),
            m_new[..., 0, 0].reshape(1, B, ML_HEADS),
            k_new.reshape(B, T, DA_HEADS, 2 * DA_HEAD_DIM),
            v_new.reshape(B, T, DA_HEADS, DA_V_DIM))


def kernel(x_prompt, x_sample, cache_k, cache_v, cache_mem_k, cache_mem_v, state_C, state_n, state_m, mem_prompt, norm_g, final_norm_g, mem_norm_g, w_mem_kv, w_in_a, b_gate_a, head_g_a, w_out_a, kv_norm_g, w_kv, w_in_b, lam_b, subln_g_b, w_out_b):
    B = x_prompt.shape[0]
    DB = x_sample.shape[0]
    n_gate = 2 * ML_HEADS
    g0 = 5 * ML_WIDTH
    w_a = w_in_a[0]
    w_gates = jnp.pad(w_a[:, g0:g0 + n_gate], ((0, 0), (0, LANES - n_gate)))
    w_gates_hi = w_gates.astype(BF16)
    w_gates_lo = (w_gates - w_gates_hi.astype(F32)).astype(BF16)
    wts = {
        "norm_g": norm_g,
        "final_norm_g": final_norm_g,
        "kv_norm_g": kv_norm_g,
        "w_a": jnp.concatenate([w_a[:, :g0], w_a[:, g0 + n_gate:]], axis=1).astype(BF16),
        "w_a_gates": jnp.concatenate([w_gates_hi, w_gates_lo], axis=1),
        "b_gate": jnp.pad(b_gate_a[0], (0, LANES - n_gate)).reshape(1, LANES),
        "head_g": head_g_a[0].reshape(1, ML_WIDTH),
        "w_out_a1": w_out_a[0, :ML_WIDTH].astype(BF16),
        "w_out_a2": w_out_a[0, ML_WIDTH:].astype(BF16),
        "w_kv": w_kv.astype(BF16),
        "w_b": w_in_b[0].astype(BF16),
        "lam_b": lam_b[0],
        "subln_g": subln_g_b[0],
        "w_out_b1": w_out_b[0, :DA_HEADS * DA_V_DIM].astype(BF16),
        "w_out_b2": w_out_b[0, DA_HEADS * DA_V_DIM:].astype(BF16),
    }

    mem2 = mem_prompt.reshape(B * MEM_LEN, D_MODEL)
    mks, mvs = [], []
    for l in range(2):
        mk, mv = _norm_matmul(mem2, mem_norm_g[l], w_mem_kv[l].astype(BF16), [MEM_WIDTH, MEM_WIDTH],
                              interleave=(0, 1), tm=256, name="mem_kv")
        mks.append(mk.reshape(B, MEM_LEN * MEM_HEADS, MEM_HEAD_DIM))
        mvs.append(mv.reshape(B, MEM_LEN * MEM_HEADS, MEM_HEAD_DIM))
    prompt_mem_k = jnp.stack(mks).reshape(2, B, MEM_LEN, MEM_HEADS, MEM_HEAD_DIM)
    prompt_mem_v = jnp.stack(mvs).reshape(2, B, MEM_LEN, MEM_HEADS, MEM_HEAD_DIM)

    zc = jnp.zeros((B, ML_HEADS, ML_HEAD_DIM, ML_HEAD_DIM), F32)
    zn = jnp.zeros((B, ML_HEADS, ML_HEAD_DIM), F32)
    zm = jnp.zeros((B, ML_HEADS), F32)
    y_prompt, prompt_C, prompt_n, prompt_m, prompt_k, prompt_v = _trunk(
        x_prompt, zc, zn, zm, None, None, [(m, 0) for m in mks], [(m, 0) for m in mvs], wts)

    cmk = cache_mem_k.reshape(2 * DB, MEM_LEN * MEM_HEADS, MEM_HEAD_DIM)
    cmv = cache_mem_v.reshape(2 * DB, MEM_LEN * MEM_HEADS, MEM_HEAD_DIM)
    smk = [(cmk, l) for l in range(2)]
    smv = [(cmv, l) for l in range(2)]
    y_sample, sample_C, sample_n, sample_m, sample_k, sample_v = _trunk(
        x_sample, state_C[0], state_n[0], state_m[0], cache_k, cache_v, smk, smv, wts)

    return (y_prompt, y_sample, prompt_C, prompt_n, prompt_m, prompt_k, prompt_v, prompt_mem_k, prompt_mem_v,
            sample_C, sample_n, sample_m, sample_k, sample_v)
```

```python
import functools
import math

import numpy as np
import jax
import jax.numpy as jnp
from jax import lax
from jax.experimental import pallas as pl
from jax.experimental.pallas import tpu as pltpu

F32 = jnp.float32
BF16 = jnp.bfloat16

D_MODEL = 1024
CHUNK = 64
ML_HEADS = 4
ML_HEAD_DIM = 256
ML_WIDTH = 1024
DA_HEADS = 8
DA_HEAD_DIM = 64
DA_V_DIM = 128
DA_WIDTH = DA_HEADS * DA_V_DIM
MEM_LEN = 256
MEM_HEADS = 4
MEM_HEAD_DIM = 128
MEM_WIDTH = 512
EPS = 1e-6
NEG_BIG = -1e30

LANES = 128
BF16_SUBLANES = 16
VMEM_LIMIT = 56 * 1024 * 1024
ROW_TILE = 512
MLSTM_CHUNK = 256
ATTN_TQ = 512
ATTN_TK = 1024
CACHE_TK = 1024
ONES_ROWS = BF16_SUBLANES

_NT = (((1,), (1,)), ((), ()))
_TN = (((0,), (0,)), ((), ()))

_LOG2E = 1.4426950216293335
_LOG2E_BF16_PARTS = (1.4453125, -0.00262451171875, 7.033348083496094e-06)
_N_PARTS = len(_LOG2E_BF16_PARTS)


def _sigmoid(x):
    return 1.0 / (1.0 + jnp.exp(-x))


def _alibi_slope(h):
    return 2.0 ** (-8.0 * (h + 1) / DA_HEADS)


def _norm_matmul_body(*refs, n_weights, splits, with_gates, head_major, interleave):
    x_ref, g_ref = refs[:2]
    w_refs = refs[2:2 + n_weights]
    rest = refs[2 + n_weights:]
    if with_gates:
        wg_ref, rest = rest[0], rest[1:]
        gate_out, rest = rest[-1], rest[:-1]
    n_out = sum(len(s) for s in splits)
    outs, hm_outs = rest[:n_out], rest[n_out:]
    x = x_ref[...]
    xn = x * lax.rsqrt(jnp.mean(x * x, axis=-1, keepdims=True) + EPS) * g_ref[...]
    xb = xn.astype(BF16)
    i = 0
    for w_ref, widths in zip(w_refs, splits):
        off = 0
        for width in widths:
            o_ref = outs[i]
            r = jnp.dot(xb, w_ref[:, off:off + width], preferred_element_type=F32)
            if i in interleave:
                nh = width // LANES
                for h in range(nh):
                    o_ref[pl.ds(h, x.shape[0], stride=nh), :] = r[:, h * LANES:(h + 1) * LANES]
            else:
                o_ref[...] = r
            for (split, transposed), hb_ref in zip(head_major, hm_outs):
                if split == i:
                    for h in range(width // LANES):
                        rh = r[:, h * LANES:(h + 1) * LANES]
                        if transposed:
                            hb_ref[h, :LANES, :] = rh.T.astype(BF16)
                            extra = lax.broadcasted_iota(jnp.int32, (ONES_ROWS, rh.shape[0]), 0) == 0
                            hb_ref[h, LANES:, :] = jnp.where(extra, 1.0, 0.0).astype(BF16)
                        else:
                            hb_ref[h] = rh.astype(BF16)
            off += width
            i += 1
    if with_gates:
        x_lo = (xn - xb.astype(F32)).astype(BF16)
        g_hi = jnp.dot(xb, wg_ref[...], preferred_element_type=F32)
        g_lo = jnp.dot(x_lo, wg_ref[:, :LANES], preferred_element_type=F32)
        gate_out[...] = g_hi[:, :LANES] + g_hi[:, LANES:] + g_lo


def _norm_matmul(x, g, weights, splits, gates_w=None, head_major=(), interleave=(), name="norm_matmul"):
    n, d = x.shape
    tm = min(ROW_TILE, n)
    assert n % tm == 0 and all(sum(s) == w.shape[1] for s, w in zip(splits, weights))
    with_gates = gates_w is not None
    head_major = tuple(head_major)
    interleave = tuple(interleave)
    flat = [width for s in splits for width in s]
    row = lambda i: (i, 0)
    const = lambda i: (0, 0)
    in_specs = [pl.BlockSpec((tm, d), row), pl.BlockSpec((1, d), const)]
    in_specs += [pl.BlockSpec(w.shape, const, pipeline_mode=pl.Buffered(1)) for w in weights]
    args = [x, g.reshape(1, d), *weights]
    out_shape, out_specs = [], []
    for i, width in enumerate(flat):
        rows, cols = (width // LANES, LANES) if i in interleave else (1, width)
        out_shape.append(jax.ShapeDtypeStruct((n * rows, cols), F32))
        out_specs.append(pl.BlockSpec((tm * rows, cols), row))
    for split, transposed in head_major:
        nh = flat[split] // LANES
        if transposed:
            out_shape.append(jax.ShapeDtypeStruct((nh, LANES + ONES_ROWS, n), BF16))
            out_specs.append(pl.BlockSpec((nh, LANES + ONES_ROWS, tm), lambda i: (0, 0, i)))
        else:
            out_shape.append(jax.ShapeDtypeStruct((nh, n, LANES), BF16))
            out_specs.append(pl.BlockSpec((nh, tm, LANES), lambda i: (0, i, 0)))
    if with_gates:
        in_specs.append(pl.BlockSpec((d, 2 * LANES), const))
        args.append(gates_w)
        out_shape.append(jax.ShapeDtypeStruct((n, LANES), F32))
        out_specs.append(pl.BlockSpec((tm, LANES), row))
    return pl.pallas_call(
        functools.partial(_norm_matmul_body, n_weights=len(weights),
                          splits=tuple(tuple(s) for s in splits), with_gates=with_gates,
                          head_major=head_major, interleave=interleave),
        grid=(n // tm,),
        in_specs=in_specs,
        out_specs=out_specs,
        out_shape=out_shape,
        compiler_params=pltpu.CompilerParams(
            dimension_semantics=("arbitrary",), vmem_limit_bytes=VMEM_LIMIT),
        name=name,
    )(*args)


def _mlstm_body(q_ref, k_ref, v_ref, o_ref, z_ref, gt_ref, bg_ref, hg_ref, c0_ref, n0_ref, m0_ref,
                h_out, c_out, n_out, m_out, c_s, n_s, m_s, *, L, nc):
    c = pl.program_id(1)

    @pl.when(c == 0)
    def _():
        c_s[...] = c0_ref[0]
        n_s[...] = n0_ref[0]
        m_s[...] = m0_ref[0]

    gc = gt_ref[...] + bg_ref[...]
    lane = lax.broadcasted_iota(jnp.int32, (L, LANES), 1)
    lf = jnp.minimum(gc, 0.0) - jnp.log1p(jnp.exp(-jnp.abs(gc)))
    row = lax.broadcasted_iota(jnp.int32, (L, L), 0)
    col = lax.broadcasted_iota(jnp.int32, (L, L), 1)
    tril = col <= row
    lf_hi = lf.astype(BF16)
    rem = lf - lf_hi.astype(F32)
    lf_mid = rem.astype(BF16)
    lf_lo = (rem - lf_mid.astype(F32)).astype(BF16)
    parts = jnp.dot(jnp.where(tril, 1.0, 0.0).astype(BF16), jnp.concatenate([lf_hi, lf_mid, lf_lo], axis=1),
                    preferred_element_type=F32)
    gcum = parts[:, :LANES] + parts[:, LANES:2 * LANES] + parts[:, 2 * LANES:]
    comb = jnp.where(lane < ML_HEADS, gc, gcum)
    rows = comb.T

    stash = []
    for h in range(ML_HEADS):
        sl = slice(h * ML_HEAD_DIM, (h + 1) * ML_HEAD_DIM)
        src_r = rows[h:h + 1, :] - rows[ML_HEADS + h:ML_HEADS + h + 1, :]
        g_c = comb[:, ML_HEADS + h:ML_HEADS + h + 1]
        m_prev = m_s[h][:, :1]

        dmat = jnp.where(tril, g_c + src_r, NEG_BIG)
        inter = g_c + m_prev
        m_t = jnp.maximum(inter, jnp.max(dmat, axis=-1, keepdims=True))
        w_intra = jnp.exp(dmat - m_t)
        w_inter = jnp.exp(inter - m_t)

        qb = q_ref[:, sl].astype(BF16)
        kb = (k_ref[:, sl] * (ML_HEAD_DIM ** -0.5)).astype(BF16)
        qk = lax.dot_general(qb, kb, _NT, preferred_element_type=F32)
        s = w_intra * qk
        stash.append((m_t, w_inter, s.astype(BF16), jnp.sum(s, axis=-1, keepdims=True)))

    for h in range(ML_HEADS):
        sl = slice(h * ML_HEAD_DIM, (h + 1) * ML_HEAD_DIM)
        m_t, w_inter, sb, s_sum = stash[h]
        ig_c = comb[:, h:h + 1]
        g_c = comb[:, ML_HEADS + h:ML_HEADS + h + 1]
        m_prev = m_s[h][:, :1]
        qh = q_ref[:, sl]
        kh = k_ref[:, sl] * (ML_HEAD_DIM ** -0.5)
        vh = v_ref[:, sl]
        qb = qh.astype(BF16)
        kb = kh.astype(BF16)
        vb = vh.astype(BF16)
        ch = c_s[h]
        nh = n_s[h]
        cq = lax.dot_general(qb, ch.astype(BF16), _NT, preferred_element_type=F32)
        num = w_inter * cq + jnp.dot(sb, vb, preferred_element_type=F32)
        nq = jnp.sum(qh * nh, axis=-1, keepdims=True)
        den = w_inter * nq + s_sum
        hh = num * (1.0 / jnp.maximum(jnp.abs(den), jnp.exp(-m_t)))

        g_last = g_c[L - 1:L, :]
        m_new = m_t[L - 1:L, :]
        w_s = jnp.exp(g_last - g_c + ig_c - m_new)
        dec = jnp.exp(g_last + m_prev - m_new)
        vw = (vh * w_s).astype(BF16)
        c_s[h] = dec * ch + lax.dot_general(vw, kb, _TN, preferred_element_type=F32)
        n_s[h] = dec * nh + jnp.sum(kh * w_s, axis=0, keepdims=True)
        m_s[h] = jnp.broadcast_to(m_new, (1, LANES))

        oh = o_ref[:, sl]
        zh = z_ref[:, sl]
        hm = _sigmoid(oh) * hh
        hm = hm * lax.rsqrt(jnp.mean(hm * hm, axis=-1, keepdims=True) + EPS) * hg_ref[:, sl]
        h_out[:, sl] = hm * (zh * _sigmoid(zh))

    @pl.when(c == nc - 1)
    def _():
        c_out[0] = c_s[...]
        n_out[0] = n_s[...]
        m_out[0] = m_s[...]


def _mlstm(q, k, v, o, z, gates, b_gate, head_g, c0, n0, m0, B, T):
    L = min(T, MLSTM_CHUNK)
    assert T % L == 0
    nc = T // L
    dh = ML_HEAD_DIM
    tok = lambda b, c: (b * nc + c, 0)
    st4 = lambda b, c: (b, 0, 0, 0)
    return pl.pallas_call(
        functools.partial(_mlstm_body, L=L, nc=nc),
        grid=(B, nc),
        in_specs=[pl.BlockSpec((L, ML_WIDTH), tok)] * 5 + [
            pl.BlockSpec((L, LANES), tok),
            pl.BlockSpec((1, LANES), lambda b, c: (0, 0)),
            pl.BlockSpec((1, ML_WIDTH), lambda b, c: (0, 0)),
            pl.BlockSpec((1, ML_HEADS, dh, dh), st4),
            pl.BlockSpec((1, ML_HEADS, 1, dh), st4),
            pl.BlockSpec((1, ML_HEADS, 1, LANES), st4),
        ],
        out_specs=[
            pl.BlockSpec((L, ML_WIDTH), tok),
            pl.BlockSpec((1, ML_HEADS, dh, dh), st4),
            pl.BlockSpec((1, ML_HEADS, 1, dh), st4),
            pl.BlockSpec((1, ML_HEADS, 1, LANES), st4),
        ],
        out_shape=[
            jax.ShapeDtypeStruct((B * T, ML_WIDTH), F32),
            jax.ShapeDtypeStruct((B, ML_HEADS, dh, dh), F32),
            jax.ShapeDtypeStruct((B, ML_HEADS, 1, dh), F32),
            jax.ShapeDtypeStruct((B, ML_HEADS, 1, LANES), F32),
        ],
        scratch_shapes=[
            pltpu.VMEM((ML_HEADS, dh, dh), F32),
            pltpu.VMEM((ML_HEADS, 1, dh), F32),
            pltpu.VMEM((ML_HEADS, 1, LANES), F32),
        ],
        compiler_params=pltpu.CompilerParams(
            dimension_semantics=("arbitrary", "arbitrary"), vmem_limit_bytes=VMEM_LIMIT),
        name="mlstm",
    )(q, k, v, o, z, gates, b_gate, head_g, c0, n0, m0)


def _epilogue_body(*refs, final_norm, nb, rows_per_batch):
    if final_norm:
        x_ref, a_ref, mq_ref, mz_ref, mk_ref, mv_ref, w1_ref, w2_ref, fg_ref, y_ref = refs
    else:
        x_ref, a_ref, mq_ref, mz_ref, mk_ref, mv_ref, w1_ref, w2_ref, y_ref = refs
    acc = x_ref[...] + jnp.dot(a_ref[...].astype(BF16), w1_ref[...], preferred_element_type=F32)

    def tok(bi):
        return slice(bi * rows_per_batch, (bi + 1) * rows_per_batch)

    def head_rows(h):
        return pl.ds(h, MEM_LEN, stride=MEM_HEADS)

    scores = {}
    for bi in range(nb):
        for h in range(MEM_HEADS):
            sl = slice(h * MEM_HEAD_DIM, (h + 1) * MEM_HEAD_DIM)
            qh = mq_ref[tok(bi), sl].astype(BF16)
            kh = mk_ref.at[bi][head_rows(h), :].astype(BF16)
            scores[bi, h] = lax.dot_general(qh, kh, _NT, preferred_element_type=F32) * (MEM_HEAD_DIM ** -0.5)
    mo_rows = []
    for bi in range(nb):
        mos = []
        for h in range(MEM_HEADS):
            sl = slice(h * MEM_HEAD_DIM, (h + 1) * MEM_HEAD_DIM)
            s = scores[bi, h]
            e = jnp.exp(s - jnp.max(s, axis=-1, keepdims=True))
            p = e * (1.0 / jnp.sum(e, axis=-1, keepdims=True))
            vh = mv_ref.at[bi][head_rows(h), :].astype(BF16)
            oh = jnp.dot(p.astype(BF16), vh, preferred_element_type=F32)
            zh = mz_ref[tok(bi), sl]
            mos.append((oh * (zh * _sigmoid(zh))).astype(BF16))
        mo_rows.append(jnp.concatenate(mos, axis=-1))
    mo = mo_rows[0] if nb == 1 else jnp.concatenate(mo_rows, axis=0)
    acc = acc + jnp.dot(mo, w2_ref[...], preferred_element_type=F32)
    if final_norm:
        acc = acc * lax.rsqrt(jnp.mean(acc * acc, axis=-1, keepdims=True) + EPS) * fg_ref[...]
    y_ref[...] = acc


def _epilogue(x, a, mq, mz, mem_k, mem_v, w1, w2, B, T, final_g=None, mem_layer=0, name="epilogue"):
    if T >= ROW_TILE:
        nb, rows_per_batch, tm = 1, ROW_TILE, ROW_TILE
    else:
        nb, rows_per_batch, tm = min(B, ROW_TILE // T), T, min(B, ROW_TILE // T) * T
    assert T % rows_per_batch == 0 and B % nb == 0
    nt = T // rows_per_batch
    tok = lambda b, i: (b * nt + i, 0)
    const = lambda b, i: (0, 0)
    mem_map = lambda b, i: (mem_layer * (B // nb) + b, 0, 0)
    final_norm = final_g is not None
    in_specs = [
        pl.BlockSpec((tm, D_MODEL), tok),
        pl.BlockSpec((tm, a.shape[1]), tok),
        pl.BlockSpec((tm, MEM_WIDTH), tok),
        pl.BlockSpec((tm, MEM_WIDTH), tok),
        pl.BlockSpec((nb, MEM_LEN * MEM_HEADS, MEM_HEAD_DIM), mem_map),
        pl.BlockSpec((nb, MEM_LEN * MEM_HEADS, MEM_HEAD_DIM), mem_map),
        pl.BlockSpec(w1.shape, const, pipeline_mode=pl.Buffered(1)),
        pl.BlockSpec(w2.shape, const, pipeline_mode=pl.Buffered(1)),
    ]
    args = [x, a, mq, mz, mem_k, mem_v, w1, w2]
    if final_norm:
        in_specs.append(pl.BlockSpec((1, D_MODEL), const))
        args.append(final_g.reshape(1, D_MODEL))
    return pl.pallas_call(
        functools.partial(_epilogue_body, final_norm=final_norm, nb=nb, rows_per_batch=rows_per_batch),
        grid=(B // nb, nt),
        in_specs=in_specs,
        out_specs=pl.BlockSpec((tm, D_MODEL), tok),
        out_shape=jax.ShapeDtypeStruct((B * T, D_MODEL), F32),
        compiler_params=pltpu.CompilerParams(
            dimension_semantics=("arbitrary", "arbitrary"), vmem_limit_bytes=VMEM_LIMIT),
        name=name,
    )(*args)


_FLAG_FIRST, _FLAG_LAST = 1, 2
_KIND_SHIFT = 2


def _position_columns(n_keys, rel_start):
    rel0 = lax.broadcasted_iota(jnp.int32, (n_keys, LANES), 0) + rel_start
    lane_k = lax.broadcasted_iota(jnp.int32, (n_keys, LANES), 1)
    hi = ((rel0 >> 7) << 7).astype(F32)
    lo = (rel0 & 127).astype(F32)
    return jnp.where(lane_k < _N_PARTS, hi, jnp.where(lane_k < 2 * _N_PARTS, lo, 0.0)).astype(BF16)


def _slope_rows(n_queries):
    row = lax.broadcasted_iota(jnp.int32, (LANES, n_queries), 0)
    cblk = jnp.zeros((LANES, n_queries), F32)
    for i, part in enumerate(_LOG2E_BF16_PARTS):
        cblk = jnp.where((row == i) | (row == i + _N_PARTS), part, cblk)
    return cblk


def _diag_adjust(kidx, qidx, k_abs, q_abs):
    visible = ((kidx + k_abs) >> 6) <= ((qidx + q_abs) >> 6)
    rel = kidx + k_abs - qidx - q_abs
    return jnp.where(visible, jnp.maximum(rel, 0).astype(F32) * (-2.0 * _LOG2E), NEG_BIG)


def _lambda(lam_ref, lam_init):
    lv = lam_ref[...]
    return (jnp.exp(jnp.sum(lv[0:1] * lv[1:2], axis=-1, keepdims=True))
            - jnp.exp(jnp.sum(lv[2:3] * lv[3:4], axis=-1, keepdims=True)) + lam_init)


def _subln_gate(o, sg_ref, zh, lam_init):
    o = o * lax.rsqrt(jnp.mean(o * o, axis=-1, keepdims=True) + EPS) * sg_ref[...] * (1.0 - lam_init)
    return o * (zh * _sigmoid(zh))


def _diff_attn_nocache_body(qi_ref, nj_ref, fl_ref, q_ref, k_ref, vt_ref, z_ref, lam_ref, sg_ref, o_ref,
                            qa_scr, acc_scr, m_scr, pos_scr, adj_scr, s_scr, mx_scr,
                            *, tq, tk, lam_init):
    t = pl.program_id(1)
    flags = fl_ref[t]
    q_start = qi_ref[t] * tq
    k_start = nj_ref[t] * tk

    @pl.when((flags & _FLAG_FIRST) != 0)
    def _():
        half = lax.broadcasted_iota(jnp.int32, (LANES, tq), 0) < DA_HEAD_DIM
        cblk = _slope_rows(tq)
        for h in range(DA_HEADS):
            sl = slice(h * DA_V_DIM, (h + 1) * DA_V_DIM)
            qh = (q_ref[:, sl] * (DA_HEAD_DIM ** -0.5 * _LOG2E)).T
            cb = (cblk * _alibi_slope(h)).astype(BF16)
            for c in range(2):
                keep = half if c == 0 else jnp.logical_not(half)
                qa_scr[2 * h + c, :LANES, :] = jnp.where(keep, qh, 0.0).astype(BF16)
                qa_scr[2 * h + c, LANES:, :] = cb
        acc_scr[...] = jnp.zeros_like(acc_scr)
        m_scr[...] = jnp.full_like(m_scr, NEG_BIG)

    n_sub = tk // tq
    n_maps = 2 * DA_HEADS

    def attend(modes):
        live = [sb for sb in range(n_sub) if modes[sb] is not None]
        for sb in live:
            pos_scr[sb] = _position_columns(tq, k_start + sb * tq - q_start)
            if modes[sb] == 'diag':
                kidx = lax.broadcasted_iota(jnp.int32, (tq, tq), 0)
                qidx = lax.broadcasted_iota(jnp.int32, (tq, tq), 1)
                adj_scr[...] = _diag_adjust(kidx, qidx, q_start, q_start)

        tasks = [(sb, idx) for sb in live for idx in range(n_maps)]

        def scores(n):
            sb, idx = tasks[n]
            h = idx // 2
            ka = jnp.concatenate([k_ref[h, sb * tq:(sb + 1) * tq, :], pos_scr[sb]], axis=1)
            s = jnp.dot(ka, qa_scr[idx], preferred_element_type=F32)
            if modes[sb] == 'diag':
                s = s + adj_scr[...] * _alibi_slope(h)
            s_scr[n % 2] = s
            mx_scr[n % 2] = jnp.max(s, axis=0, keepdims=True)

        def softmax(n):
            sb, idx = tasks[n]
            s = s_scr[n % 2]
            m_old = m_scr[idx]
            m_new = jnp.maximum(m_old, mx_scr[n % 2])
            p = jnp.exp2(s - m_new).astype(BF16)
            alpha = jnp.exp2(m_old - m_new)
            m_scr[idx] = m_new
            acc_scr[idx] = alpha * acc_scr[idx] + jnp.dot(
                vt_ref[idx // 2, :, sb * tq:(sb + 1) * tq], p, preferred_element_type=F32)

        scores(0)
        for n in range(len(tasks)):
            if n + 1 < len(tasks):
                scores(n + 1)
            softmax(n)

    kind = flags >> _KIND_SHIFT
    for d in range(n_sub + 1):
        modes = ['past'] * n_sub if d == n_sub else ['past'] * d + ['diag'] + [None] * (n_sub - d - 1)
        pl.when(kind == d)(functools.partial(attend, modes))

    @pl.when((flags & _FLAG_LAST) != 0)
    def _():
        lam = _lambda(lam_ref, lam_init)
        for h in range(DA_HEADS):
            sl = slice(h * DA_V_DIM, (h + 1) * DA_V_DIM)
            a1 = acc_scr[2 * h]
            a2 = acc_scr[2 * h + 1]
            ot = (a1[:DA_V_DIM] * (1.0 / a1[DA_V_DIM:DA_V_DIM + 1])
                  - lam * (a2[:DA_V_DIM] * (1.0 / a2[DA_V_DIM:DA_V_DIM + 1])))
            o_ref[:, sl] = _subln_gate(ot.T, sg_ref, z_ref[:, sl], lam_init)


def _nocache_steps(T, tq, n_sub):
    qi_t, nj_t, fl_t = [], [], []
    for qi in range(T // tq):
        last_j = qi // n_sub
        for j in range(last_j + 1):
            kind = n_sub if j < last_j else qi - j * n_sub
            qi_t.append(qi)
            nj_t.append(j)
            fl_t.append((_FLAG_FIRST if j == 0 else 0) | (_FLAG_LAST if j == last_j else 0)
                        | (kind << _KIND_SHIFT))
    as_i32 = lambda a: jnp.asarray(np.asarray(a, dtype=np.int32))
    return as_i32(qi_t), as_i32(nj_t), as_i32(fl_t), len(qi_t)


def _diff_attn_nocache(q, z, k_heads, vt_heads, lam_v, subln_g, B, T, lam_init, tq=ATTN_TQ, tk=ATTN_TK):
    tk = min(tk, T)
    assert T % tq == 0 and T % tk == 0 and tk % tq == 0 and tq % CHUNK == 0
    qi_t, nj_t, fl_t, n_steps = _nocache_steps(T, tq, tk // tq)
    nq = T // tq
    nkv = T // tk
    q_map = lambda b, t, qi, nj, fl: (b * nq + qi[t], 0)
    k_map = lambda b, t, qi, nj, fl: (0, b * nkv + nj[t], 0)
    vt_map = lambda b, t, qi, nj, fl: (0, 0, b * nkv + nj[t])
    const = lambda b, t, qi, nj, fl: (0, 0)
    grid_spec = pltpu.PrefetchScalarGridSpec(
        num_scalar_prefetch=3,
        grid=(B, n_steps),
        in_specs=[
            pl.BlockSpec((tq, DA_WIDTH), q_map),
            pl.BlockSpec((DA_HEADS, tk, LANES), k_map),
            pl.BlockSpec((DA_HEADS, DA_V_DIM + ONES_ROWS, tk), vt_map),
            pl.BlockSpec((tq, DA_WIDTH), q_map),
            pl.BlockSpec((4, DA_HEAD_DIM), const),
            pl.BlockSpec((1, DA_V_DIM), const),
        ],
        out_specs=pl.BlockSpec((tq, DA_WIDTH), q_map),
        scratch_shapes=[
            pltpu.VMEM((2 * DA_HEADS, 2 * LANES, tq), BF16),
            pltpu.VMEM((2 * DA_HEADS, DA_V_DIM + ONES_ROWS, tq), F32),
            pltpu.VMEM((2 * DA_HEADS, 1, tq), F32),
            pltpu.VMEM((tk // tq, tq, LANES), BF16),
            pltpu.VMEM((tq, tq), F32),
            pltpu.VMEM((2, tq, tq), F32),
            pltpu.VMEM((2, 1, tq), F32),
        ],
    )
    return pl.pallas_call(
        functools.partial(_diff_attn_nocache_body, tq=tq, tk=tk, lam_init=lam_init),
        grid_spec=grid_spec,
        out_shape=jax.ShapeDtypeStruct((B * T, DA_WIDTH), F32),
        compiler_params=pltpu.CompilerParams(
            dimension_semantics=("arbitrary", "arbitrary"), vmem_limit_bytes=VMEM_LIMIT),
        name="diff_attn_nocache",
    )(qi_t, nj_t, fl_t, q, k_heads, vt_heads, z, lam_v, subln_g.reshape(1, DA_V_DIM))


def _diff_attn_cache_body(q_ref, pk_ref, pv_ref, k_ref, v_ref, z_ref, lam_ref, sg_ref, o_ref,
                          qa_scr, acc_scr, m_scr, l_scr, s_scr, *, P, T, tkp, lam_init):
    t = pl.program_id(1)
    n_past = P // tkp

    @pl.when(t == 0)
    def _():
        half = lax.broadcasted_iota(jnp.int32, (T, LANES), 1) < DA_HEAD_DIM
        cblk = _slope_rows(2 * T)
        for h in range(DA_HEADS):
            sl = slice(h * DA_V_DIM, (h + 1) * DA_V_DIM)
            qh = q_ref[:, sl] * (DA_HEAD_DIM ** -0.5 * _LOG2E)
            both = jnp.concatenate([jnp.where(half, qh, 0.0), jnp.where(half, 0.0, qh)], axis=0)
            qa_scr[h, :LANES, :] = both.T.astype(BF16)
            qa_scr[h, LANES:, :] = (cblk * _alibi_slope(h)).astype(BF16)
        acc_scr[...] = jnp.zeros_like(acc_scr)
        l_scr[...] = jnp.zeros_like(l_scr)
        m_scr[...] = jnp.full_like(m_scr, NEG_BIG)

    def update(h, s, vb):
        m_old = m_scr[h]
        m_new = jnp.maximum(m_old, jnp.max(s, axis=0, keepdims=True))
        p = jnp.exp2(s - m_new)
        alpha = jnp.exp2(m_old - m_new)
        l_scr[h] = alpha * l_scr[h] + jnp.sum(p, axis=0, keepdims=True)
        acc_scr[h] = alpha * acc_scr[h] + lax.dot_general(
            vb, p.astype(BF16), _TN, preferred_element_type=F32)
        m_scr[h] = m_new

    @pl.when(t < n_past)
    def _():
        pos_blk = _position_columns(tkp, t * tkp - P)
        for h in range(DA_HEADS):
            rows = pl.ds(h, tkp, stride=DA_HEADS)
            ka = jnp.concatenate([pk_ref[rows, :].astype(BF16), pos_blk], axis=1)
            s_scr[h] = jnp.dot(ka, qa_scr[h], preferred_element_type=F32)
        for h in range(DA_HEADS):
            rows = pl.ds(h, tkp, stride=DA_HEADS)
            update(h, s_scr[h], pv_ref[rows, :].astype(BF16))

    @pl.when(t == n_past)
    def _():
        pos_blk = _position_columns(T, 0)
        kidx = lax.broadcasted_iota(jnp.int32, (T, 2 * T), 0)
        qidx = lax.broadcasted_iota(jnp.int32, (T, 2 * T), 1) & (T - 1)
        adj = _diag_adjust(kidx, qidx, P, P)
        for h in range(DA_HEADS):
            rows = pl.ds(h, T, stride=DA_HEADS)
            ka = jnp.concatenate([k_ref[rows, :].astype(BF16), pos_blk], axis=1)
            s = jnp.dot(ka, qa_scr[h], preferred_element_type=F32) + adj * _alibi_slope(h)
            update(h, s, v_ref[rows, :].astype(BF16))

        lam = _lambda(lam_ref, lam_init)
        for h in range(DA_HEADS):
            sl = slice(h * DA_V_DIM, (h + 1) * DA_V_DIM)
            a = (acc_scr[h] * (1.0 / l_scr[h])).T
            o_ref[:, sl] = _subln_gate(a[:T] - lam * a[T:], sg_ref, z_ref[:, sl], lam_init)


def _diff_attn_cache(q, z, k_new, v_new, past_k, past_v, lam_v, subln_g, B, T, lam_init, tkp=CACHE_TK):
    P = past_k.shape[1]
    assert 2 * T == LANES and T == CHUNK and P % tkp == 0 and P % CHUNK == 0
    n_past = P // tkp
    tok = lambda b, t: (b, 0)
    past = lambda b, t: (b, jnp.minimum(t, n_past - 1), 0)
    const = lambda b, t: (0, 0)
    pk = past_k.reshape(B, P * DA_HEADS, 2 * DA_HEAD_DIM)
    pv = past_v.reshape(B, P * DA_HEADS, DA_V_DIM)
    return pl.pallas_call(
        functools.partial(_diff_attn_cache_body, P=P, T=T, tkp=tkp, lam_init=lam_init),
        grid=(B, n_past + 1),
        in_specs=[
            pl.BlockSpec((T, DA_WIDTH), tok),
            pl.BlockSpec((None, tkp * DA_HEADS, LANES), past),
            pl.BlockSpec((None, tkp * DA_HEADS, LANES), past),
            pl.BlockSpec((T * DA_HEADS, LANES), tok),
            pl.BlockSpec((T * DA_HEADS, LANES), tok),
            pl.BlockSpec((T, DA_WIDTH), tok),
            pl.BlockSpec((4, DA_HEAD_DIM), const),
            pl.BlockSpec((1, DA_V_DIM), const),
        ],
        out_specs=pl.BlockSpec((T, DA_WIDTH), tok),
        out_shape=jax.ShapeDtypeStruct((B * T, DA_WIDTH), F32),
        scratch_shapes=[
            pltpu.VMEM((DA_HEADS, 2 * LANES, 2 * T), BF16),
            pltpu.VMEM((DA_HEADS, DA_V_DIM, 2 * T), F32),
            pltpu.VMEM((DA_HEADS, 1, 2 * T), F32),
            pltpu.VMEM((DA_HEADS, 1, 2 * T), F32),
            pltpu.VMEM((DA_HEADS, tkp, 2 * T), F32),
        ],
        compiler_params=pltpu.CompilerParams(
            dimension_semantics=("arbitrary", "arbitrary"), vmem_limit_bytes=VMEM_LIMIT),
        name="diff_attn_cache",
    )(q, pk, pv, k_new, v_new, z, lam_v, subln_g.reshape(1, DA_V_DIM))


def _trunk(x, c0, n0, m0, past_k, past_v, mem_k, mem_v, wts):
    B, T, _ = x.shape
    n_tok = B * T
    x2 = x.reshape(n_tok, D_MODEL)

    q, k, v, o, z, mq, mz, gates = _norm_matmul(
        x2, wts["norm_g"][0], [wts["w_a_main"], wts["w_a_mem"]], [[ML_WIDTH] * 5, [MEM_WIDTH] * 2],
        gates_w=wts["w_a_gates"], name="in_proj_a")
    hm, c_new, n_new, m_new = _mlstm(
        q, k, v, o, z, gates, wts["b_gate"], wts["head_g"],
        c0, n0.reshape(B, ML_HEADS, 1, ML_HEAD_DIM),
        jnp.broadcast_to(m0.reshape(B, ML_HEADS, 1, 1), (B, ML_HEADS, 1, LANES)), B, T)
    x1 = _epilogue(x2, hm, mq, mz, mem_k[0][0], mem_v[0][0], wts["w_out_a1"], wts["w_out_a2"], B, T,
                   mem_layer=mem_k[0][1], name="epilogue_a")

    kv_splits = [[DA_WIDTH, DA_WIDTH]]
    qd, zd, mq2, mz2 = _norm_matmul(x1, wts["norm_g"][1], [wts["w_b"]],
                                    [[DA_WIDTH, DA_WIDTH, MEM_WIDTH, MEM_WIDTH]], name="in_proj_b")
    lam_init = 0.8 - 0.6 * math.exp(-0.3 * 1)
    if past_k is None:
        k_new, v_new, k_heads, vt_heads = _norm_matmul(x1, wts["kv_norm_g"], [wts["w_kv"]], kv_splits,
                                                       head_major=((0, False), (1, True)),
                                                       interleave=(0, 1), name="kv_proj")
        od = _diff_attn_nocache(qd, zd, k_heads, vt_heads, wts["lam_b"], wts["subln_g"], B, T, lam_init)
    else:
        k_new, v_new = _norm_matmul(x1, wts["kv_norm_g"], [wts["w_kv"]], kv_splits, interleave=(0, 1),
                                    name="kv_proj")
        od = _diff_attn_cache(qd, zd, k_new, v_new, past_k, past_v, wts["lam_b"], wts["subln_g"], B, T,
                              lam_init)
    y = _epilogue(x1, od, mq2, mz2, mem_k[1][0], mem_v[1][0], wts["w_out_b1"], wts["w_out_b2"], B, T,
                  final_g=wts["final_norm_g"], mem_layer=mem_k[1][1], name="epilogue_b")

    return (y.reshape(B, T, D_MODEL),
            c_new.reshape(1, B, ML_HEADS, ML_HEAD_DIM, ML_HEAD_DIM),
            n_new.reshape(1, B, ML_HEADS, ML_HEAD_DIM),
            m_new[..., 0, 0].reshape(1, B, ML_HEADS),
            k_new.reshape(B, T, DA_HEADS, 2 * DA_HEAD_DIM),
            v_new.reshape(B, T, DA_HEADS, DA_V_DIM))


def kernel(x_prompt, x_sample, cache_k, cache_v, cache_mem_k, cache_mem_v, state_C, state_n, state_m, mem_prompt, norm_g, final_norm_g, mem_norm_g, w_mem_kv, w_in_a, b_gate_a, head_g_a, w_out_a, kv_norm_g, w_kv, w_in_b, lam_b, subln_g_b, w_out_b):
    B = x_prompt.shape[0]
    DB = x_sample.shape[0]
    n_gate = 2 * ML_HEADS
    g0 = 5 * ML_WIDTH
    w_a = w_in_a[0]
    w_gates = jnp.pad(w_a[:, g0:g0 + n_gate], ((0, 0), (0, LANES - n_gate)))
    w_gates_hi = w_gates.astype(BF16)
    w_gates_lo = (w_gates - w_gates_hi.astype(F32)).astype(BF16)
    wts = {
        "norm_g": norm_g,
        "final_norm_g": final_norm_g,
        "kv_norm_g": kv_norm_g,
        "w_a_main": w_a[:, :g0].astype(BF16),
        "w_a_mem": w_a[:, g0 + n_gate:].astype(BF16),
        "w_a_gates": jnp.concatenate([w_gates_hi, w_gates_lo], axis=1),
        "b_gate": jnp.pad(b_gate_a[0], (0, LANES - n_gate)).reshape(1, LANES),
        "head_g": head_g_a[0].reshape(1, ML_WIDTH),
        "w_out_a1": w_out_a[0, :ML_WIDTH].astype(BF16),
        "w_out_a2": w_out_a[0, ML_WIDTH:].astype(BF16),
        "w_kv": w_kv.astype(BF16),
        "w_b": w_in_b[0].astype(BF16),
        "lam_b": lam_b[0],
        "subln_g": subln_g_b[0],
        "w_out_b1": w_out_b[0, :DA_WIDTH].astype(BF16),
        "w_out_b2": w_out_b[0, DA_WIDTH:].astype(BF16),
    }

    mem2 = mem_prompt.reshape(B * MEM_LEN, D_MODEL)
    mks, mvs = [], []
    for l in range(2):
        mk, mv = _norm_matmul(mem2, mem_norm_g[l], [w_mem_kv[l].astype(BF16)], [[MEM_WIDTH, MEM_WIDTH]],
                              interleave=(0, 1), name="mem_kv")
        mks.append(mk.reshape(B, MEM_LEN * MEM_HEADS, MEM_HEAD_DIM))
        mvs.append(mv.reshape(B, MEM_LEN * MEM_HEADS, MEM_HEAD_DIM))
    prompt_mem_k = jnp.stack(mks).reshape(2, B, MEM_LEN, MEM_HEADS, MEM_HEAD_DIM)
    prompt_mem_v = jnp.stack(mvs).reshape(2, B, MEM_LEN, MEM_HEADS, MEM_HEAD_DIM)

    zc = jnp.zeros((B, ML_HEADS, ML_HEAD_DIM, ML_HEAD_DIM), F32)
    zn = jnp.zeros((B, ML_HEADS, ML_HEAD_DIM), F32)
    zm = jnp.zeros((B, ML_HEADS), F32)
    y_prompt, prompt_C, prompt_n, prompt_m, prompt_k, prompt_v = _trunk(
        x_prompt, zc, zn, zm, None, None, [(m, 0) for m in mks], [(m, 0) for m in mvs], wts)

    cmk = cache_mem_k.reshape(2 * DB, MEM_LEN * MEM_HEADS, MEM_HEAD_DIM)
    cmv = cache_mem_v.reshape(2 * DB, MEM_LEN * MEM_HEADS, MEM_HEAD_DIM)
    smk = [(cmk, l) for l in range(2)]
    smv = [(cmv, l) for l in range(2)]
    y_sample, sample_C, sample_n, sample_m, sample_k, sample_v = _trunk(
        x_sample, state_C[0], state_n[0], state_m[0], cache_k, cache_v, smk, smv, wts)

    return (y_prompt, y_sample, prompt_C, prompt_n, prompt_m, prompt_k, prompt_v, prompt_mem_k, prompt_mem_v,
            sample_C, sample_n, sample_m, sample_k, sample_v)
```

```python
import functools
import math

import numpy as np
import jax
import jax.numpy as jnp
from jax import lax
from jax.experimental import pallas as pl
from jax.experimental.pallas import tpu as pltpu

F32 = jnp.float32
BF16 = jnp.bfloat16

D_MODEL = 1024
CHUNK = 64
ML_HEADS = 4
ML_HEAD_DIM = 256
ML_WIDTH = 1024
DA_HEADS = 8
DA_HEAD_DIM = 64
DA_V_DIM = 128
DA_WIDTH = DA_HEADS * DA_V_DIM
MEM_LEN = 256
MEM_HEADS = 4
MEM_HEAD_DIM = 128
MEM_WIDTH = 512
EPS = 1e-6
NEG_BIG = -1e30

LANES = 128
BF16_SUBLANES = 16
VMEM_LIMIT = 56 * 1024 * 1024
ROW_TILE = 512
MLSTM_CHUNK = 256
ATTN_TQ = 512
ATTN_TK = 2048
CACHE_TK = 1024
ONES_ROWS = BF16_SUBLANES

_NT = (((1,), (1,)), ((), ()))
_TN = (((0,), (0,)), ((), ()))

_LOG2E = 1.4426950216293335
_LOG2E_BF16_PARTS = (1.4453125, -0.00262451171875, 7.033348083496094e-06)
_N_PARTS = len(_LOG2E_BF16_PARTS)


def _sigmoid(x):
    return 1.0 / (1.0 + jnp.exp(-x))


def _alibi_slope(h):
    return 2.0 ** (-8.0 * (h + 1) / DA_HEADS)


def _norm_matmul_body(*refs, n_weights, splits, with_gates, head_major, interleave):
    x_ref, g_ref = refs[:2]
    w_refs = refs[2:2 + n_weights]
    rest = refs[2 + n_weights:]
    if with_gates:
        wg_ref, rest = rest[0], rest[1:]
        gate_out, rest = rest[-1], rest[:-1]
    n_out = sum(len(s) for s in splits)
    outs, hm_outs = rest[:n_out], rest[n_out:]
    x = x_ref[...]
    xn = x * lax.rsqrt(jnp.mean(x * x, axis=-1, keepdims=True) + EPS) * g_ref[...]
    xb = xn.astype(BF16)
    i = 0
    for w_ref, widths in zip(w_refs, splits):
        off = 0
        for width in widths:
            o_ref = outs[i]
            r = jnp.dot(xb, w_ref[:, off:off + width], preferred_element_type=F32)
            if i in interleave:
                nh = width // LANES
                for h in range(nh):
                    o_ref[pl.ds(h, x.shape[0], stride=nh), :] = r[:, h * LANES:(h + 1) * LANES]
            else:
                o_ref[...] = r
            for (split, transposed), hb_ref in zip(head_major, hm_outs):
                if split == i:
                    for h in range(width // LANES):
                        rh = r[:, h * LANES:(h + 1) * LANES]
                        if transposed:
                            hb_ref[h, :LANES, :] = rh.T.astype(BF16)
                            extra = lax.broadcasted_iota(jnp.int32, (ONES_ROWS, rh.shape[0]), 0) == 0
                            hb_ref[h, LANES:, :] = jnp.where(extra, 1.0, 0.0).astype(BF16)
                        else:
                            hb_ref[h] = rh.astype(BF16)
            off += width
            i += 1
    if with_gates:
        x_lo = (xn - xb.astype(F32)).astype(BF16)
        g_hi = jnp.dot(xb, wg_ref[...], preferred_element_type=F32)
        g_lo = jnp.dot(x_lo, wg_ref[:, :LANES], preferred_element_type=F32)
        gate_out[...] = g_hi[:, :LANES] + g_hi[:, LANES:] + g_lo


def _norm_matmul(x, g, weights, splits, gates_w=None, head_major=(), interleave=(), name="norm_matmul"):
    n, d = x.shape
    tm = min(ROW_TILE, n)
    assert n % tm == 0 and all(sum(s) == w.shape[1] for s, w in zip(splits, weights))
    with_gates = gates_w is not None
    head_major = tuple(head_major)
    interleave = tuple(interleave)
    flat = [width for s in splits for width in s]
    row = lambda i: (i, 0)
    const = lambda i: (0, 0)
    in_specs = [pl.BlockSpec((tm, d), row), pl.BlockSpec((1, d), const)]
    in_specs += [pl.BlockSpec(w.shape, const, pipeline_mode=pl.Buffered(1)) for w in weights]
    args = [x, g.reshape(1, d), *weights]
    out_shape, out_specs = [], []
    for i, width in enumerate(flat):
        rows, cols = (width // LANES, LANES) if i in interleave else (1, width)
        out_shape.append(jax.ShapeDtypeStruct((n * rows, cols), F32))
        out_specs.append(pl.BlockSpec((tm * rows, cols), row))
    for split, transposed in head_major:
        nh = flat[split] // LANES
        if transposed:
            out_shape.append(jax.ShapeDtypeStruct((nh, LANES + ONES_ROWS, n), BF16))
            out_specs.append(pl.BlockSpec((nh, LANES + ONES_ROWS, tm), lambda i: (0, 0, i)))
        else:
            out_shape.append(jax.ShapeDtypeStruct((nh, n, LANES), BF16))
            out_specs.append(pl.BlockSpec((nh, tm, LANES), lambda i: (0, i, 0)))
    if with_gates:
        in_specs.append(pl.BlockSpec((d, 2 * LANES), const))
        args.append(gates_w)
        out_shape.append(jax.ShapeDtypeStruct((n, LANES), F32))
        out_specs.append(pl.BlockSpec((tm, LANES), row))
    return pl.pallas_call(
        functools.partial(_norm_matmul_body, n_weights=len(weights),
                          splits=tuple(tuple(s) for s in splits), with_gates=with_gates,
                          head_major=head_major, interleave=interleave),
        grid=(n // tm,),
        in_specs=in_specs,
        out_specs=out_specs,
        out_shape=out_shape,
        compiler_params=pltpu.CompilerParams(
            dimension_semantics=("arbitrary",), vmem_limit_bytes=VMEM_LIMIT),
        name=name,
    )(*args)


def _mlstm_body(q_ref, k_ref, v_ref, o_ref, z_ref, gt_ref, bg_ref, hg_ref, c0_ref, n0_ref, m0_ref,
                h_out, c_out, n_out, m_out, c_s, n_s, m_s, *, L, nc):
    c = pl.program_id(1)

    @pl.when(c == 0)
    def _():
        c_s[...] = c0_ref[0]
        n_s[...] = n0_ref[0]
        m_s[...] = m0_ref[0]

    gc = gt_ref[...] + bg_ref[...]
    lane = lax.broadcasted_iota(jnp.int32, (L, LANES), 1)
    lf = jnp.minimum(gc, 0.0) - jnp.log1p(jnp.exp(-jnp.abs(gc)))
    row = lax.broadcasted_iota(jnp.int32, (L, L), 0)
    col = lax.broadcasted_iota(jnp.int32, (L, L), 1)
    tril = col <= row
    lf_hi = lf.astype(BF16)
    rem = lf - lf_hi.astype(F32)
    lf_mid = rem.astype(BF16)
    lf_lo = (rem - lf_mid.astype(F32)).astype(BF16)
    parts = jnp.dot(jnp.where(tril, 1.0, 0.0).astype(BF16), jnp.concatenate([lf_hi, lf_mid, lf_lo], axis=1),
                    preferred_element_type=F32)
    gcum = parts[:, :LANES] + parts[:, LANES:2 * LANES] + parts[:, 2 * LANES:]
    comb = jnp.where(lane < ML_HEADS, gc, gcum)
    rows = comb.T

    stash = []
    for h in range(ML_HEADS):
        sl = slice(h * ML_HEAD_DIM, (h + 1) * ML_HEAD_DIM)
        src_r = rows[h:h + 1, :] - rows[ML_HEADS + h:ML_HEADS + h + 1, :]
        g_c = comb[:, ML_HEADS + h:ML_HEADS + h + 1]
        m_prev = m_s[h][:, :1]

        dmat = jnp.where(tril, g_c + src_r, NEG_BIG)
        inter = g_c + m_prev
        m_t = jnp.maximum(inter, jnp.max(dmat, axis=-1, keepdims=True))
        w_intra = jnp.exp(dmat - m_t)
        w_inter = jnp.exp(inter - m_t)

        qb = q_ref[:, sl].astype(BF16)
        kb = (k_ref[:, sl] * (ML_HEAD_DIM ** -0.5)).astype(BF16)
        qk = lax.dot_general(qb, kb, _NT, preferred_element_type=F32)
        s = w_intra * qk
        stash.append((m_t, w_inter, s.astype(BF16), jnp.sum(s, axis=-1, keepdims=True)))

    for h in range(ML_HEADS):
        sl = slice(h * ML_HEAD_DIM, (h + 1) * ML_HEAD_DIM)
        m_t, w_inter, sb, s_sum = stash[h]
        ig_c = comb[:, h:h + 1]
        g_c = comb[:, ML_HEADS + h:ML_HEADS + h + 1]
        m_prev = m_s[h][:, :1]
        qh = q_ref[:, sl]
        kh = k_ref[:, sl] * (ML_HEAD_DIM ** -0.5)
        vh = v_ref[:, sl]
        qb = qh.astype(BF16)
        kb = kh.astype(BF16)
        vb = vh.astype(BF16)
        ch = c_s[h]
        nh = n_s[h]
        cq = lax.dot_general(qb, ch.astype(BF16), _NT, preferred_element_type=F32)
        num = w_inter * cq + jnp.dot(sb, vb, preferred_element_type=F32)
        nq = jnp.sum(qh * nh, axis=-1, keepdims=True)
        den = w_inter * nq + s_sum
        hh = num * (1.0 / jnp.maximum(jnp.abs(den), jnp.exp(-m_t)))

        g_last = g_c[L - 1:L, :]
        m_new = m_t[L - 1:L, :]
        w_s = jnp.exp(g_last - g_c + ig_c - m_new)
        dec = jnp.exp(g_last + m_prev - m_new)
        vw = (vh * w_s).astype(BF16)
        c_s[h] = dec * ch + lax.dot_general(vw, kb, _TN, preferred_element_type=F32)
        n_s[h] = dec * nh + jnp.sum(kh * w_s, axis=0, keepdims=True)
        m_s[h] = jnp.broadcast_to(m_new, (1, LANES))

        oh = o_ref[:, sl]
        zh = z_ref[:, sl]
        hm = _sigmoid(oh) * hh
        hm = hm * lax.rsqrt(jnp.mean(hm * hm, axis=-1, keepdims=True) + EPS) * hg_ref[:, sl]
        h_out[:, sl] = hm * (zh * _sigmoid(zh))

    @pl.when(c == nc - 1)
    def _():
        c_out[0] = c_s[...]
        n_out[0] = n_s[...]
        m_out[0] = m_s[...]


def _mlstm(q, k, v, o, z, gates, b_gate, head_g, c0, n0, m0, B, T):
    L = min(T, MLSTM_CHUNK)
    assert T % L == 0
    nc = T // L
    dh = ML_HEAD_DIM
    tok = lambda b, c: (b * nc + c, 0)
    st4 = lambda b, c: (b, 0, 0, 0)
    return pl.pallas_call(
        functools.partial(_mlstm_body, L=L, nc=nc),
        grid=(B, nc),
        in_specs=[pl.BlockSpec((L, ML_WIDTH), tok)] * 5 + [
            pl.BlockSpec((L, LANES), tok),
            pl.BlockSpec((1, LANES), lambda b, c: (0, 0)),
            pl.BlockSpec((1, ML_WIDTH), lambda b, c: (0, 0)),
            pl.BlockSpec((1, ML_HEADS, dh, dh), st4),
            pl.BlockSpec((1, ML_HEADS, 1, dh), st4),
            pl.BlockSpec((1, ML_HEADS, 1, LANES), st4),
        ],
        out_specs=[
            pl.BlockSpec((L, ML_WIDTH), tok),
            pl.BlockSpec((1, ML_HEADS, dh, dh), st4),
            pl.BlockSpec((1, ML_HEADS, 1, dh), st4),
            pl.BlockSpec((1, ML_HEADS, 1, LANES), st4),
        ],
        out_shape=[
            jax.ShapeDtypeStruct((B * T, ML_WIDTH), F32),
            jax.ShapeDtypeStruct((B, ML_HEADS, dh, dh), F32),
            jax.ShapeDtypeStruct((B, ML_HEADS, 1, dh), F32),
            jax.ShapeDtypeStruct((B, ML_HEADS, 1, LANES), F32),
        ],
        scratch_shapes=[
            pltpu.VMEM((ML_HEADS, dh, dh), F32),
            pltpu.VMEM((ML_HEADS, 1, dh), F32),
            pltpu.VMEM((ML_HEADS, 1, LANES), F32),
        ],
        compiler_params=pltpu.CompilerParams(
            dimension_semantics=("arbitrary", "arbitrary"), vmem_limit_bytes=VMEM_LIMIT),
        name="mlstm",
    )(q, k, v, o, z, gates, b_gate, head_g, c0, n0, m0)


def _epilogue_body(*refs, final_norm, nb, rows_per_batch):
    if final_norm:
        x_ref, a_ref, mq_ref, mz_ref, mk_ref, mv_ref, w1_ref, w2_ref, fg_ref, y_ref = refs
    else:
        x_ref, a_ref, mq_ref, mz_ref, mk_ref, mv_ref, w1_ref, w2_ref, y_ref = refs
    acc = x_ref[...] + jnp.dot(a_ref[...].astype(BF16), w1_ref[...], preferred_element_type=F32)

    def tok(bi):
        return slice(bi * rows_per_batch, (bi + 1) * rows_per_batch)

    def head_rows(h):
        return pl.ds(h, MEM_LEN, stride=MEM_HEADS)

    scores = {}
    for bi in range(nb):
        for h in range(MEM_HEADS):
            sl = slice(h * MEM_HEAD_DIM, (h + 1) * MEM_HEAD_DIM)
            qh = mq_ref[tok(bi), sl].astype(BF16)
            kh = mk_ref.at[bi][head_rows(h), :].astype(BF16)
            scores[bi, h] = lax.dot_general(qh, kh, _NT, preferred_element_type=F32) * (MEM_HEAD_DIM ** -0.5)
    mo_rows = []
    for bi in range(nb):
        mos = []
        for h in range(MEM_HEADS):
            sl = slice(h * MEM_HEAD_DIM, (h + 1) * MEM_HEAD_DIM)
            s = scores[bi, h]
            e = jnp.exp(s - jnp.max(s, axis=-1, keepdims=True))
            p = e * (1.0 / jnp.sum(e, axis=-1, keepdims=True))
            vh = mv_ref.at[bi][head_rows(h), :].astype(BF16)
            oh = jnp.dot(p.astype(BF16), vh, preferred_element_type=F32)
            zh = mz_ref[tok(bi), sl]
            mos.append((oh * (zh * _sigmoid(zh))).astype(BF16))
        mo_rows.append(jnp.concatenate(mos, axis=-1))
    mo = mo_rows[0] if nb == 1 else jnp.concatenate(mo_rows, axis=0)
    acc = acc + jnp.dot(mo, w2_ref[...], preferred_element_type=F32)
    if final_norm:
        acc = acc * lax.rsqrt(jnp.mean(acc * acc, axis=-1, keepdims=True) + EPS) * fg_ref[...]
    y_ref[...] = acc


def _epilogue(x, a, mq, mz, mem_k, mem_v, w1, w2, B, T, final_g=None, mem_layer=0, name="epilogue"):
    if T >= ROW_TILE:
        nb, rows_per_batch, tm = 1, ROW_TILE, ROW_TILE
    else:
        nb, rows_per_batch, tm = min(B, ROW_TILE // T), T, min(B, ROW_TILE // T) * T
    assert T % rows_per_batch == 0 and B % nb == 0
    nt = T // rows_per_batch
    tok = lambda b, i: (b * nt + i, 0)
    const = lambda b, i: (0, 0)
    mem_map = lambda b, i: (mem_layer * (B // nb) + b, 0, 0)
    final_norm = final_g is not None
    in_specs = [
        pl.BlockSpec((tm, D_MODEL), tok),
        pl.BlockSpec((tm, a.shape[1]), tok),
        pl.BlockSpec((tm, MEM_WIDTH), tok),
        pl.BlockSpec((tm, MEM_WIDTH), tok),
        pl.BlockSpec((nb, MEM_LEN * MEM_HEADS, MEM_HEAD_DIM), mem_map),
        pl.BlockSpec((nb, MEM_LEN * MEM_HEADS, MEM_HEAD_DIM), mem_map),
        pl.BlockSpec(w1.shape, const, pipeline_mode=pl.Buffered(1)),
        pl.BlockSpec(w2.shape, const, pipeline_mode=pl.Buffered(1)),
    ]
    args = [x, a, mq, mz, mem_k, mem_v, w1, w2]
    if final_norm:
        in_specs.append(pl.BlockSpec((1, D_MODEL), const))
        args.append(final_g.reshape(1, D_MODEL))
    return pl.pallas_call(
        functools.partial(_epilogue_body, final_norm=final_norm, nb=nb, rows_per_batch=rows_per_batch),
        grid=(B // nb, nt),
        in_specs=in_specs,
        out_specs=pl.BlockSpec((tm, D_MODEL), tok),
        out_shape=jax.ShapeDtypeStruct((B * T, D_MODEL), F32),
        compiler_params=pltpu.CompilerParams(
            dimension_semantics=("arbitrary", "arbitrary"), vmem_limit_bytes=VMEM_LIMIT),
        name=name,
    )(*args)


_FLAG_FIRST = 1
_KIND_SHIFT = 1


def _position_columns(n_keys, rel_start):
    rel0 = lax.broadcasted_iota(jnp.int32, (n_keys, LANES), 0) + rel_start
    lane_k = lax.broadcasted_iota(jnp.int32, (n_keys, LANES), 1)
    hi = ((rel0 >> 7) << 7).astype(F32)
    lo = (rel0 & 127).astype(F32)
    return jnp.where(lane_k < _N_PARTS, hi, jnp.where(lane_k < 2 * _N_PARTS, lo, 0.0)).astype(BF16)


def _slope_rows(n_queries):
    row = lax.broadcasted_iota(jnp.int32, (LANES, n_queries), 0)
    cblk = jnp.zeros((LANES, n_queries), F32)
    for i, part in enumerate(_LOG2E_BF16_PARTS):
        cblk = jnp.where((row == i) | (row == i + _N_PARTS), part, cblk)
    return cblk


def _diag_adjust(kidx, qidx, k_abs, q_abs):
    visible = ((kidx + k_abs) >> 6) <= ((qidx + q_abs) >> 6)
    rel = kidx + k_abs - qidx - q_abs
    return jnp.where(visible, jnp.maximum(rel, 0).astype(F32) * (-2.0 * _LOG2E), NEG_BIG)


def _lambda(lam_ref, lam_init):
    lv = lam_ref[...]
    return (jnp.exp(jnp.sum(lv[0:1] * lv[1:2], axis=-1, keepdims=True))
            - jnp.exp(jnp.sum(lv[2:3] * lv[3:4], axis=-1, keepdims=True)) + lam_init)


def _subln_gate(o, sg_ref, zh, lam_init):
    o = o * lax.rsqrt(jnp.mean(o * o, axis=-1, keepdims=True) + EPS) * sg_ref[...] * (1.0 - lam_init)
    return o * (zh * _sigmoid(zh))


def _diff_attn_nocache_body(qi_ref, nj_ref, fl_ref, q_ref, k_ref, vt_ref, z_ref, lam_ref, sg_ref, o_ref,
                            qa_scr, acc_scr, m_scr, pos_scr, adj_scr, s_scr, mx_scr,
                            *, tq, tk, lam_init):
    t = pl.program_id(1)
    flags = fl_ref[t]
    q_start = qi_ref[t] * tq
    k_start = nj_ref[t] * tk

    @pl.when((flags & _FLAG_FIRST) != 0)
    def _():
        half = lax.broadcasted_iota(jnp.int32, (LANES, tq), 0) < DA_HEAD_DIM
        cblk = _slope_rows(tq)
        for h in range(DA_HEADS):
            sl = slice(h * DA_V_DIM, (h + 1) * DA_V_DIM)
            qh = (q_ref[:, sl] * (DA_HEAD_DIM ** -0.5 * _LOG2E)).T
            cb = (cblk * _alibi_slope(h)).astype(BF16)
            for c in range(2):
                keep = half if c == 0 else jnp.logical_not(half)
                qa_scr[2 * h + c, :LANES, :] = jnp.where(keep, qh, 0.0).astype(BF16)
                qa_scr[2 * h + c, LANES:, :] = cb
        acc_scr[...] = jnp.zeros_like(acc_scr)
        m_scr[...] = jnp.full_like(m_scr, NEG_BIG)
        pos_scr[...] = _position_columns(tq, 0)
        kidx = lax.broadcasted_iota(jnp.int32, (tq, tq), 0)
        qidx = lax.broadcasted_iota(jnp.int32, (tq, tq), 1)
        adj_scr[...] = _diag_adjust(kidx, qidx, q_start, q_start)

    n_sub = tk // tq
    n_maps = 2 * DA_HEADS

    def finalize():
        lam = _lambda(lam_ref, lam_init)
        for h in range(DA_HEADS):
            sl = slice(h * DA_V_DIM, (h + 1) * DA_V_DIM)
            a1 = acc_scr[2 * h]
            a2 = acc_scr[2 * h + 1]
            ot = (a1[:DA_V_DIM] * (1.0 / a1[DA_V_DIM:DA_V_DIM + 1])
                  - lam * (a2[:DA_V_DIM] * (1.0 / a2[DA_V_DIM:DA_V_DIM + 1])))
            o_ref[:, sl] = _subln_gate(ot.T, sg_ref, z_ref[:, sl], lam_init)

    def attend(modes):
        live = [sb for sb in range(n_sub) if modes[sb] is not None]
        tasks = [(sb, idx) for sb in live for idx in range(n_maps)]
        offset = [(k_start + sb * tq - q_start).astype(F32) for sb in range(n_sub)]

        def scores(n):
            sb, idx = tasks[n]
            h = idx // 2
            ka = jnp.concatenate([k_ref[h, sb * tq:(sb + 1) * tq, :], pos_scr[...]], axis=1)
            s = jnp.dot(ka, qa_scr[idx], preferred_element_type=F32)
            if modes[sb] == 'diag':
                s = s + adj_scr[...] * _alibi_slope(h)
            s_scr[n % 2] = s
            mx_scr[n % 2] = jnp.max(s, axis=0, keepdims=True)

        def softmax(n):
            sb, idx = tasks[n]
            s = s_scr[n % 2]
            shift = offset[sb] * (_alibi_slope(idx // 2) * _LOG2E)
            m_old = m_scr[idx]
            m_new = jnp.maximum(m_old, mx_scr[n % 2] + shift)
            p = jnp.exp2(s - (m_new - shift)).astype(BF16)
            alpha = jnp.exp2(m_old - m_new)
            m_scr[idx] = m_new
            acc_scr[idx] = alpha * acc_scr[idx] + jnp.dot(
                vt_ref[idx // 2, :, sb * tq:(sb + 1) * tq], p, preferred_element_type=F32)

        scores(0)
        for n in range(len(tasks)):
            if n + 1 < len(tasks):
                scores(n + 1)
            softmax(n)
        if 'diag' in modes:
            finalize()

    kind = flags >> _KIND_SHIFT
    for d in range(n_sub + 1):
        modes = ['past'] * n_sub if d == n_sub else ['past'] * d + ['diag'] + [None] * (n_sub - d - 1)
        pl.when(kind == d)(functools.partial(attend, modes))


def _nocache_steps(T, tq, n_sub):
    qi_t, nj_t, fl_t = [], [], []
    for qi in range(T // tq):
        last_j = qi // n_sub
        for j in range(last_j + 1):
            kind = n_sub if j < last_j else qi - j * n_sub
            qi_t.append(qi)
            nj_t.append(j)
            fl_t.append((_FLAG_FIRST if j == 0 else 0) | (kind << _KIND_SHIFT))
    as_i32 = lambda a: jnp.asarray(np.asarray(a, dtype=np.int32))
    return as_i32(qi_t), as_i32(nj_t), as_i32(fl_t), len(qi_t)


def _diff_attn_nocache(q, z, k_heads, vt_heads, lam_v, subln_g, B, T, lam_init, tq=ATTN_TQ, tk=ATTN_TK):
    tk = min(tk, T)
    assert T % tq == 0 and T % tk == 0 and tk % tq == 0 and tq % CHUNK == 0
    qi_t, nj_t, fl_t, n_steps = _nocache_steps(T, tq, tk // tq)
    nq = T // tq
    nkv = T // tk
    q_map = lambda b, t, qi, nj, fl: (b * nq + qi[t], 0)
    k_map = lambda b, t, qi, nj, fl: (0, b * nkv + nj[t], 0)
    vt_map = lambda b, t, qi, nj, fl: (0, 0, b * nkv + nj[t])
    const = lambda b, t, qi, nj, fl: (0, 0)
    grid_spec = pltpu.PrefetchScalarGridSpec(
        num_scalar_prefetch=3,
        grid=(B, n_steps),
        in_specs=[
            pl.BlockSpec((tq, DA_WIDTH), q_map),
            pl.BlockSpec((DA_HEADS, tk, LANES), k_map),
            pl.BlockSpec((DA_HEADS, DA_V_DIM + ONES_ROWS, tk), vt_map),
            pl.BlockSpec((tq, DA_WIDTH), q_map),
            pl.BlockSpec((4, DA_HEAD_DIM), const),
            pl.BlockSpec((1, DA_V_DIM), const),
        ],
        out_specs=pl.BlockSpec((tq, DA_WIDTH), q_map),
        scratch_shapes=[
            pltpu.VMEM((2 * DA_HEADS, 2 * LANES, tq), BF16),
            pltpu.VMEM((2 * DA_HEADS, DA_V_DIM + ONES_ROWS, tq), F32),
            pltpu.VMEM((2 * DA_HEADS, 1, tq), F32),
            pltpu.VMEM((tq, LANES), BF16),
            pltpu.VMEM((tq, tq), F32),
            pltpu.VMEM((2, tq, tq), F32),
            pltpu.VMEM((2, 1, tq), F32),
        ],
    )
    return pl.pallas_call(
        functools.partial(_diff_attn_nocache_body, tq=tq, tk=tk, lam_init=lam_init),
        grid_spec=grid_spec,
        out_shape=jax.ShapeDtypeStruct((B * T, DA_WIDTH), F32),
        compiler_params=pltpu.CompilerParams(
            dimension_semantics=("arbitrary", "arbitrary"), vmem_limit_bytes=VMEM_LIMIT),
        name="diff_attn_nocache",
    )(qi_t, nj_t, fl_t, q, k_heads, vt_heads, z, lam_v, subln_g.reshape(1, DA_V_DIM))


def _diff_attn_cache_body(q_ref, pk_ref, pv_ref, k_ref, v_ref, z_ref, lam_ref, sg_ref, o_ref,
                          qa_scr, acc_scr, m_scr, l_scr, s_scr, p_scr, *, P, T, tkp, lam_init):
    t = pl.program_id(1)
    n_past = P // tkp

    @pl.when(t == 0)
    def _():
        half = lax.broadcasted_iota(jnp.int32, (T, LANES), 1) < DA_HEAD_DIM
        cblk = _slope_rows(2 * T)
        for h in range(DA_HEADS):
            sl = slice(h * DA_V_DIM, (h + 1) * DA_V_DIM)
            qh = q_ref[:, sl] * (DA_HEAD_DIM ** -0.5 * _LOG2E)
            both = jnp.concatenate([jnp.where(half, qh, 0.0), jnp.where(half, 0.0, qh)], axis=0)
            qa_scr[h, :LANES, :] = both.T.astype(BF16)
            qa_scr[h, LANES:, :] = (cblk * _alibi_slope(h)).astype(BF16)
        acc_scr[...] = jnp.zeros_like(acc_scr)
        l_scr[...] = jnp.zeros_like(l_scr)
        m_scr[...] = jnp.full_like(m_scr, NEG_BIG)

    def update(h, s, vb):
        m_old = m_scr[h]
        m_new = jnp.maximum(m_old, jnp.max(s, axis=0, keepdims=True))
        p = jnp.exp2(s - m_new)
        alpha = jnp.exp2(m_old - m_new)
        l_scr[h] = alpha * l_scr[h] + jnp.sum(p, axis=0, keepdims=True)
        acc_scr[h] = alpha * acc_scr[h] + lax.dot_general(
            vb, p.astype(BF16), _TN, preferred_element_type=F32)
        m_scr[h] = m_new

    @pl.when(t < n_past)
    def _():
        pos_blk = _position_columns(tkp, t * tkp - P)
        for h in range(DA_HEADS):
            rows = pl.ds(h, tkp, stride=DA_HEADS)
            ka = jnp.concatenate([pk_ref[rows, :].astype(BF16), pos_blk], axis=1)
            half_keys = tkp // 2
            for part in range(2):
                ksl = slice(part * half_keys, (part + 1) * half_keys)
                s_scr[h, ksl, :] = jnp.dot(ka[ksl], qa_scr[h], preferred_element_type=F32)
        alphas = []
        for h in range(DA_HEADS):
            s = s_scr[h]
            m_old = m_scr[h]
            m_new = jnp.maximum(m_old, jnp.max(s, axis=0, keepdims=True))
            p = jnp.exp2(s - m_new)
            alpha = jnp.exp2(m_old - m_new)
            l_scr[h] = alpha * l_scr[h] + jnp.sum(p, axis=0, keepdims=True)
            m_scr[h] = m_new
            p_scr[h] = p.astype(BF16)
            alphas.append(alpha)
        for h in range(DA_HEADS):
            rows = pl.ds(h, tkp, stride=DA_HEADS)
            acc_scr[h] = alphas[h] * acc_scr[h] + lax.dot_general(
                pv_ref[rows, :].astype(BF16), p_scr[h], _TN, preferred_element_type=F32)

    @pl.when(t == n_past)
    def _():
        pos_blk = _position_columns(T, 0)
        kidx = lax.broadcasted_iota(jnp.int32, (T, 2 * T), 0)
        qidx = lax.broadcasted_iota(jnp.int32, (T, 2 * T), 1) & (T - 1)
        adj = _diag_adjust(kidx, qidx, P, P)
        for h in range(DA_HEADS):
            rows = pl.ds(h, T, stride=DA_HEADS)
            ka = jnp.concatenate([k_ref[rows, :].astype(BF16), pos_blk], axis=1)
            s = jnp.dot(ka, qa_scr[h], preferred_element_type=F32) + adj * _alibi_slope(h)
            update(h, s, v_ref[rows, :].astype(BF16))

        lam = _lambda(lam_ref, lam_init)
        for h in range(DA_HEADS):
            sl = slice(h * DA_V_DIM, (h + 1) * DA_V_DIM)
            a = (acc_scr[h] * (1.0 / l_scr[h])).T
            o_ref[:, sl] = _subln_gate(a[:T] - lam * a[T:], sg_ref, z_ref[:, sl], lam_init)


def _diff_attn_cache(q, z, k_new, v_new, past_k, past_v, lam_v, subln_g, B, T, lam_init, tkp=CACHE_TK):
    P = past_k.shape[1]
    assert 2 * T == LANES and T == CHUNK and P % tkp == 0 and P % CHUNK == 0
    n_past = P // tkp
    tok = lambda b, t: (b, 0)
    past = lambda b, t: (b, jnp.minimum(t, n_past - 1), 0)
    const = lambda b, t: (0, 0)
    pk = past_k.reshape(B, P * DA_HEADS, 2 * DA_HEAD_DIM)
    pv = past_v.reshape(B, P * DA_HEADS, DA_V_DIM)
    return pl.pallas_call(
        functools.partial(_diff_attn_cache_body, P=P, T=T, tkp=tkp, lam_init=lam_init),
        grid=(B, n_past + 1),
        in_specs=[
            pl.BlockSpec((T, DA_WIDTH), tok),
            pl.BlockSpec((None, tkp * DA_HEADS, LANES), past),
            pl.BlockSpec((None, tkp * DA_HEADS, LANES), past),
            pl.BlockSpec((T * DA_HEADS, LANES), tok),
            pl.BlockSpec((T * DA_HEADS, LANES), tok),
            pl.BlockSpec((T, DA_WIDTH), tok),
            pl.BlockSpec((4, DA_HEAD_DIM), const),
            pl.BlockSpec((1, DA_V_DIM), const),
        ],
        out_specs=pl.BlockSpec((T, DA_WIDTH), tok),
        out_shape=jax.ShapeDtypeStruct((B * T, DA_WIDTH), F32),
        scratch_shapes=[
            pltpu.VMEM((DA_HEADS, 2 * LANES, 2 * T), BF16),
            pltpu.VMEM((DA_HEADS, DA_V_DIM, 2 * T), F32),
            pltpu.VMEM((DA_HEADS, 1, 2 * T), F32),
            pltpu.VMEM((DA_HEADS, 1, 2 * T), F32),
            pltpu.VMEM((DA_HEADS, tkp, 2 * T), F32),
            pltpu.VMEM((DA_HEADS, tkp, 2 * T), BF16),
        ],
        compiler_params=pltpu.CompilerParams(
            dimension_semantics=("arbitrary", "arbitrary"), vmem_limit_bytes=VMEM_LIMIT),
        name="diff_attn_cache",
    )(q, pk, pv, k_new, v_new, z, lam_v, subln_g.reshape(1, DA_V_DIM))


def _trunk(x, c0, n0, m0, past_k, past_v, mem_k, mem_v, wts):
    B, T, _ = x.shape
    n_tok = B * T
    x2 = x.reshape(n_tok, D_MODEL)

    q, k, v, o, z, mq, mz, gates = _norm_matmul(
        x2, wts["norm_g"][0], [wts["w_a_main"], wts["w_a_mem"]], [[ML_WIDTH] * 5, [MEM_WIDTH] * 2],
        gates_w=wts["w_a_gates"], name="in_proj_a")
    hm, c_new, n_new, m_new = _mlstm(
        q, k, v, o, z, gates, wts["b_gate"], wts["head_g"],
        c0, n0.reshape(B, ML_HEADS, 1, ML_HEAD_DIM),
        jnp.broadcast_to(m0.reshape(B, ML_HEADS, 1, 1), (B, ML_HEADS, 1, LANES)), B, T)
    x1 = _epilogue(x2, hm, mq, mz, mem_k[0][0], mem_v[0][0], wts["w_out_a1"], wts["w_out_a2"], B, T,
                   mem_layer=mem_k[0][1], name="epilogue_a")

    kv_splits = [[DA_WIDTH, DA_WIDTH]]
    qd, zd, mq2, mz2 = _norm_matmul(x1, wts["norm_g"][1], [wts["w_b"]],
                                    [[DA_WIDTH, DA_WIDTH, MEM_WIDTH, MEM_WIDTH]], name="in_proj_b")
    lam_init = 0.8 - 0.6 * math.exp(-0.3 * 1)
    if past_k is None:
        k_new, v_new, k_heads, vt_heads = _norm_matmul(x1, wts["kv_norm_g"], [wts["w_kv"]], kv_splits,
                                                       head_major=((0, False), (1, True)),
                                                       interleave=(0, 1), name="kv_proj")
        od = _diff_attn_nocache(qd, zd, k_heads, vt_heads, wts["lam_b"], wts["subln_g"], B, T, lam_init)
    else:
        k_new, v_new = _norm_matmul(x1, wts["kv_norm_g"], [wts["w_kv"]], kv_splits, interleave=(0, 1),
                                    name="kv_proj")
        od = _diff_attn_cache(qd, zd, k_new, v_new, past_k, past_v, wts["lam_b"], wts["subln_g"], B, T,
                              lam_init)
    y = _epilogue(x1, od, mq2, mz2, mem_k[1][0], mem_v[1][0], wts["w_out_b1"], wts["w_out_b2"], B, T,
                  final_g=wts["final_norm_g"], mem_layer=mem_k[1][1], name="epilogue_b")

    return (y.reshape(B, T, D_MODEL),
            c_new.reshape(1, B, ML_HEADS, ML_HEAD_DIM, ML_HEAD_DIM),
            n_new.reshape(1, B, ML_HEADS, ML_HEAD_DIM),
            m_new[..., 0, 0].reshape(1, B, ML_HEADS),
            k_new.reshape(B, T, DA_HEADS, 2 * DA_HEAD_DIM),
            v_new.reshape(B, T, DA_HEADS, DA_V_DIM))


def kernel(x_prompt, x_sample, cache_k, cache_v, cache_mem_k, cache_mem_v, state_C, state_n, state_m, mem_prompt, norm_g, final_norm_g, mem_norm_g, w_mem_kv, w_in_a, b_gate_a, head_g_a, w_out_a, kv_norm_g, w_kv, w_in_b, lam_b, subln_g_b, w_out_b):
    B = x_prompt.shape[0]
    DB = x_sample.shape[0]
    n_gate = 2 * ML_HEADS
    g0 = 5 * ML_WIDTH
    w_a = w_in_a[0]
    w_gates = jnp.pad(w_a[:, g0:g0 + n_gate], ((0, 0), (0, LANES - n_gate)))
    w_gates_hi = w_gates.astype(BF16)
    w_gates_lo = (w_gates - w_gates_hi.astype(F32)).astype(BF16)
    wts = {
        "norm_g": norm_g,
        "final_norm_g": final_norm_g,
        "kv_norm_g": kv_norm_g,
        "w_a_main": w_a[:, :g0].astype(BF16),
        "w_a_mem": w_a[:, g0 + n_gate:].astype(BF16),
        "w_a_gates": jnp.concatenate([w_gates_hi, w_gates_lo], axis=1),
        "b_gate": jnp.pad(b_gate_a[0], (0, LANES - n_gate)).reshape(1, LANES),
        "head_g": head_g_a[0].reshape(1, ML_WIDTH),
        "w_out_a1": w_out_a[0, :ML_WIDTH].astype(BF16),
        "w_out_a2": w_out_a[0, ML_WIDTH:].astype(BF16),
        "w_kv": w_kv.astype(BF16),
        "w_b": w_in_b[0].astype(BF16),
        "lam_b": lam_b[0],
        "subln_g": subln_g_b[0],
        "w_out_b1": w_out_b[0, :DA_WIDTH].astype(BF16),
        "w_out_b2": w_out_b[0, DA_WIDTH:].astype(BF16),
    }

    mem2 = mem_prompt.reshape(B * MEM_LEN, D_MODEL)
    mks, mvs = [], []
    for l in range(2):
        mk, mv = _norm_matmul(mem2, mem_norm_g[l], [w_mem_kv[l].astype(BF16)], [[MEM_WIDTH, MEM_WIDTH]],
                              interleave=(0, 1), name="mem_kv")
        mks.append(mk.reshape(B, MEM_LEN * MEM_HEADS, MEM_HEAD_DIM))
        mvs.append(mv.reshape(B, MEM_LEN * MEM_HEADS, MEM_HEAD_DIM))
    prompt_mem_k = jnp.stack(mks).reshape(2, B, MEM_LEN, MEM_HEADS, MEM_HEAD_DIM)
    prompt_mem_v = jnp.stack(mvs).reshape(2, B, MEM_LEN, MEM_HEADS, MEM_HEAD_DIM)

    zc = jnp.zeros((B, ML_HEADS, ML_HEAD_DIM, ML_HEAD_DIM), F32)
    zn = jnp.zeros((B, ML_HEADS, ML_HEAD_DIM), F32)
    zm = jnp.zeros((B, ML_HEADS), F32)
    y_prompt, prompt_C, prompt_n, prompt_m, prompt_k, prompt_v = _trunk(
        x_prompt, zc, zn, zm, None, None, [(m, 0) for m in mks], [(m, 0) for m in mvs], wts)

    cmk = cache_mem_k.reshape(2 * DB, MEM_LEN * MEM_HEADS, MEM_HEAD_DIM)
    cmv = cache_mem_v.reshape(2 * DB, MEM_LEN * MEM_HEADS, MEM_HEAD_DIM)
    smk = [(cmk, l) for l in range(2)]
    smv = [(cmv, l) for l in range(2)]
    y_sample, sample_C, sample_n, sample_m, sample_k, sample_v = _trunk(
        x_sample, state_C[0], state_n[0], state_m[0], cache_k, cache_v, smk, smv, wts)

    return (y_prompt, y_sample, prompt_C, prompt_n, prompt_m, prompt_k, prompt_v, prompt_mem_k, prompt_mem_v,
            sample_C, sample_n, sample_m, sample_k, sample_v)
```

```python
import functools
import math

import numpy as np
import jax
import jax.numpy as jnp
from jax import lax
from jax.experimental import pallas as pl
from jax.experimental.pallas import tpu as pltpu

F32 = jnp.float32
BF16 = jnp.bfloat16

D_MODEL = 1024
CHUNK = 64
ML_HEADS = 4
ML_HEAD_DIM = 256
ML_WIDTH = 1024
DA_HEADS = 8
DA_HEAD_DIM = 64
DA_V_DIM = 128
DA_WIDTH = DA_HEADS * DA_V_DIM
MEM_LEN = 256
MEM_HEADS = 4
MEM_HEAD_DIM = 128
MEM_WIDTH = 512
EPS = 1e-6
NEG_BIG = -1e30

LANES = 128
BF16_SUBLANES = 16
VMEM_LIMIT = 56 * 1024 * 1024
ROW_TILE = 512
MAX_ROW_TILE = 1024
ROW_TILE_VMEM_BUDGET = 40 * 1024 * 1024
MLSTM_CHUNK = 256
ATTN_TQ = 512
ATTN_TK = 1024
CACHE_TK = 1024
ONES_ROWS = BF16_SUBLANES

_NT = (((1,), (1,)), ((), ()))
_TN = (((0,), (0,)), ((), ()))

_LOG2E = 1.4426950216293335
_LOG2E_BF16_PARTS = (1.4453125, -0.00262451171875, 7.033348083496094e-06)
_N_PARTS = len(_LOG2E_BF16_PARTS)


def _sigmoid(x):
    return 1.0 / (1.0 + jnp.exp(-x))


def _alibi_slope(h):
    return 2.0 ** (-8.0 * (h + 1) / DA_HEADS)


def _row_tile(n, row_bytes, resident_bytes):
    tm = min(ROW_TILE, n)
    while (2 * tm <= MAX_ROW_TILE and n % (2 * tm) == 0 and n // (2 * tm) >= 2
           and resident_bytes + 2 * (2 * tm) * row_bytes <= ROW_TILE_VMEM_BUDGET):
        tm *= 2
    return tm


def _norm_matmul_body(*refs, n_weights, splits, with_gates, head_major, interleave):
    x_ref, g_ref = refs[:2]
    w_refs = refs[2:2 + n_weights]
    rest = refs[2 + n_weights:]
    if with_gates:
        wg_ref, rest = rest[0], rest[1:]
        gate_out, rest = rest[-1], rest[:-1]
    n_out = sum(len(s) for s in splits)
    outs, hm_outs = rest[:n_out], rest[n_out:]
    x = x_ref[...]
    xn = x * lax.rsqrt(jnp.mean(x * x, axis=-1, keepdims=True) + EPS) * g_ref[...]
    xb = xn.astype(BF16)
    i = 0
    for w_ref, widths in zip(w_refs, splits):
        off = 0
        for width in widths:
            o_ref = outs[i]
            r = jnp.dot(xb, w_ref[:, off:off + width], preferred_element_type=F32)
            if i in interleave:
                nh = width // LANES
                for h in range(nh):
                    o_ref[pl.ds(h, x.shape[0], stride=nh), :] = r[:, h * LANES:(h + 1) * LANES]
            else:
                o_ref[...] = r
            for (split, transposed), hb_ref in zip(head_major, hm_outs):
                if split == i:
                    for h in range(width // LANES):
                        rh = r[:, h * LANES:(h + 1) * LANES]
                        if transposed:
                            hb_ref[h, :LANES, :] = rh.T.astype(BF16)
                            extra = lax.broadcasted_iota(jnp.int32, (ONES_ROWS, rh.shape[0]), 0) == 0
                            hb_ref[h, LANES:, :] = jnp.where(extra, 1.0, 0.0).astype(BF16)
                        else:
                            hb_ref[h] = rh.astype(BF16)
            off += width
            i += 1
    if with_gates:
        x_lo = (xn - xb.astype(F32)).astype(BF16)
        g_hi = jnp.dot(xb, wg_ref[...], preferred_element_type=F32)
        g_lo = jnp.dot(x_lo, wg_ref[:, :LANES], preferred_element_type=F32)
        gate_out[...] = g_hi[:, :LANES] + g_hi[:, LANES:] + g_lo


def _norm_matmul(x, g, weights, splits, gates_w=None, head_major=(), interleave=(), name="norm_matmul"):
    n, d = x.shape
    with_gates = gates_w is not None
    head_major = tuple(head_major)
    interleave = tuple(interleave)
    flat = [width for s in splits for width in s]
    row_bytes = 4 * (d + sum(flat) + (LANES if with_gates else 0))
    row_bytes += sum(2 * (flat[split] + (ONES_ROWS * flat[split] // LANES if tr else 0)) for split, tr in head_major)
    tm = _row_tile(n, row_bytes, sum(2 * w.size for w in weights))
    assert n % tm == 0 and all(sum(s) == w.shape[1] for s, w in zip(splits, weights))
    row = lambda i: (i, 0)
    const = lambda i: (0, 0)
    in_specs = [pl.BlockSpec((tm, d), row), pl.BlockSpec((1, d), const)]
    in_specs += [pl.BlockSpec(w.shape, const, pipeline_mode=pl.Buffered(1)) for w in weights]
    args = [x, g.reshape(1, d), *weights]
    out_shape, out_specs = [], []
    for i, width in enumerate(flat):
        rows, cols = (width // LANES, LANES) if i in interleave else (1, width)
        out_shape.append(jax.ShapeDtypeStruct((n * rows, cols), F32))
        out_specs.append(pl.BlockSpec((tm * rows, cols), row))
    for split, transposed in head_major:
        nh = flat[split] // LANES
        if transposed:
            out_shape.append(jax.ShapeDtypeStruct((nh, LANES + ONES_ROWS, n), BF16))
            out_specs.append(pl.BlockSpec((nh, LANES + ONES_ROWS, tm), lambda i: (0, 0, i)))
        else:
            out_shape.append(jax.ShapeDtypeStruct((nh, n, LANES), BF16))
            out_specs.append(pl.BlockSpec((nh, tm, LANES), lambda i: (0, i, 0)))
    if with_gates:
        in_specs.append(pl.BlockSpec((d, 2 * LANES), const))
        args.append(gates_w)
        out_shape.append(jax.ShapeDtypeStruct((n, LANES), F32))
        out_specs.append(pl.BlockSpec((tm, LANES), row))
    return pl.pallas_call(
        functools.partial(_norm_matmul_body, n_weights=len(weights),
                          splits=tuple(tuple(s) for s in splits), with_gates=with_gates,
                          head_major=head_major, interleave=interleave),
        grid=(n // tm,),
        in_specs=in_specs,
        out_specs=out_specs,
        out_shape=out_shape,
        compiler_params=pltpu.CompilerParams(
            dimension_semantics=("arbitrary",), vmem_limit_bytes=VMEM_LIMIT),
        name=name,
    )(*args)


def _mlstm_body(q_ref, k_ref, v_ref, o_ref, z_ref, gt_ref, bg_ref, hg_ref, c0_ref, n0_ref, m0_ref,
                h_out, c_out, n_out, m_out, c_s, n_s, m_s, *, L, nc):
    c = pl.program_id(1)

    @pl.when(c == 0)
    def _():
        c_s[...] = c0_ref[0]
        n_s[...] = n0_ref[0]
        m_s[...] = m0_ref[0]

    gc = gt_ref[...] + bg_ref[...]
    lane = lax.broadcasted_iota(jnp.int32, (L, LANES), 1)
    lf = jnp.minimum(gc, 0.0) - jnp.log1p(jnp.exp(-jnp.abs(gc)))
    row = lax.broadcasted_iota(jnp.int32, (L, L), 0)
    col = lax.broadcasted_iota(jnp.int32, (L, L), 1)
    tril = col <= row
    lf_hi = lf.astype(BF16)
    rem = lf - lf_hi.astype(F32)
    lf_mid = rem.astype(BF16)
    lf_lo = (rem - lf_mid.astype(F32)).astype(BF16)
    parts = jnp.dot(jnp.where(tril, 1.0, 0.0).astype(BF16), jnp.concatenate([lf_hi, lf_mid, lf_lo], axis=1),
                    preferred_element_type=F32)
    gcum = parts[:, :LANES] + parts[:, LANES:2 * LANES] + parts[:, 2 * LANES:]
    comb = jnp.where(lane < ML_HEADS, gc, gcum)
    rows = comb.T

    stash = []
    for h in range(ML_HEADS):
        sl = slice(h * ML_HEAD_DIM, (h + 1) * ML_HEAD_DIM)
        src_r = rows[h:h + 1, :] - rows[ML_HEADS + h:ML_HEADS + h + 1, :]
        g_c = comb[:, ML_HEADS + h:ML_HEADS + h + 1]
        m_prev = m_s[h][:, :1]

        dmat = jnp.where(tril, g_c + src_r, NEG_BIG)
        inter = g_c + m_prev
        m_t = jnp.maximum(inter, jnp.max(dmat, axis=-1, keepdims=True))
        w_intra = jnp.exp(dmat - m_t)
        w_inter = jnp.exp(inter - m_t)

        qb = q_ref[:, sl].astype(BF16)
        kb = (k_ref[:, sl] * (ML_HEAD_DIM ** -0.5)).astype(BF16)
        qk = lax.dot_general(qb, kb, _NT, preferred_element_type=F32)
        s = w_intra * qk
        stash.append((m_t, w_inter, s.astype(BF16), jnp.sum(s, axis=-1, keepdims=True)))

    for h in range(ML_HEADS):
        sl = slice(h * ML_HEAD_DIM, (h + 1) * ML_HEAD_DIM)
        m_t, w_inter, sb, s_sum = stash[h]
        ig_c = comb[:, h:h + 1]
        g_c = comb[:, ML_HEADS + h:ML_HEADS + h + 1]
        m_prev = m_s[h][:, :1]
        qh = q_ref[:, sl]
        kh = k_ref[:, sl] * (ML_HEAD_DIM ** -0.5)
        vh = v_ref[:, sl]
        qb = qh.astype(BF16)
        kb = kh.astype(BF16)
        vb = vh.astype(BF16)
        ch = c_s[h]
        nh = n_s[h]
        cq = lax.dot_general(qb, ch.astype(BF16), _NT, preferred_element_type=F32)
        num = w_inter * cq + jnp.dot(sb, vb, preferred_element_type=F32)
        nq = jnp.sum(qh * nh, axis=-1, keepdims=True)
        den = w_inter * nq + s_sum
        hh = num * (1.0 / jnp.maximum(jnp.abs(den), jnp.exp(-m_t)))

        g_last = g_c[L - 1:L, :]
        m_new = m_t[L - 1:L, :]
        w_s = jnp.exp(g_last - g_c + ig_c - m_new)
        dec = jnp.exp(g_last + m_prev - m_new)
        vw = (vh * w_s).astype(BF16)
        c_s[h] = dec * ch + lax.dot_general(vw, kb, _TN, preferred_element_type=F32)
        n_s[h] = dec * nh + jnp.sum(kh * w_s, axis=0, keepdims=True)
        m_s[h] = jnp.broadcast_to(m_new, (1, LANES))

        oh = o_ref[:, sl]
        zh = z_ref[:, sl]
        hm = _sigmoid(oh) * hh
        hm = hm * lax.rsqrt(jnp.mean(hm * hm, axis=-1, keepdims=True) + EPS) * hg_ref[:, sl]
        h_out[:, sl] = hm * (zh * _sigmoid(zh))

    @pl.when(c == nc - 1)
    def _():
        c_out[0] = c_s[...]
        n_out[0] = n_s[...]
        m_out[0] = m_s[...]


def _mlstm(q, k, v, o, z, gates, b_gate, head_g, c0, n0, m0, B, T):
    L = min(T, MLSTM_CHUNK)
    assert T % L == 0
    nc = T // L
    dh = ML_HEAD_DIM
    tok = lambda b, c: (b * nc + c, 0)
    st4 = lambda b, c: (b, 0, 0, 0)
    return pl.pallas_call(
        functools.partial(_mlstm_body, L=L, nc=nc),
        grid=(B, nc),
        in_specs=[pl.BlockSpec((L, ML_WIDTH), tok)] * 5 + [
            pl.BlockSpec((L, LANES), tok),
            pl.BlockSpec((1, LANES), lambda b, c: (0, 0)),
            pl.BlockSpec((1, ML_WIDTH), lambda b, c: (0, 0)),
            pl.BlockSpec((1, ML_HEADS, dh, dh), st4),
            pl.BlockSpec((1, ML_HEADS, 1, dh), st4),
            pl.BlockSpec((1, ML_HEADS, 1, LANES), st4),
        ],
        out_specs=[
            pl.BlockSpec((L, ML_WIDTH), tok),
            pl.BlockSpec((1, ML_HEADS, dh, dh), st4),
            pl.BlockSpec((1, ML_HEADS, 1, dh), st4),
            pl.BlockSpec((1, ML_HEADS, 1, LANES), st4),
        ],
        out_shape=[
            jax.ShapeDtypeStruct((B * T, ML_WIDTH), F32),
            jax.ShapeDtypeStruct((B, ML_HEADS, dh, dh), F32),
            jax.ShapeDtypeStruct((B, ML_HEADS, 1, dh), F32),
            jax.ShapeDtypeStruct((B, ML_HEADS, 1, LANES), F32),
        ],
        scratch_shapes=[
            pltpu.VMEM((ML_HEADS, dh, dh), F32),
            pltpu.VMEM((ML_HEADS, 1, dh), F32),
            pltpu.VMEM((ML_HEADS, 1, LANES), F32),
        ],
        compiler_params=pltpu.CompilerParams(
            dimension_semantics=("arbitrary", "arbitrary"), vmem_limit_bytes=VMEM_LIMIT),
        name="mlstm",
    )(q, k, v, o, z, gates, b_gate, head_g, c0, n0, m0)


def _epilogue_body(*refs, final_norm, nb, rows_per_batch):
    if final_norm:
        x_ref, a_ref, mq_ref, mz_ref, mk_ref, mv_ref, w1_ref, w2_ref, fg_ref, y_ref = refs
    else:
        x_ref, a_ref, mq_ref, mz_ref, mk_ref, mv_ref, w1_ref, w2_ref, y_ref = refs
    acc = x_ref[...] + jnp.dot(a_ref[...].astype(BF16), w1_ref[...], preferred_element_type=F32)

    def tok(bi):
        return slice(bi * rows_per_batch, (bi + 1) * rows_per_batch)

    def head_rows(h):
        return pl.ds(h, MEM_LEN, stride=MEM_HEADS)

    scores = {}
    for bi in range(nb):
        for h in range(MEM_HEADS):
            sl = slice(h * MEM_HEAD_DIM, (h + 1) * MEM_HEAD_DIM)
            qh = mq_ref[tok(bi), sl].astype(BF16)
            kh = mk_ref.at[bi][head_rows(h), :].astype(BF16)
            scores[bi, h] = lax.dot_general(qh, kh, _NT, preferred_element_type=F32) * (MEM_HEAD_DIM ** -0.5)
    mo_rows = []
    for bi in range(nb):
        mos = []
        for h in range(MEM_HEADS):
            sl = slice(h * MEM_HEAD_DIM, (h + 1) * MEM_HEAD_DIM)
            s = scores[bi, h]
            e = jnp.exp(s - jnp.max(s, axis=-1, keepdims=True))
            p = e * (1.0 / jnp.sum(e, axis=-1, keepdims=True))
            vh = mv_ref.at[bi][head_rows(h), :].astype(BF16)
            oh = jnp.dot(p.astype(BF16), vh, preferred_element_type=F32)
            zh = mz_ref[tok(bi), sl]
            mos.append((oh * (zh * _sigmoid(zh))).astype(BF16))
        mo_rows.append(jnp.concatenate(mos, axis=-1))
    mo = mo_rows[0] if nb == 1 else jnp.concatenate(mo_rows, axis=0)
    acc = acc + jnp.dot(mo, w2_ref[...], preferred_element_type=F32)
    if final_norm:
        acc = acc * lax.rsqrt(jnp.mean(acc * acc, axis=-1, keepdims=True) + EPS) * fg_ref[...]
    y_ref[...] = acc


def _epilogue(x, a, mq, mz, mem_k, mem_v, w1, w2, B, T, final_g=None, mem_layer=0, name="epilogue"):
    row_bytes = 4 * (2 * D_MODEL + a.shape[1] + 2 * MEM_WIDTH)
    tile = _row_tile(B * T, row_bytes, 2 * (w1.size + w2.size))
    if T >= tile:
        nb, rows_per_batch, tm = 1, tile, tile
    else:
        nb, rows_per_batch, tm = min(B, tile // T), T, min(B, tile // T) * T
    assert T % rows_per_batch == 0 and B % nb == 0
    nt = T // rows_per_batch
    tok = lambda b, i: (b * nt + i, 0)
    const = lambda b, i: (0, 0)
    mem_map = lambda b, i: (mem_layer * (B // nb) + b, 0, 0)
    final_norm = final_g is not None
    in_specs = [
        pl.BlockSpec((tm, D_MODEL), tok),
        pl.BlockSpec((tm, a.shape[1]), tok),
        pl.BlockSpec((tm, MEM_WIDTH), tok),
        pl.BlockSpec((tm, MEM_WIDTH), tok),
        pl.BlockSpec((nb, MEM_LEN * MEM_HEADS, MEM_HEAD_DIM), mem_map),
        pl.BlockSpec((nb, MEM_LEN * MEM_HEADS, MEM_HEAD_DIM), mem_map),
        pl.BlockSpec(w1.shape, const, pipeline_mode=pl.Buffered(1)),
        pl.BlockSpec(w2.shape, const, pipeline_mode=pl.Buffered(1)),
    ]
    args = [x, a, mq, mz, mem_k, mem_v, w1, w2]
    if final_norm:
        in_specs.append(pl.BlockSpec((1, D_MODEL), const))
        args.append(final_g.reshape(1, D_MODEL))
    return pl.pallas_call(
        functools.partial(_epilogue_body, final_norm=final_norm, nb=nb, rows_per_batch=rows_per_batch),
        grid=(B // nb, nt),
        in_specs=in_specs,
        out_specs=pl.BlockSpec((tm, D_MODEL), tok),
        out_shape=jax.ShapeDtypeStruct((B * T, D_MODEL), F32),
        compiler_params=pltpu.CompilerParams(
            dimension_semantics=("arbitrary", "arbitrary"), vmem_limit_bytes=VMEM_LIMIT),
        name=name,
    )(*args)


_FLAG_FIRST, _FLAG_LAST = 1, 2
_KIND_SHIFT = 2


def _position_columns(n_keys, rel_start):
    rel0 = lax.broadcasted_iota(jnp.int32, (n_keys, LANES), 0) + rel_start
    lane_k = lax.broadcasted_iota(jnp.int32, (n_keys, LANES), 1)
    hi = ((rel0 >> 7) << 7).astype(F32)
    lo = (rel0 & 127).astype(F32)
    return jnp.where(lane_k < _N_PARTS, hi, jnp.where(lane_k < 2 * _N_PARTS, lo, 0.0)).astype(BF16)


def _slope_rows(n_queries):
    row = lax.broadcasted_iota(jnp.int32, (LANES, n_queries), 0)
    cblk = jnp.zeros((LANES, n_queries), F32)
    for i, part in enumerate(_LOG2E_BF16_PARTS):
        cblk = jnp.where((row == i) | (row == i + _N_PARTS), part, cblk)
    return cblk


def _diag_adjust(kidx, qidx, k_abs, q_abs):
    visible = ((kidx + k_abs) >> 6) <= ((qidx + q_abs) >> 6)
    rel = kidx + k_abs - qidx - q_abs
    return jnp.where(visible, jnp.maximum(rel, 0).astype(F32) * (-2.0 * _LOG2E), NEG_BIG)


def _lambda(lam_ref, lam_init):
    lv = lam_ref[...]
    return (jnp.exp(jnp.sum(lv[0:1] * lv[1:2], axis=-1, keepdims=True))
            - jnp.exp(jnp.sum(lv[2:3] * lv[3:4], axis=-1, keepdims=True)) + lam_init)


def _subln_gate(o, sg_ref, zh, lam_init):
    o = o * lax.rsqrt(jnp.mean(o * o, axis=-1, keepdims=True) + EPS) * sg_ref[...] * (1.0 - lam_init)
    return o * (zh * _sigmoid(zh))


def _diff_attn_nocache_body(qi_ref, nj_ref, fl_ref, q_ref, k_ref, vt_ref, z_ref, lam_ref, sg_ref, o_ref,
                            qa_scr, acc_scr, m_scr, pos_scr, adj_scr, s_scr, mx_scr,
                            *, tq, tk, lam_init):
    t = pl.program_id(1)
    flags = fl_ref[t]
    q_start = qi_ref[t] * tq
    k_start = nj_ref[t] * tk

    @pl.when((flags & _FLAG_FIRST) != 0)
    def _():
        half = lax.broadcasted_iota(jnp.int32, (LANES, tq), 0) < DA_HEAD_DIM
        cblk = _slope_rows(tq)
        for h in range(DA_HEADS):
            sl = slice(h * DA_V_DIM, (h + 1) * DA_V_DIM)
            qh = (q_ref[:, sl] * (DA_HEAD_DIM ** -0.5 * _LOG2E)).T
            cb = (cblk * _alibi_slope(h)).astype(BF16)
            for c in range(2):
                keep = half if c == 0 else jnp.logical_not(half)
                qa_scr[2 * h + c, :LANES, :] = jnp.where(keep, qh, 0.0).astype(BF16)
                qa_scr[2 * h + c, LANES:, :] = cb
        acc_scr[...] = jnp.zeros_like(acc_scr)
        m_scr[...] = jnp.full_like(m_scr, NEG_BIG)

    n_sub = tk // tq
    n_maps = 2 * DA_HEADS

    def attend(modes):
        live = [sb for sb in range(n_sub) if modes[sb] is not None]
        for sb in live:
            pos_scr[sb] = _position_columns(tq, k_start + sb * tq - q_start)
            if modes[sb] == 'diag':
                kidx = lax.broadcasted_iota(jnp.int32, (tq, tq), 0)
                qidx = lax.broadcasted_iota(jnp.int32, (tq, tq), 1)
                adj_scr[...] = _diag_adjust(kidx, qidx, q_start, q_start)

        tasks = [(sb, idx) for sb in live for idx in range(n_maps)]

        def scores(n):
            sb, idx = tasks[n]
            h = idx // 2
            ka = jnp.concatenate([k_ref[h, sb * tq:(sb + 1) * tq, :], pos_scr[sb]], axis=1)
            s = jnp.dot(ka, qa_scr[idx], preferred_element_type=F32)
            if modes[sb] == 'diag':
                s = s + adj_scr[...] * _alibi_slope(h)
            s_scr[n % 2] = s
            mx_scr[n % 2] = jnp.max(s, axis=0, keepdims=True)

        def softmax(n):
            sb, idx = tasks[n]
            s = s_scr[n % 2]
            m_old = m_scr[idx]
            m_new = jnp.maximum(m_old, mx_scr[n % 2])
            p = jnp.exp2(s - m_new).astype(BF16)
            alpha = jnp.exp2(m_old - m_new)
            m_scr[idx] = m_new
            acc_scr[idx] = alpha * acc_scr[idx] + jnp.dot(
                vt_ref[idx // 2, :, sb * tq:(sb + 1) * tq], p, preferred_element_type=F32)

        scores(0)
        for n in range(len(tasks)):
            if n + 1 < len(tasks):
                scores(n + 1)
            softmax(n)

    kind = flags >> _KIND_SHIFT
    for d in range(n_sub + 1):
        modes = ['past'] * n_sub if d == n_sub else ['past'] * d + ['diag'] + [None] * (n_sub - d - 1)
        pl.when(kind == d)(functools.partial(attend, modes))

    @pl.when((flags & _FLAG_LAST) != 0)
    def _():
        lam = _lambda(lam_ref, lam_init)
        for h in range(DA_HEADS):
            sl = slice(h * DA_V_DIM, (h + 1) * DA_V_DIM)
            a1 = acc_scr[2 * h]
            a2 = acc_scr[2 * h + 1]
            ot = (a1[:DA_V_DIM] * (1.0 / a1[DA_V_DIM:DA_V_DIM + 1])
                  - lam * (a2[:DA_V_DIM] * (1.0 / a2[DA_V_DIM:DA_V_DIM + 1])))
            o_ref[:, sl] = _subln_gate(ot.T, sg_ref, z_ref[:, sl], lam_init)


def _nocache_steps(T, tq, n_sub):
    qi_t, nj_t, fl_t = [], [], []
    for qi in range(T // tq):
        last_j = qi // n_sub
        for j in range(last_j + 1):
            kind = n_sub if j < last_j else qi - j * n_sub
            qi_t.append(qi)
            nj_t.append(j)
            fl_t.append((_FLAG_FIRST if j == 0 else 0) | (_FLAG_LAST if j == last_j else 0)
                        | (kind << _KIND_SHIFT))
    as_i32 = lambda a: jnp.asarray(np.asarray(a, dtype=np.int32))
    return as_i32(qi_t), as_i32(nj_t), as_i32(fl_t), len(qi_t)


def _diff_attn_nocache(q, z, k_heads, vt_heads, lam_v, subln_g, B, T, lam_init, tq=ATTN_TQ, tk=ATTN_TK):
    tk = min(tk, T)
    assert T % tq == 0 and T % tk == 0 and tk % tq == 0 and tq % CHUNK == 0
    qi_t, nj_t, fl_t, n_steps = _nocache_steps(T, tq, tk // tq)
    nq = T // tq
    nkv = T // tk
    q_map = lambda b, t, qi, nj, fl: (b * nq + qi[t], 0)
    k_map = lambda b, t, qi, nj, fl: (0, b * nkv + nj[t], 0)
    vt_map = lambda b, t, qi, nj, fl: (0, 0, b * nkv + nj[t])
    const = lambda b, t, qi, nj, fl: (0, 0)
    grid_spec = pltpu.PrefetchScalarGridSpec(
        num_scalar_prefetch=3,
        grid=(B, n_steps),
        in_specs=[
            pl.BlockSpec((tq, DA_WIDTH), q_map),
            pl.BlockSpec((DA_HEADS, tk, LANES), k_map),
            pl.BlockSpec((DA_HEADS, DA_V_DIM + ONES_ROWS, tk), vt_map),
            pl.BlockSpec((tq, DA_WIDTH), q_map),
            pl.BlockSpec((4, DA_HEAD_DIM), const),
            pl.BlockSpec((1, DA_V_DIM), const),
        ],
        out_specs=pl.BlockSpec((tq, DA_WIDTH), q_map),
        scratch_shapes=[
            pltpu.VMEM((2 * DA_HEADS, 2 * LANES, tq), BF16),
            pltpu.VMEM((2 * DA_HEADS, DA_V_DIM + ONES_ROWS, tq), F32),
            pltpu.VMEM((2 * DA_HEADS, 1, tq), F32),
            pltpu.VMEM((tk // tq, tq, LANES), BF16),
            pltpu.VMEM((tq, tq), F32),
            pltpu.VMEM((2, tq, tq), F32),
            pltpu.VMEM((2, 1, tq), F32),
        ],
    )
    return pl.pallas_call(
        functools.partial(_diff_attn_nocache_body, tq=tq, tk=tk, lam_init=lam_init),
        grid_spec=grid_spec,
        out_shape=jax.ShapeDtypeStruct((B * T, DA_WIDTH), F32),
        compiler_params=pltpu.CompilerParams(
            dimension_semantics=("arbitrary", "arbitrary"), vmem_limit_bytes=VMEM_LIMIT),
        name="diff_attn_nocache",
    )(qi_t, nj_t, fl_t, q, k_heads, vt_heads, z, lam_v, subln_g.reshape(1, DA_V_DIM))


def _diff_attn_cache_body(q_ref, pk_ref, pv_ref, k_ref, v_ref, z_ref, lam_ref, sg_ref, o_ref,
                          qa_scr, acc_scr, m_scr, l_scr, s_scr, *, P, T, tkp, lam_init):
    t = pl.program_id(1)
    n_past = P // tkp

    @pl.when(t == 0)
    def _():
        half = lax.broadcasted_iota(jnp.int32, (T, LANES), 1) < DA_HEAD_DIM
        cblk = _slope_rows(2 * T)
        for h in range(DA_HEADS):
            sl = slice(h * DA_V_DIM, (h + 1) * DA_V_DIM)
            qh = q_ref[:, sl] * (DA_HEAD_DIM ** -0.5 * _LOG2E)
            both = jnp.concatenate([jnp.where(half, qh, 0.0), jnp.where(half, 0.0, qh)], axis=0)
            qa_scr[h, :LANES, :] = both.T.astype(BF16)
            qa_scr[h, LANES:, :] = (cblk * _alibi_slope(h)).astype(BF16)
        acc_scr[...] = jnp.zeros_like(acc_scr)
        l_scr[...] = jnp.zeros_like(l_scr)
        m_scr[...] = jnp.full_like(m_scr, NEG_BIG)

    def update(h, s, vb):
        m_old = m_scr[h]
        m_new = jnp.maximum(m_old, jnp.max(s, axis=0, keepdims=True))
        p = jnp.exp2(s - m_new)
        alpha = jnp.exp2(m_old - m_new)
        l_scr[h] = alpha * l_scr[h] + jnp.sum(p, axis=0, keepdims=True)
        acc_scr[h] = alpha * acc_scr[h] + lax.dot_general(
            vb, p.astype(BF16), _TN, preferred_element_type=F32)
        m_scr[h] = m_new

    @pl.when(t < n_past)
    def _():
        pos_blk = _position_columns(tkp, t * tkp - P)
        for h in range(DA_HEADS):
            rows = pl.ds(h, tkp, stride=DA_HEADS)
            ka = jnp.concatenate([pk_ref[rows, :].astype(BF16), pos_blk], axis=1)
            s_scr[h] = jnp.dot(ka, qa_scr[h], preferred_element_type=F32)
        for h in range(DA_HEADS):
            rows = pl.ds(h, tkp, stride=DA_HEADS)
            update(h, s_scr[h], pv_ref[rows, :].astype(BF16))

    @pl.when(t == n_past)
    def _():
        pos_blk = _position_columns(T, 0)
        kidx = lax.broadcasted_iota(jnp.int32, (T, 2 * T), 0)
        qidx = lax.broadcasted_iota(jnp.int32, (T, 2 * T), 1) & (T - 1)
        adj = _diag_adjust(kidx, qidx, P, P)
        for h in range(DA_HEADS):
            rows = pl.ds(h, T, stride=DA_HEADS)
            ka = jnp.concatenate([k_ref[rows, :].astype(BF16), pos_blk], axis=1)
            s = jnp.dot(ka, qa_scr[h], preferred_element_type=F32) + adj * _alibi_slope(h)
            update(h, s, v_ref[rows, :].astype(BF16))

        lam = _lambda(lam_ref, lam_init)
        for h in range(DA_HEADS):
            sl = slice(h * DA_V_DIM, (h + 1) * DA_V_DIM)
            a = (acc_scr[h] * (1.0 / l_scr[h])).T
            o_ref[:, sl] = _subln_gate(a[:T] - lam * a[T:], sg_ref, z_ref[:, sl], lam_init)


def _diff_attn_cache(q, z, k_new, v_new, past_k, past_v, lam_v, subln_g, B, T, lam_init, tkp=CACHE_TK):
    P = past_k.shape[1]
    assert 2 * T == LANES and T == CHUNK and P % tkp == 0 and P % CHUNK == 0
    n_past = P // tkp
    tok = lambda b, t: (b, 0)
    past = lambda b, t: (b, jnp.minimum(t, n_past - 1), 0)
    const = lambda b, t: (0, 0)
    pk = past_k.reshape(B, P * DA_HEADS, 2 * DA_HEAD_DIM)
    pv = past_v.reshape(B, P * DA_HEADS, DA_V_DIM)
    return pl.pallas_call(
        functools.partial(_diff_attn_cache_body, P=P, T=T, tkp=tkp, lam_init=lam_init),
        grid=(B, n_past + 1),
        in_specs=[
            pl.BlockSpec((T, DA_WIDTH), tok),
            pl.BlockSpec((None, tkp * DA_HEADS, LANES), past),
            pl.BlockSpec((None, tkp * DA_HEADS, LANES), past),
            pl.BlockSpec((T * DA_HEADS, LANES), tok),
            pl.BlockSpec((T * DA_HEADS, LANES), tok),
            pl.BlockSpec((T, DA_WIDTH), tok),
            pl.BlockSpec((4, DA_HEAD_DIM), const),
            pl.BlockSpec((1, DA_V_DIM), const),
        ],
        out_specs=pl.BlockSpec((T, DA_WIDTH), tok),
        out_shape=jax.ShapeDtypeStruct((B * T, DA_WIDTH), F32),
        scratch_shapes=[
            pltpu.VMEM((DA_HEADS, 2 * LANES, 2 * T), BF16),
            pltpu.VMEM((DA_HEADS, DA_V_DIM, 2 * T), F32),
            pltpu.VMEM((DA_HEADS, 1, 2 * T), F32),
            pltpu.VMEM((DA_HEADS, 1, 2 * T), F32),
            pltpu.VMEM((DA_HEADS, tkp, 2 * T), F32),
        ],
        compiler_params=pltpu.CompilerParams(
            dimension_semantics=("arbitrary", "arbitrary"), vmem_limit_bytes=VMEM_LIMIT),
        name="diff_attn_cache",
    )(q, pk, pv, k_new, v_new, z, lam_v, subln_g.reshape(1, DA_V_DIM))


def _trunk(x, c0, n0, m0, past_k, past_v, mem_k, mem_v, wts):
    B, T, _ = x.shape
    n_tok = B * T
    x2 = x.reshape(n_tok, D_MODEL)

    q, k, v, o, z, mq, mz, gates = _norm_matmul(
        x2, wts["norm_g"][0], [wts["w_a_main"], wts["w_a_mem"]], [[ML_WIDTH] * 5, [MEM_WIDTH] * 2],
        gates_w=wts["w_a_gates"], name="in_proj_a")
    hm, c_new, n_new, m_new = _mlstm(
        q, k, v, o, z, gates, wts["b_gate"], wts["head_g"],
        c0, n0.reshape(B, ML_HEADS, 1, ML_HEAD_DIM),
        jnp.broadcast_to(m0.reshape(B, ML_HEADS, 1, 1), (B, ML_HEADS, 1, LANES)), B, T)
    x1 = _epilogue(x2, hm, mq, mz, mem_k[0][0], mem_v[0][0], wts["w_out_a1"], wts["w_out_a2"], B, T,
                   mem_layer=mem_k[0][1], name="epilogue_a")

    kv_splits = [[DA_WIDTH, DA_WIDTH]]
    qd, zd, mq2, mz2 = _norm_matmul(x1, wts["norm_g"][1], [wts["w_b"]],
                                    [[DA_WIDTH, DA_WIDTH, MEM_WIDTH, MEM_WIDTH]], name="in_proj_b")
    lam_init = 0.8 - 0.6 * math.exp(-0.3 * 1)
    if past_k is None:
        k_new, v_new, k_heads, vt_heads = _norm_matmul(x1, wts["kv_norm_g"], [wts["w_kv"]], kv_splits,
                                                       head_major=((0, False), (1, True)),
                                                       interleave=(0, 1), name="kv_proj")
        od = _diff_attn_nocache(qd, zd, k_heads, vt_heads, wts["lam_b"], wts["subln_g"], B, T, lam_init)
    else:
        k_new, v_new = _norm_matmul(x1, wts["kv_norm_g"], [wts["w_kv"]], kv_splits, interleave=(0, 1),
                                    name="kv_proj")
        od = _diff_attn_cache(qd, zd, k_new, v_new, past_k, past_v, wts["lam_b"], wts["subln_g"], B, T,
                              lam_init)
    y = _epilogue(x1, od, mq2, mz2, mem_k[1][0], mem_v[1][0], wts["w_out_b1"], wts["w_out_b2"], B, T,
                  final_g=wts["final_norm_g"], mem_layer=mem_k[1][1], name="epilogue_b")

    return (y.reshape(B, T, D_MODEL),
            c_new.reshape(1, B, ML_HEADS, ML_HEAD_DIM, ML_HEAD_DIM),
            n_new.reshape(1, B, ML_HEADS, ML_HEAD_DIM),
            m_new[..., 0, 0].reshape(1, B, ML_HEADS),
            k_new.reshape(B, T, DA_HEADS, 2 * DA_HEAD_DIM),
            v_new.reshape(B, T, DA_HEADS, DA_V_DIM))


def kernel(x_prompt, x_sample, cache_k, cache_v, cache_mem_k, cache_mem_v, state_C, state_n, state_m, mem_prompt, norm_g, final_norm_g, mem_norm_g, w_mem_kv, w_in_a, b_gate_a, head_g_a, w_out_a, kv_norm_g, w_kv, w_in_b, lam_b, subln_g_b, w_out_b):
    B = x_prompt.shape[0]
    DB = x_sample.shape[0]
    n_gate = 2 * ML_HEADS
    g0 = 5 * ML_WIDTH
    w_a = w_in_a[0]
    w_gates = jnp.pad(w_a[:, g0:g0 + n_gate], ((0, 0), (0, LANES - n_gate)))
    w_gates_hi = w_gates.astype(BF16)
    w_gates_lo = (w_gates - w_gates_hi.astype(F32)).astype(BF16)
    wts = {
        "norm_g": norm_g,
        "final_norm_g": final_norm_g,
        "kv_norm_g": kv_norm_g,
        "w_a_main": w_a[:, :g0].astype(BF16),
        "w_a_mem": w_a[:, g0 + n_gate:].astype(BF16),
        "w_a_gates": jnp.concatenate([w_gates_hi, w_gates_lo], axis=1),
        "b_gate": jnp.pad(b_gate_a[0], (0, LANES - n_gate)).reshape(1, LANES),
        "head_g": head_g_a[0].reshape(1, ML_WIDTH),
        "w_out_a1": w_out_a[0, :ML_WIDTH].astype(BF16),
        "w_out_a2": w_out_a[0, ML_WIDTH:].astype(BF16),
        "w_kv": w_kv.astype(BF16),
        "w_b": w_in_b[0].astype(BF16),
        "lam_b": lam_b[0],
        "subln_g": subln_g_b[0],
        "w_out_b1": w_out_b[0, :DA_WIDTH].astype(BF16),
        "w_out_b2": w_out_b[0, DA_WIDTH:].astype(BF16),
    }

    mem2 = mem_prompt.reshape(B * MEM_LEN, D_MODEL)
    mks, mvs = [], []
    for l in range(2):
        mk, mv = _norm_matmul(mem2, mem_norm_g[l], [w_mem_kv[l].astype(BF16)], [[MEM_WIDTH, MEM_WIDTH]],
                              interleave=(0, 1), name="mem_kv")
        mks.append(mk.reshape(B, MEM_LEN * MEM_HEADS, MEM_HEAD_DIM))
        mvs.append(mv.reshape(B, MEM_LEN * MEM_HEADS, MEM_HEAD_DIM))
    prompt_mem_k = jnp.stack(mks).reshape(2, B, MEM_LEN, MEM_HEADS, MEM_HEAD_DIM)
    prompt_mem_v = jnp.stack(mvs).reshape(2, B, MEM_LEN, MEM_HEADS, MEM_HEAD_DIM)

    zc = jnp.zeros((B, ML_HEADS, ML_HEAD_DIM, ML_HEAD_DIM), F32)
    zn = jnp.zeros((B, ML_HEADS, ML_HEAD_DIM), F32)
    zm = jnp.zeros((B, ML_HEADS), F32)
    y_prompt, prompt_C, prompt_n, prompt_m, prompt_k, prompt_v = _trunk(
        x_prompt, zc, zn, zm, None, None, [(m, 0) for m in mks], [(m, 0) for m in mvs], wts)

    cmk = cache_mem_k.reshape(2 * DB, MEM_LEN * MEM_HEADS, MEM_HEAD_DIM)
    cmv = cache_mem_v.reshape(2 * DB, MEM_LEN * MEM_HEADS, MEM_HEAD_DIM)
    smk = [(cmk, l) for l in range(2)]
    smv = [(cmv, l) for l in range(2)]
    y_sample, sample_C, sample_n, sample_m, sample_k, sample_v = _trunk(
        x_sample, state_C[0], state_n[0], state_m[0], cache_k, cache_v, smk, smv, wts)

    return (y_prompt, y_sample, prompt_C, prompt_n, prompt_m, prompt_k, prompt_v, prompt_mem_k, prompt_mem_v,
            sample_C, sample_n, sample_m, sample_k, sample_v)
```
